```python
import jax, jax.numpy as jnp
from jax import lax
import numpy as np

D_MODEL = 1024
BATCH = 8
SEQ = 2048
DEPTH = 2

CHUNK = 64
N_PREV_CHUNKS = 8
BAND = (N_PREV_CHUNKS + 1) * CHUNK
A_HEADS = 8
A_HEAD_DIM = 64
A_WIDTH = A_HEADS * A_HEAD_DIM
REL_CLIP = 128
B_HEADS = 4
B_KEY_DIM = 64
B_VAL_DIM = 128
B_K_WIDTH = B_HEADS * B_KEY_DIM
B_V_WIDTH = B_HEADS * B_VAL_DIM
GATE_RANK = 16
GATE_TEMP = 16.0
D_FF = 2816
N_EXPERTS = 8
TOP_K = 2
D_FF_EXPERT = 3584
PLE_DIM = 256
NORM_EPS = 1e-6
N_DENSE = (DEPTH + 1) // 2
N_MOE = DEPTH // 2
IN_SPLITS = (A_WIDTH, A_WIDTH, A_WIDTH, B_K_WIDTH, B_K_WIDTH, B_V_WIDTH, GATE_RANK, B_V_WIDTH, D_MODEL, D_MODEL)
IN_WIDTH = 3 * A_WIDTH + 2 * B_K_WIDTH + 2 * B_V_WIDTH + GATE_RANK + 2 * D_MODEL

kernel_name = "hybrid_chunk_attn_gla_moe_block"


def rmsnorm(x, g):
    xf = x.astype(jnp.float32)
    y = xf * lax.rsqrt(jnp.mean(xf * xf, axis=-1, keepdims=True) + NORM_EPS)
    return (y * g.astype(jnp.float32)).astype(x.dtype)


def chunk_band_attention(q, k, v, rel_bias):
    b, s, h, d = q.shape
    nc = s // CHUNK
    qc = q.reshape(b, nc, CHUNK, h, d) * (d ** -0.5)

    def band(t):
        tc = t.reshape(b, nc, CHUNK, h, d)
        tp = jnp.pad(tc, ((0, 0), (N_PREV_CHUNKS, 0), (0, 0), (0, 0), (0, 0)))
        return jnp.concatenate([tp[:, j:j + nc] for j in range(N_PREV_CHUNKS + 1)], axis=2)

    kb, vb = band(k), band(v)
    scores = jnp.einsum('bnqhd,bnkhd->bnhqk', qc, kb).astype(jnp.float32)
    dist = np.arange(CHUNK)[:, None] + N_PREV_CHUNKS * CHUNK - np.arange(BAND)[None, :]
    rel_idx = np.clip(dist, -REL_CLIP, REL_CLIP) + REL_CLIP
    bias = rel_bias.astype(jnp.float32)[:, rel_idx]
    key_chunk = np.arange(nc)[:, None] - N_PREV_CHUNKS + np.arange(BAND)[None, :] // CHUNK
    valid = key_chunk >= 0
    scores = jnp.where(valid[None, :, None, None, :], scores + bias[None, None], -1e30)
    probs = jax.nn.softmax(scores, axis=-1).astype(v.dtype)
    out = jnp.einsum('bnhqk,bnkhd->bnqhd', probs, vb)
    return out.reshape(b, s, h * d)


def gla_chunk_state(q, k, v, log_a):
    b, s, h, dk = q.shape
    dv = v.shape[-1]
    nc = s // CHUNK
    f32 = jnp.float32
    la = log_a.reshape(b, nc, CHUNK, h, dk)
    cum = jnp.cumsum(la, axis=2)
    tot = cum[:, :, -1]
    kdec = k.reshape(b, nc, CHUNK, h, dk).astype(f32) * jnp.exp(tot[:, :, None] - cum)
    d_state = jnp.einsum('bnlhk,bnlhv->nbhkv', kdec, v.reshape(b, nc, CHUNK, h, dv).astype(f32))
    decay = jnp.exp(tot).transpose(1, 0, 2, 3)

    def step(state, inp):
        dec, ds = inp
        state = dec[..., None] * state + ds
        return state, state

    _, states = lax.scan(step, jnp.zeros((b, h, dk, dv), f32), (decay, d_state))
    out = jnp.einsum('bnlhk,nbhkv->bnlhv', q.reshape(b, nc, CHUNK, h, dk).astype(f32), states)
    return out.reshape(b, s, h, dv)


def token_mixer(h, w_in, rel_bias, w_gla_gate_up, b_gla_gate, gla_norm, w_branch_a, w_branch_b, w_out):
    b, s, _ = h.shape
    proj = h @ w_in
    offsets = [int(o) for o in np.cumsum(IN_SPLITS)[:-1]]
    qa, ka, va, qb, kb, vb, g_down, r, ga, gb = jnp.split(proj, offsets, axis=-1)
    ya = chunk_band_attention(qa.reshape(b, s, A_HEADS, A_HEAD_DIM),
                              ka.reshape(b, s, A_HEADS, A_HEAD_DIM),
                              va.reshape(b, s, A_HEADS, A_HEAD_DIM), rel_bias)
    log_a = jax.nn.log_sigmoid((g_down @ w_gla_gate_up + b_gla_gate).astype(jnp.float32)) / GATE_TEMP
    ob = gla_chunk_state(qb.reshape(b, s, B_HEADS, B_KEY_DIM) * (B_KEY_DIM ** -0.5),
                         kb.reshape(b, s, B_HEADS, B_KEY_DIM),
                         vb.reshape(b, s, B_HEADS, B_VAL_DIM),
                         log_a.reshape(b, s, B_HEADS, B_KEY_DIM))
    ob = rmsnorm(ob, gla_norm.reshape(B_HEADS, B_VAL_DIM)).reshape(b, s, B_V_WIDTH).astype(h.dtype)
    yb = ob * jax.nn.silu(r)
    merged = jax.nn.sigmoid(ga) * (ya @ w_branch_a) + jax.nn.sigmoid(gb) * (yb @ w_branch_b)
    return merged @ w_out


def swiglu(h, w_gate, w_up, w_down):
    return (jax.nn.silu(h @ w_gate) * (h @ w_up)) @ w_down


def moe_swiglu(h, router_w, w_gate, w_up, w_down):
    logits = (h @ router_w).astype(jnp.float32)
    top_vals, top_idx = lax.top_k(logits, TOP_K)
    top_w = jax.nn.softmax(top_vals, axis=-1)
    combine = jnp.sum(jax.nn.one_hot(top_idx, N_EXPERTS, dtype=jnp.float32) * top_w[..., None], axis=-2)
    combine = combine.astype(h.dtype)
    out = jnp.zeros_like(h)
    for e in range(N_EXPERTS):
        out = out + combine[..., e:e + 1] * swiglu(h, w_gate[e], w_up[e], w_down[e])
    return out


def setup_inputs(seed: int = 0) -> dict:
    key = jax.random.key(seed)
    ks = jax.random.split(key, 32)
    f32 = jnp.float32

    def nrm(k, shape, fan_in):
        return jax.random.normal(k, shape, f32) * (fan_in ** -0.5)

    def gain(k, shape):
        return 1.0 + 0.05 * jax.random.normal(k, shape, f32)

    return {
        "x": jax.random.normal(ks[0], (BATCH, SEQ, D_MODEL), f32),
        "p": jax.random.normal(ks[1], (DEPTH, BATCH, SEQ, PLE_DIM), f32),
        "w_in": nrm(ks[2], (DEPTH, D_MODEL, IN_WIDTH), D_MODEL),
        "rel_bias": 0.1 * jax.random.normal(ks[3], (DEPTH, A_HEADS, 2 * REL_CLIP + 1), f32),
        "w_gla_gate_up": nrm(ks[4], (DEPTH, GATE_RANK, B_K_WIDTH), GATE_RANK),
        "b_gla_gate": 0.1 * jax.random.normal(ks[5], (DEPTH, B_K_WIDTH), f32),
        "gla_norm": gain(ks[6], (DEPTH, B_V_WIDTH)),
        "w_branch_a": nrm(ks[7], (DEPTH, A_WIDTH, D_MODEL), A_WIDTH),
        "w_branch_b": nrm(ks[8], (DEPTH, B_V_WIDTH, D_MODEL), B_V_WIDTH),
        "w_out": nrm(ks[9], (DEPTH, D_MODEL, D_MODEL), D_MODEL),
        "norm_mix_pre": gain(ks[10], (DEPTH, D_MODEL)),
        "norm_mix_post": gain(ks[11], (DEPTH, D_MODEL)),
        "norm_ffn_pre": gain(ks[12], (DEPTH, D_MODEL)),
        "norm_ffn_post": gain(ks[13], (DEPTH, D_MODEL)),
        "ffn_w_gate": nrm(ks[14], (N_DENSE, D_MODEL, D_FF), D_MODEL),
        "ffn_w_up": nrm(ks[15], (N_DENSE, D_MODEL, D_FF), D_MODEL),
        "ffn_w_down": nrm(ks[16], (N_DENSE, D_FF, D_MODEL), D_FF),
        "router_w": nrm(ks[17], (N_MOE, D_MODEL, N_EXPERTS), D_MODEL),
        "moe_w_gate": nrm(ks[18], (N_MOE, N_EXPERTS, D_MODEL, D_FF_EXPERT), D_MODEL),
        "moe_w_up": nrm(ks[19], (N_MOE, N_EXPERTS, D_MODEL, D_FF_EXPERT), D_MODEL),
        "moe_w_down": nrm(ks[20], (N_MOE, N_EXPERTS, D_FF_EXPERT, D_MODEL), D_FF_EXPERT),
        "ple_w_proj": nrm(ks[21], (DEPTH, PLE_DIM, D_MODEL), PLE_DIM),
        "ple_w_gate": nrm(ks[22], (DEPTH, D_MODEL, D_MODEL), D_MODEL),
        "ple_norm": gain(ks[23], (DEPTH, D_MODEL)),
    }


def reference(x, p, w_in, rel_bias, w_gla_gate_up, b_gla_gate, gla_norm, w_branch_a, w_branch_b, w_out,
              norm_mix_pre, norm_mix_post, norm_ffn_pre, norm_ffn_post,
              ffn_w_gate, ffn_w_up, ffn_w_down, router_w, moe_w_gate, moe_w_up, moe_w_down,
              ple_w_proj, ple_w_gate, ple_norm):
    for i in range(DEPTH):
        h = rmsnorm(x, norm_mix_pre[i])
        y = token_mixer(h, w_in[i], rel_bias[i], w_gla_gate_up[i], b_gla_gate[i], gla_norm[i],
                        w_branch_a[i], w_branch_b[i], w_out[i])
        x = x + rmsnorm(y, norm_mix_post[i])
        h = rmsnorm(x, norm_ffn_pre[i])
        j = i // 2
        if i % 2 == 0:
            y = swiglu(h, ffn_w_gate[j], ffn_w_up[j], ffn_w_down[j])
        else:
            y = moe_swiglu(h, router_w[j], moe_w_gate[j], moe_w_up[j], moe_w_down[j])
        x = x + rmsnorm(y, norm_ffn_post[i])
        e = (p[i] @ ple_w_proj[i]) * jax.nn.sigmoid(x @ ple_w_gate[i])
        x = x + rmsnorm(e, ple_norm[i])
    return x
```

```python
import functools

import numpy as np
import jax
import jax.numpy as jnp
from jax import lax
from jax.experimental import pallas as pl
from jax.experimental.pallas import tpu as pltpu

f32 = jnp.float32
bf16 = jnp.bfloat16

D_MODEL = 1024
CHUNK = 64
N_PREV_CHUNKS = 8
A_HEADS = 8
A_HEAD_DIM = 64
A_WIDTH = A_HEADS * A_HEAD_DIM
REL_CLIP = 128
B_HEADS = 4
B_KEY_DIM = 64
B_VAL_DIM = 128
B_K_WIDTH = B_HEADS * B_KEY_DIM
B_V_WIDTH = B_HEADS * B_VAL_DIM
GATE_RANK = 16
GATE_TEMP = 16.0
N_EXPERTS = 8
PLE_DIM = 256
NORM_EPS = 1e-6

LANES = 128
VMEM_LIMIT = 56 * 1024 * 1024

COL_QA, COL_KA, COL_VA = 0, 512, 1024
COL_QB, COL_KB, COL_VB = 1536, 1792, 2048
COL_R, COL_GA, COL_GB, COL_GD = 2560, 3072, 4096, 5120
IN_WIDTH_PAD = 5376
IN_TILE_N = 1792

QBLK = 2 * CHUNK
KBAND = (N_PREV_CHUNKS + 2) * CHUNK
KPAD = N_PREV_CHUNKS * CHUNK

MOE_TM = 512
ROW_BLK = 512
FF_CHUNK = 512


def _params(*sem):
    return pltpu.CompilerParams(dimension_semantics=sem, vmem_limit_bytes=VMEM_LIMIT)


def _rms(x, g):
    return x * lax.rsqrt(jnp.mean(x * x, axis=-1, keepdims=True) + NORM_EPS) * g


def _sigmoid(x):
    return 1.0 / (1.0 + jnp.exp(-x))


def _silu(x):
    return x * _sigmoid(x)


def _chunks(n, c):
    return [(s, min(s + c, n)) for s in range(0, n, c)]


def _const_spec(shape):
    nd = len(shape)
    return pl.BlockSpec(shape, lambda *_: (0,) * nd, pipeline_mode=pl.Buffered(1))


def _in_proj_kernel(x_ref, g_ref, w_ref, o_ref, h_ref):
    @pl.when(pl.program_id(1) == 0)
    def _():
        h_ref[...] = _rms(x_ref[...], g_ref[...]).astype(bf16)

    o_ref[...] = jnp.dot(h_ref[...], w_ref[...], preferred_element_type=f32).astype(o_ref.dtype)


def _in_proj(x, g, w, tm=1024):
    t = x.shape[0]
    n = w.shape[1]
    return pl.pallas_call(
        _in_proj_kernel,
        grid=(t // tm, n // IN_TILE_N),
        in_specs=[pl.BlockSpec((tm, D_MODEL), lambda i, j: (i, 0)),
                  pl.BlockSpec((1, D_MODEL), lambda i, j: (0, 0)),
                  pl.BlockSpec((D_MODEL, IN_TILE_N), lambda i, j: (0, j))],
        out_specs=pl.BlockSpec((tm, IN_TILE_N), lambda i, j: (i, j)),
        out_shape=jax.ShapeDtypeStruct((t, n), bf16),
        scratch_shapes=[pltpu.VMEM((tm, D_MODEL), bf16)],
        compiler_params=_params("parallel", "arbitrary"),
        name="in_proj",
    )(x, g, w)


def _attn_kernel(q_ref, k_ref, v_ref, bias_ref, o_ref, kpad, vpad):
    i = pl.program_id(1)
    seq = k_ref.shape[1]

    @pl.when(i == 0)
    def _():
        zeros = jnp.zeros((KPAD, A_WIDTH), bf16)
        kpad[0:KPAD, :] = zeros
        vpad[0:KPAD, :] = zeros
        kpad[KPAD:KPAD + seq, :] = k_ref[0]
        vpad[KPAD:KPAD + seq, :] = v_ref[0]

    start = pl.multiple_of(i * QBLK, QBLK)
    lane = lax.broadcasted_iota(jnp.int32, (QBLK, LANES), 1)
    upper = lane >= A_HEAD_DIM
    kcol = lax.broadcasted_iota(jnp.int32, (QBLK, KBAND), 1)
    in_seq = kcol >= KPAD - i * QBLK
    for hp in range(A_WIDTH // LANES):
        cols = slice(hp * LANES, (hp + 1) * LANES)
        qp = q_ref[0, :, cols] * (A_HEAD_DIM ** -0.5)
        kp = kpad[pl.ds(start, KBAND), cols]
        vp = vpad[pl.ds(start, KBAND), cols]
        outs = []
        for s in range(2):
            qm = jnp.where(upper == bool(s), qp, jnp.zeros_like(qp))
            sc = lax.dot_general(qm, kp, (((1,), (1,)), ((), ())), preferred_element_type=f32)
            sc = sc + bias_ref[2 * hp + s]
            sc = jnp.where(in_seq, sc, -1e30)
            m = jnp.max(sc, axis=-1, keepdims=True)
            p = jnp.exp(sc - m)
            l = jnp.sum(p, axis=-1, keepdims=True)
            o = jnp.dot(p.astype(bf16), vp, preferred_element_type=f32)
            outs.append(o / l)
        o_ref[0, :, cols] = jnp.where(upper, outs[1], outs[0]).astype(o_ref.dtype)


def _attention(proj3, bias):
    b, s, _ = proj3.shape
    return pl.pallas_call(
        _attn_kernel,
        grid=(b, s // QBLK),
        in_specs=[pl.BlockSpec((1, QBLK, A_WIDTH), lambda bi, i: (bi, i, COL_QA // A_WIDTH)),
                  pl.BlockSpec((1, s, A_WIDTH), lambda bi, i: (bi, 0, COL_KA // A_WIDTH)),
                  pl.BlockSpec((1, s, A_WIDTH), lambda bi, i: (bi, 0, COL_VA // A_WIDTH)),
                  _const_spec((A_HEADS, QBLK, KBAND))],
        out_specs=pl.BlockSpec((1, QBLK, A_WIDTH), lambda bi, i: (bi, i, 0)),
        out_shape=jax.ShapeDtypeStruct((b, s, A_WIDTH), bf16),
        scratch_shapes=[pltpu.VMEM((KPAD + s, A_WIDTH), bf16),
                        pltpu.VMEM((KPAD + s, A_WIDTH), bf16)],
        compiler_params=_params("parallel", "arbitrary"),
        name="chunk_attention",
    )(proj3, proj3, proj3, bias)


def _attn_bias(rel_bias):
    r = np.arange(QBLK)[:, None]
    k = np.arange(KBAND)[None, :]
    rel_idx = np.clip(r + KPAD - k, -REL_CLIP, REL_CLIP) + REL_CLIP
    first = r < CHUNK
    visible = np.where(first, k < KBAND - CHUNK, k >= CHUNK)
    bias = rel_bias.astype(f32)[:, rel_idx]
    return jnp.where(visible[None], bias, -1e30)


def _gla_kernel(q_ref, k_ref, v_ref, gd_ref, r_ref, wup_ref, b_ref, gn_ref, o_ref, state):
    @pl.when(pl.program_id(1) == 0)
    def _():
        state[...] = jnp.zeros_like(state)

    tq = q_ref.shape[1]
    row = lax.broadcasted_iota(jnp.int32, (CHUNK, CHUNK), 0)
    col = lax.broadcasted_iota(jnp.int32, (CHUNK, CHUNK), 1)
    later = (col > row).astype(f32)
    ones = jnp.ones((CHUNK, LANES), f32)
    lane = lax.broadcasted_iota(jnp.int32, (CHUNK, LANES), 1)
    upper = lane >= B_KEY_DIM
    tn = (((0,), (0,)), ((), ()))
    hi = lax.Precision.HIGHEST
    for c in range(tq // CHUNK):
        rows = slice(c * CHUNK, (c + 1) * CHUNK)
        z = jnp.dot(gd_ref[0, rows, :], wup_ref[...], preferred_element_type=f32) + b_ref[...]
        log_a = (jnp.minimum(z, 0.0) - jnp.log(1.0 + jnp.exp(-jnp.abs(z)))) / GATE_TEMP
        rev = jnp.dot(later, log_a, preferred_element_type=f32, precision=hi)
        kdec = k_ref[0, rows, :].astype(f32) * jnp.exp(rev)
        for pr in range(B_HEADS // 2):
            cols = slice(pr * LANES, (pr + 1) * LANES)
            tot = lax.dot_general(log_a[:, cols], ones, tn, preferred_element_type=f32, precision=hi)
            kd = kdec[:, cols]
            d_state = None
            for s in range(2):
                h = 2 * pr + s
                kdm = jnp.where(upper == bool(s), kd, 0.0).astype(bf16)
                vh = v_ref[0, rows, h * B_VAL_DIM:(h + 1) * B_VAL_DIM]
                part = lax.dot_general(kdm, vh, tn, preferred_element_type=f32)
                d_state = part if d_state is None else d_state + part
            st = jnp.exp(tot) * state[pr] + d_state
            state[pr] = st
            stb = st.astype(bf16)
            qp = q_ref[0, rows, cols] * (B_KEY_DIM ** -0.5)
            for s in range(2):
                h = 2 * pr + s
                vcols = slice(h * B_VAL_DIM, (h + 1) * B_VAL_DIM)
                qm = jnp.where(upper == bool(s), qp, jnp.zeros_like(qp))
                o = jnp.dot(qm, stb, preferred_element_type=f32)
                o = _rms(o, gn_ref[:, vcols])
                r = r_ref[0, rows, vcols].astype(f32)
                o_ref[0, rows, vcols] = (o * _silu(r)).astype(o_ref.dtype)


def _gla(proj3, w_up, b_gate, gla_norm, tq=256):
    b, s, _ = proj3.shape
    return pl.pallas_call(
        _gla_kernel,
        grid=(b, s // tq),
        in_specs=[pl.BlockSpec((1, tq, B_K_WIDTH), lambda bi, i: (bi, i, COL_QB // B_K_WIDTH)),
                  pl.BlockSpec((1, tq, B_K_WIDTH), lambda bi, i: (bi, i, COL_KB // B_K_WIDTH)),
                  pl.BlockSpec((1, tq, B_V_WIDTH), lambda bi, i: (bi, i, COL_VB // B_V_WIDTH)),
                  pl.BlockSpec((1, tq, LANES), lambda bi, i: (bi, i, COL_GD // LANES)),
                  pl.BlockSpec((1, tq, B_V_WIDTH), lambda bi, i: (bi, i, COL_R // B_V_WIDTH)),
                  _const_spec((LANES, B_K_WIDTH)),
                  _const_spec((1, B_K_WIDTH)),
                  _const_spec((1, B_V_WIDTH))],
        out_specs=pl.BlockSpec((1, tq, B_V_WIDTH), lambda bi, i: (bi, i, 0)),
        out_shape=jax.ShapeDtypeStruct((b, s, B_V_WIDTH), bf16),
        scratch_shapes=[pltpu.VMEM((B_HEADS // 2, LANES, B_VAL_DIM), f32)],
        compiler_params=_params("parallel", "arbitrary"),
        name="gla",
    )(proj3, proj3, proj3, proj3, proj3, w_up, b_gate, gla_norm)


def _route_top2(h, rw_ref):
    logits = jnp.dot(h, rw_ref[...], preferred_element_type=f32, precision=lax.Precision.HIGHEST)
    lane = lax.broadcasted_iota(jnp.int32, logits.shape, 1).astype(f32)
    neg = -jnp.inf
    lg = jnp.where(lane < N_EXPERTS, logits, neg)
    m1 = jnp.max(lg, axis=-1, keepdims=True)
    i1 = jnp.min(jnp.where(lg == m1, lane, float(LANES)), axis=-1, keepdims=True)
    lg2 = jnp.where(lane == i1, neg, lg)
    m2 = jnp.max(lg2, axis=-1, keepdims=True)
    i2 = jnp.min(jnp.where(lg2 == m2, lane, float(LANES)), axis=-1, keepdims=True)
    e2 = jnp.exp(m2 - m1)
    w1 = 1.0 / (1.0 + e2)
    w2 = e2 / (1.0 + e2)
    return jnp.where(lane == 0, i1, jnp.where(lane == 1, i2, jnp.where(lane == 2, w1, jnp.where(lane == 3, w2, 0.0))))


def _mix_out_kernel(*refs, routed):
    if routed:
        (ya_ref, yb_ref, ga_ref, gb_ref, x_ref, wa_ref, wb_ref, wo_ref, gpost_ref, gpre_ref, rw_ref,
         xo_ref, ho_ref, route_ref) = refs
    else:
        (ya_ref, yb_ref, ga_ref, gb_ref, x_ref, wa_ref, wb_ref, wo_ref, gpost_ref, gpre_ref,
         xo_ref, ho_ref) = refs
    a = jnp.dot(ya_ref[...], wa_ref[...], preferred_element_type=f32)
    b = jnp.dot(yb_ref[...], wb_ref[...], preferred_element_type=f32)
    merged = _sigmoid(ga_ref[...].astype(f32)) * a + _sigmoid(gb_ref[...].astype(f32)) * b
    y = jnp.dot(merged.astype(bf16), wo_ref[...], preferred_element_type=f32)
    x1 = x_ref[...] + _rms(y, gpost_ref[...])
    h = _rms(x1, gpre_ref[...])
    xo_ref[...] = x1
    ho_ref[...] = h.astype(ho_ref.dtype)
    if routed:
        route_ref[...] = _route_top2(h, rw_ref)


def _mix_out(ya, yb, proj, x, wa, wb, wo, g_post, g_pre, router_w=None, tm=512):
    t = x.shape[0]
    routed = router_w is not None
    row = lambda i: (i, 0)
    in_specs = [pl.BlockSpec((tm, A_WIDTH), row),
                pl.BlockSpec((tm, B_V_WIDTH), row),
                pl.BlockSpec((tm, D_MODEL), lambda i: (i, COL_GA // D_MODEL)),
                pl.BlockSpec((tm, D_MODEL), lambda i: (i, COL_GB // D_MODEL)),
                pl.BlockSpec((tm, D_MODEL), row),
                _const_spec((A_WIDTH, D_MODEL)),
                _const_spec((B_V_WIDTH, D_MODEL)),
                _const_spec((D_MODEL, D_MODEL)),
                _const_spec((1, D_MODEL)),
                _const_spec((1, D_MODEL))]
    args = [ya, yb, proj, proj, x, wa, wb, wo, g_post, g_pre]
    out_specs = [pl.BlockSpec((tm, D_MODEL), row), pl.BlockSpec((tm, D_MODEL), row)]
    out_shape = [jax.ShapeDtypeStruct((t, D_MODEL), f32),
                 jax.ShapeDtypeStruct((t, D_MODEL), f32 if routed else bf16)]
    if routed:
        in_specs.append(_const_spec((D_MODEL, LANES)))
        args.append(router_w)
        out_specs.append(pl.BlockSpec((tm, LANES), row))
        out_shape.append(jax.ShapeDtypeStruct((t, LANES), f32))
    return pl.pallas_call(
        functools.partial(_mix_out_kernel, routed=routed),
        grid=(t // tm,),
        in_specs=in_specs,
        out_specs=out_specs,
        out_shape=out_shape,
        compiler_params=_params("parallel"),
        name="mix_out",
    )(*args)


def _post_ffn(y, x1, p, gpost_ref, wpp_ref, wpg_ref, gple_ref):
    x2 = x1 + _rms(y, gpost_ref[...])
    e = jnp.dot(p.astype(bf16), wpp_ref[...], preferred_element_type=f32)
    e = e * _sigmoid(jnp.dot(x2.astype(bf16), wpg_ref[...], preferred_element_type=f32))
    return x2 + _rms(e, gple_ref[...])


def _swiglu_chunks(x, wg_ref, wu_ref, wd_ref, lead, width):
    acc = None
    for c0, c1 in _chunks(width, FF_CHUNK):
        g = jnp.dot(x, wg_ref[lead + (slice(None), slice(c0, c1))], preferred_element_type=f32)
        u = jnp.dot(x, wu_ref[lead + (slice(None), slice(c0, c1))], preferred_element_type=f32)
        act = (_silu(g) * u).astype(bf16)
        part = jnp.dot(act, wd_ref[lead + (slice(c0, c1), slice(None))], preferred_element_type=f32)
        acc = part if acc is None else acc + part
    return acc


def _dense_ffn_kernel(h_ref, x_ref, p_ref, wg_ref, wu_ref, wd_ref, gpost_ref, wpp_ref, wpg_ref, gple_ref, o_ref):
    y = _swiglu_chunks(h_ref[...], wg_ref, wu_ref, wd_ref, (), wg_ref.shape[1])
    o_ref[...] = _post_ffn(y, x_ref[...], p_ref[...], gpost_ref, wpp_ref, wpg_ref, gple_ref)


def _dense_ffn(h, x1, p, wg, wu, wd, g_post, wpp, wpg, g_ple, tm=512):
    t = x1.shape[0]
    ff = wg.shape[1]
    row = lambda i: (i, 0)
    return pl.pallas_call(
        _dense_ffn_kernel,
        grid=(t // tm,),
        in_specs=[pl.BlockSpec((tm, D_MODEL), row),
                  pl.BlockSpec((tm, D_MODEL), row),
                  pl.BlockSpec((tm, PLE_DIM), row),
                  _const_spec((D_MODEL, ff)),
                  _const_spec((D_MODEL, ff)),
                  _const_spec((ff, D_MODEL)),
                  _const_spec((1, D_MODEL)),
                  _const_spec((PLE_DIM, D_MODEL)),
                  _const_spec((D_MODEL, D_MODEL)),
                  _const_spec((1, D_MODEL))],
        out_specs=pl.BlockSpec((tm, D_MODEL), row),
        out_shape=jax.ShapeDtypeStruct((t, D_MODEL), f32),
        compiler_params=_params("parallel"),
        name="dense_ffn",
    )(h, x1, p, wg, wu, wd, g_post, wpp, wpg, g_ple)


def _row_copy(src, s, dst, d, sem):
    return pltpu.make_async_copy(src.at[pl.ds(s, 1)], dst.at[pl.ds(d, 1)], sem)


def _scatter_kernel(pos_ref, h_ref, xs_in, xs_out, sem):
    del xs_in
    rows = h_ref.shape[0]

    def issue(r, c):
        _row_copy(h_ref, r, xs_out, pos_ref[2 * r], sem).start()
        _row_copy(h_ref, r, xs_out, pos_ref[2 * r + 1], sem).start()
        return c

    lax.fori_loop(0, rows, issue, 0)

    def drain(r, c):
        _row_copy(h_ref, 0, xs_out, 0, sem).wait()
        _row_copy(h_ref, 0, xs_out, 0, sem).wait()
        return c

    lax.fori_loop(0, rows, drain, 0)


def _scatter_rows(pos, h, xs):
    t = h.shape[0]
    return pl.pallas_call(
        _scatter_kernel,
        grid=(t // ROW_BLK,),
        in_specs=[pl.BlockSpec((2 * ROW_BLK,), lambda i: (i,), memory_space=pltpu.SMEM),
                  pl.BlockSpec((ROW_BLK, D_MODEL), lambda i: (i, 0)),
                  pl.BlockSpec(memory_space=pl.ANY)],
        out_specs=pl.BlockSpec(memory_space=pl.ANY),
        out_shape=jax.ShapeDtypeStruct(xs.shape, xs.dtype),
        scratch_shapes=[pltpu.SemaphoreType.DMA(())],
        input_output_aliases={2: 0},
        compiler_params=_params("arbitrary"),
        name="moe_scatter",
    )(pos, h, xs)


def _expert_kernel(te_ref, nu_ref, x_ref, wg_ref, wu_ref, wd_ref, o_ref, acc_ref):
    del te_ref
    i = pl.program_id(0)
    f = pl.program_id(1)

    @pl.when(i < nu_ref[0])
    def _():
        y = _swiglu_chunks(x_ref[...].astype(bf16), wg_ref, wu_ref, wd_ref, (0,), wg_ref.shape[2])

        @pl.when(f == 0)
        def _():
            acc_ref[...] = y

        @pl.when(f > 0)
        def _():
            acc_ref[...] += y

    last = f == pl.num_programs(1) - 1

    @pl.when(jnp.logical_and(last, i < nu_ref[0]))
    def _():
        o_ref[...] = acc_ref[...]

    @pl.when(jnp.logical_and(last, i >= nu_ref[0]))
    def _():
        o_ref[...] = jnp.zeros_like(o_ref)


def _expert_ffn(tile_expert, n_used, xs, wg, wu, wd, tf=1792):
    n_rows = xs.shape[0]
    ff = wg.shape[2]
    nf = ff // tf

    def f_idx(i, f, nu):
        return jnp.where(i < nu[0], f, nf - 1)

    return pl.pallas_call(
        _expert_kernel,
        grid_spec=pltpu.PrefetchScalarGridSpec(
            num_scalar_prefetch=2,
            grid=(n_rows // MOE_TM, nf),
            in_specs=[pl.BlockSpec((MOE_TM, D_MODEL), lambda i, f, te, nu: (i, 0)),
                      pl.BlockSpec((1, D_MODEL, tf), lambda i, f, te, nu: (te[i], 0, f_idx(i, f, nu))),
                      pl.BlockSpec((1, D_MODEL, tf), lambda i, f, te, nu: (te[i], 0, f_idx(i, f, nu))),
                      pl.BlockSpec((1, tf, D_MODEL), lambda i, f, te, nu: (te[i], f_idx(i, f, nu), 0))],
            out_specs=pl.BlockSpec((MOE_TM, D_MODEL), lambda i, f, te, nu: (i, 0)),
            scratch_shapes=[pltpu.VMEM((MOE_TM, D_MODEL), f32)]),
        out_shape=jax.ShapeDtypeStruct((n_rows, D_MODEL), f32),
        compiler_params=_params("arbitrary", "arbitrary"),
        name="moe_experts",
    )(tile_expert, n_used, xs, wg, wu, wd)


def _combine_kernel(pos_ref, ys_hbm, route_ref, x_ref, p_ref, gpost_ref, wpp_ref, wpg_ref, gple_ref,
                    o_ref, buf, sem):
    rows = x_ref.shape[0]

    def issue(r, c):
        _row_copy(ys_hbm, pos_ref[2 * r], buf.at[0], r, sem).start()
        _row_copy(ys_hbm, pos_ref[2 * r + 1], buf.at[1], r, sem).start()
        return c

    lax.fori_loop(0, rows, issue, 0)

    def drain(r, c):
        _row_copy(ys_hbm, 0, buf.at[0], 0, sem).wait()
        _row_copy(ys_hbm, 0, buf.at[0], 0, sem).wait()
        return c

    lax.fori_loop(0, rows, drain, 0)
    route = route_ref[...]
    y = route[:, 2:3] * buf[0] + route[:, 3:4] * buf[1]
    o_ref[...] = _post_ffn(y, x_ref[...], p_ref[...], gpost_ref, wpp_ref, wpg_ref, gple_ref)


def _combine(pos, ys, route, x1, p, g_post, wpp, wpg, g_ple):
    t = x1.shape[0]
    row = lambda i: (i, 0)
    return pl.pallas_call(
        _combine_kernel,
        grid=(t // ROW_BLK,),
        in_specs=[pl.BlockSpec((2 * ROW_BLK,), lambda i: (i,), memory_space=pltpu.SMEM),
                  pl.BlockSpec(memory_space=pl.ANY),
                  pl.BlockSpec((ROW_BLK, LANES), row),
                  pl.BlockSpec((ROW_BLK, D_MODEL), row),
                  pl.BlockSpec((ROW_BLK, PLE_DIM), row),
                  _const_spec((1, D_MODEL)),
                  _const_spec((PLE_DIM, D_MODEL)),
                  _const_spec((D_MODEL, D_MODEL)),
                  _const_spec((1, D_MODEL))],
        out_specs=pl.BlockSpec((ROW_BLK, D_MODEL), row),
        out_shape=jax.ShapeDtypeStruct((t, D_MODEL), f32),
        scratch_shapes=[pltpu.VMEM((2, ROW_BLK, D_MODEL), f32), pltpu.SemaphoreType.DMA(())],
        compiler_params=_params("arbitrary"),
        name="moe_combine",
    )(pos, ys, route, x1, p, g_post, wpp, wpg, g_ple)


def _route_slots(route, n_tiles):
    ids = route[:, 0:2].astype(jnp.int32).reshape(-1)
    onehot = (ids[None, :] == jnp.arange(N_EXPERTS, dtype=jnp.int32)[:, None]).astype(jnp.int32)
    csum = jnp.cumsum(onehot, axis=1)
    rank = jnp.sum(onehot * csum, axis=0) - 1
    counts = csum[:, -1]
    padded = ((counts + MOE_TM - 1) // MOE_TM) * MOE_TM
    ends = jnp.cumsum(padded)
    pos = (ends - padded)[ids] + rank
    tile_start = jnp.arange(n_tiles, dtype=jnp.int32) * MOE_TM
    tile_expert = jnp.minimum(jnp.sum((tile_start[:, None] >= ends[None, :]).astype(jnp.int32), axis=1),
                              N_EXPERTS - 1)
    n_used = (ends[-1] // MOE_TM).reshape(1)
    return pos.astype(jnp.int32), tile_expert.astype(jnp.int32), n_used.astype(jnp.int32)


def _moe(h, route, x1, p, wg, wu, wd, g_post, wpp, wpg, g_ple):
    t = h.shape[0]
    n_tiles = 2 * t // MOE_TM + N_EXPERTS
    pos, tile_expert, n_used = _route_slots(route, n_tiles)
    xs = _scatter_rows(pos, h, jnp.zeros((n_tiles * MOE_TM, D_MODEL), f32))
    ys = _expert_ffn(tile_expert, n_used, xs, wg, wu, wd)
    return _combine(pos, ys, route, x1, p, g_post, wpp, wpg, g_ple)


def _prep_w_in(w):
    gd0 = COL_R
    gd1 = gd0 + GATE_RANK
    w = jnp.concatenate([w[:, :gd0], w[:, gd1:], w[:, gd0:gd1]], axis=1)
    return jnp.pad(w, ((0, 0), (0, IN_WIDTH_PAD - w.shape[1]))).astype(bf16)


def kernel(x, p, w_in, rel_bias, w_gla_gate_up, b_gla_gate, gla_norm, w_branch_a, w_branch_b, w_out, norm_mix_pre, norm_mix_post, norm_ffn_pre, norm_ffn_post, ffn_w_gate, ffn_w_up, ffn_w_down, router_w, moe_w_gate, moe_w_up, moe_w_down, ple_w_proj, ple_w_gate, ple_norm):
    b, s, d = x.shape
    t = b * s
    depth = w_in.shape[0]
    x = x.reshape(t, d)
    vec = lambda a: a.reshape(1, -1).astype(f32)
    for i in range(depth):
        proj = _in_proj(x, vec(norm_mix_pre[i]), _prep_w_in(w_in[i]))
        proj3 = proj.reshape(b, s, IN_WIDTH_PAD)
        ya = _attention(proj3, _attn_bias(rel_bias[i])).reshape(t, A_WIDTH)
        w_up = jnp.pad(w_gla_gate_up[i], ((0, LANES - GATE_RANK), (0, 0))).astype(bf16)
        yb = _gla(proj3, w_up, vec(b_gla_gate[i]), vec(gla_norm[i])).reshape(t, B_V_WIDTH)
        j = i // 2
        routed = i % 2 == 1
        rw = jnp.pad(router_w[j], ((0, 0), (0, LANES - N_EXPERTS))).astype(f32) if routed else None
        outs = _mix_out(ya, yb, proj, x, w_branch_a[i].astype(bf16), w_branch_b[i].astype(bf16),
                        w_out[i].astype(bf16), vec(norm_mix_post[i]), vec(norm_ffn_pre[i]), rw)
        p_i = p[i].reshape(t, PLE_DIM)
        tail = (vec(norm_ffn_post[i]), ple_w_proj[i].astype(bf16), ple_w_gate[i].astype(bf16), vec(ple_norm[i]))
        if routed:
            x1, h, route = outs
            x = _moe(h, route, x1, p_i, moe_w_gate[j].astype(bf16), moe_w_up[j].astype(bf16),
                     moe_w_down[j].astype(bf16), *tail)
        else:
            x1, h = outs
            x = _dense_ffn(h, x1, p_i, ffn_w_gate[j].astype(bf16), ffn_w_up[j].astype(bf16),
                           ffn_w_down[j].astype(bf16), *tail)
    return x.reshape(b, s, d)
```

```python
import functools

import numpy as np
import jax
import jax.numpy as jnp
from jax import lax
from jax.experimental import pallas as pl
from jax.experimental.pallas import tpu as pltpu

f32 = jnp.float32
bf16 = jnp.bfloat16

D_MODEL = 1024
CHUNK = 64
N_PREV_CHUNKS = 8
A_HEADS = 8
A_HEAD_DIM = 64
A_WIDTH = A_HEADS * A_HEAD_DIM
REL_CLIP = 128
B_HEADS = 4
B_KEY_DIM = 64
B_VAL_DIM = 128
B_K_WIDTH = B_HEADS * B_KEY_DIM
B_V_WIDTH = B_HEADS * B_VAL_DIM
GATE_RANK = 16
GATE_TEMP = 16.0
N_EXPERTS = 8
PLE_DIM = 256
NORM_EPS = 1e-6

LANES = 128
VMEM_LIMIT = 56 * 1024 * 1024

COL_QA, COL_KA, COL_VA = 0, 512, 1024
COL_QB, COL_KB, COL_VB = 1536, 1792, 2048
COL_R, COL_GA, COL_GB, COL_GD = 2560, 3072, 4096, 5120
IN_WIDTH_PAD = 5376
IN_TILE_N = 1792

QBLK = 2 * CHUNK
KBAND = (N_PREV_CHUNKS + 2) * CHUNK
KPAD = N_PREV_CHUNKS * CHUNK

MOE_TM = 512
ROW_BLK = 512
FF_CHUNK = 512
ISSUE_UNROLL = 8


def _params(*sem):
    return pltpu.CompilerParams(dimension_semantics=sem, vmem_limit_bytes=VMEM_LIMIT)


def _rms(x, g):
    return x * lax.rsqrt(jnp.mean(x * x, axis=-1, keepdims=True) + NORM_EPS) * g


def _sigmoid(x):
    return 1.0 / (1.0 + jnp.exp(-x))


def _silu(x):
    return x * _sigmoid(x)


def _chunks(n, c):
    return [(s, min(s + c, n)) for s in range(0, n, c)]


def _const_spec(shape):
    nd = len(shape)
    return pl.BlockSpec(shape, lambda *_: (0,) * nd, pipeline_mode=pl.Buffered(1))


def _in_proj_kernel(x_ref, g_ref, w_ref, o_ref, h_ref):
    @pl.when(pl.program_id(1) == 0)
    def _():
        h_ref[...] = _rms(x_ref[...], g_ref[...]).astype(bf16)

    o_ref[...] = jnp.dot(h_ref[...], w_ref[...], preferred_element_type=f32).astype(o_ref.dtype)


def _in_proj(x, g, w, tm=1024):
    t = x.shape[0]
    n = w.shape[1]
    return pl.pallas_call(
        _in_proj_kernel,
        grid=(t // tm, n // IN_TILE_N),
        in_specs=[pl.BlockSpec((tm, D_MODEL), lambda i, j: (i, 0)),
                  pl.BlockSpec((1, D_MODEL), lambda i, j: (0, 0)),
                  pl.BlockSpec((D_MODEL, IN_TILE_N), lambda i, j: (0, j))],
        out_specs=pl.BlockSpec((tm, IN_TILE_N), lambda i, j: (i, j)),
        out_shape=jax.ShapeDtypeStruct((t, n), bf16),
        scratch_shapes=[pltpu.VMEM((tm, D_MODEL), bf16)],
        compiler_params=_params("parallel", "arbitrary"),
        name="in_proj",
    )(x, g, w)


def _attn_kernel(q_ref, k_ref, v_ref, bias_ref, o_ref, kpad, vpad):
    i = pl.program_id(1)
    seq = k_ref.shape[1]

    @pl.when(i == 0)
    def _():
        zeros = jnp.zeros((KPAD, A_WIDTH), bf16)
        kpad[0:KPAD, :] = zeros
        vpad[0:KPAD, :] = zeros
        kpad[KPAD:KPAD + seq, :] = k_ref[0]
        vpad[KPAD:KPAD + seq, :] = v_ref[0]

    start = pl.multiple_of(i * QBLK, QBLK)
    lane = lax.broadcasted_iota(jnp.int32, (QBLK, LANES), 1)
    upper = lane >= A_HEAD_DIM
    kcol = lax.broadcasted_iota(jnp.int32, (QBLK, KBAND), 1)
    in_seq = kcol >= KPAD - i * QBLK
    for hp in range(A_WIDTH // LANES):
        cols = slice(hp * LANES, (hp + 1) * LANES)
        qp = q_ref[0, :, cols] * (A_HEAD_DIM ** -0.5)
        kp = kpad[pl.ds(start, KBAND), cols]
        vp = vpad[pl.ds(start, KBAND), cols]
        outs = []
        for s in range(2):
            qm = jnp.where(upper == bool(s), qp, jnp.zeros_like(qp))
            sc = lax.dot_general(qm, kp, (((1,), (1,)), ((), ())), preferred_element_type=f32)
            sc = sc + bias_ref[2 * hp + s]
            sc = jnp.where(in_seq, sc, -1e30)
            m = jnp.max(sc, axis=-1, keepdims=True)
            p = jnp.exp(sc - m)
            l = jnp.sum(p, axis=-1, keepdims=True)
            o = jnp.dot(p.astype(bf16), vp, preferred_element_type=f32)
            outs.append(o / l)
        o_ref[0, :, cols] = jnp.where(upper, outs[1], outs[0]).astype(o_ref.dtype)


def _attention(proj3, bias):
    b, s, _ = proj3.shape
    return pl.pallas_call(
        _attn_kernel,
        grid=(b, s // QBLK),
        in_specs=[pl.BlockSpec((1, QBLK, A_WIDTH), lambda bi, i: (bi, i, COL_QA // A_WIDTH)),
                  pl.BlockSpec((1, s, A_WIDTH), lambda bi, i: (bi, 0, COL_KA // A_WIDTH)),
                  pl.BlockSpec((1, s, A_WIDTH), lambda bi, i: (bi, 0, COL_VA // A_WIDTH)),
                  _const_spec((A_HEADS, QBLK, KBAND))],
        out_specs=pl.BlockSpec((1, QBLK, A_WIDTH), lambda bi, i: (bi, i, 0)),
        out_shape=jax.ShapeDtypeStruct((b, s, A_WIDTH), bf16),
        scratch_shapes=[pltpu.VMEM((KPAD + s, A_WIDTH), bf16),
                        pltpu.VMEM((KPAD + s, A_WIDTH), bf16)],
        compiler_params=_params("parallel", "arbitrary"),
        name="chunk_attention",
    )(proj3, proj3, proj3, bias)


def _attn_bias(rel_bias):
    h = rel_bias.shape[0]
    r = np.arange(QBLK)[:, None]
    k = np.arange(KBAND)[None, :]
    visible = np.where(r < CHUNK, k < KBAND - CHUNK, k >= CHUNK)
    span = QBLK + KBAND - 1
    n_far = span - 2 * REL_CLIP
    line = jnp.concatenate([jnp.broadcast_to(rel_bias[:, -1:], (h, n_far)), rel_bias[:, :0:-1]], axis=1)
    line = jnp.pad(line.astype(f32), ((0, 0), (0, 1)))
    skew = jnp.tile(line, (1, QBLK))[:, :QBLK * span].reshape(h, QBLK, span)
    bias = skew[:, :, QBLK - 1:]
    return jnp.where(visible[None], bias, -1e30)


def _gla_kernel(q_ref, k_ref, v_ref, gd_ref, r_ref, wup_ref, b_ref, gn_ref, o_ref, state):
    @pl.when(pl.program_id(1) == 0)
    def _():
        state[...] = jnp.zeros_like(state)

    tq = q_ref.shape[1]
    row = lax.broadcasted_iota(jnp.int32, (CHUNK, CHUNK), 0)
    col = lax.broadcasted_iota(jnp.int32, (CHUNK, CHUNK), 1)
    later = (col > row).astype(f32)
    ones = jnp.ones((CHUNK, LANES), f32)
    lane = lax.broadcasted_iota(jnp.int32, (CHUNK, LANES), 1)
    upper = lane >= B_KEY_DIM
    tn = (((0,), (0,)), ((), ()))
    hi = lax.Precision.HIGHEST
    for c in range(tq // CHUNK):
        rows = slice(c * CHUNK, (c + 1) * CHUNK)
        z = jnp.dot(gd_ref[0, rows, :], wup_ref[...], preferred_element_type=f32) + b_ref[...]
        log_a = (jnp.minimum(z, 0.0) - jnp.log(1.0 + jnp.exp(-jnp.abs(z)))) / GATE_TEMP
        rev = jnp.dot(later, log_a, preferred_element_type=f32, precision=hi)
        kdec = k_ref[0, rows, :].astype(f32) * jnp.exp(rev)
        for pr in range(B_HEADS // 2):
            cols = slice(pr * LANES, (pr + 1) * LANES)
            tot = lax.dot_general(log_a[:, cols], ones, tn, preferred_element_type=f32, precision=hi)
            kd = kdec[:, cols]
            d_state = None
            for s in range(2):
                h = 2 * pr + s
                kdm = jnp.where(upper == bool(s), kd, 0.0).astype(bf16)
                vh = v_ref[0, rows, h * B_VAL_DIM:(h + 1) * B_VAL_DIM]
                part = lax.dot_general(kdm, vh, tn, preferred_element_type=f32)
                d_state = part if d_state is None else d_state + part
            st = jnp.exp(tot) * state[pr] + d_state
            state[pr] = st
            stb = st.astype(bf16)
            qp = q_ref[0, rows, cols] * (B_KEY_DIM ** -0.5)
            for s in range(2):
                h = 2 * pr + s
                vcols = slice(h * B_VAL_DIM, (h + 1) * B_VAL_DIM)
                qm = jnp.where(upper == bool(s), qp, jnp.zeros_like(qp))
                o = jnp.dot(qm, stb, preferred_element_type=f32)
                o = _rms(o, gn_ref[:, vcols])
                r = r_ref[0, rows, vcols].astype(f32)
                o_ref[0, rows, vcols] = (o * _silu(r)).astype(o_ref.dtype)


def _gla(proj3, w_up, b_gate, gla_norm, tq=256):
    b, s, _ = proj3.shape
    return pl.pallas_call(
        _gla_kernel,
        grid=(b, s // tq),
        in_specs=[pl.BlockSpec((1, tq, B_K_WIDTH), lambda bi, i: (bi, i, COL_QB // B_K_WIDTH)),
                  pl.BlockSpec((1, tq, B_K_WIDTH), lambda bi, i: (bi, i, COL_KB // B_K_WIDTH)),
                  pl.BlockSpec((1, tq, B_V_WIDTH), lambda bi, i: (bi, i, COL_VB // B_V_WIDTH)),
                  pl.BlockSpec((1, tq, LANES), lambda bi, i: (bi, i, COL_GD // LANES)),
                  pl.BlockSpec((1, tq, B_V_WIDTH), lambda bi, i: (bi, i, COL_R // B_V_WIDTH)),
                  _const_spec((LANES, B_K_WIDTH)),
                  _const_spec((1, B_K_WIDTH)),
                  _const_spec((1, B_V_WIDTH))],
        out_specs=pl.BlockSpec((1, tq, B_V_WIDTH), lambda bi, i: (bi, i, 0)),
        out_shape=jax.ShapeDtypeStruct((b, s, B_V_WIDTH), bf16),
        scratch_shapes=[pltpu.VMEM((B_HEADS // 2, LANES, B_VAL_DIM), f32)],
        compiler_params=_params("parallel", "arbitrary"),
        name="gla",
    )(proj3, proj3, proj3, proj3, proj3, w_up, b_gate, gla_norm)


def _route_top2(h, rw_ref):
    logits = jnp.dot(h.astype(bf16), rw_ref[...], preferred_element_type=f32)
    lane = lax.broadcasted_iota(jnp.int32, logits.shape, 1).astype(f32)
    neg = -jnp.inf
    lg = jnp.where(lane < N_EXPERTS, logits, neg)
    m1 = jnp.max(lg, axis=-1, keepdims=True)
    i1 = jnp.min(jnp.where(lg == m1, lane, float(LANES)), axis=-1, keepdims=True)
    lg2 = jnp.where(lane == i1, neg, lg)
    m2 = jnp.max(lg2, axis=-1, keepdims=True)
    i2 = jnp.min(jnp.where(lg2 == m2, lane, float(LANES)), axis=-1, keepdims=True)
    e2 = jnp.exp(m2 - m1)
    w1 = 1.0 / (1.0 + e2)
    w2 = e2 / (1.0 + e2)
    return jnp.where(lane == 0, i1, jnp.where(lane == 1, i2, jnp.where(lane == 2, w1, jnp.where(lane == 3, w2, 0.0))))


def _mix_out_kernel(*refs, routed):
    if routed:
        (ya_ref, yb_ref, ga_ref, gb_ref, x_ref, wa_ref, wb_ref, wo_ref, gpost_ref, gpre_ref, rw_ref,
         xo_ref, ho_ref, route_ref) = refs
    else:
        (ya_ref, yb_ref, ga_ref, gb_ref, x_ref, wa_ref, wb_ref, wo_ref, gpost_ref, gpre_ref,
         xo_ref, ho_ref) = refs
    a = jnp.dot(ya_ref[...], wa_ref[...], preferred_element_type=f32)
    b = jnp.dot(yb_ref[...], wb_ref[...], preferred_element_type=f32)
    merged = _sigmoid(ga_ref[...].astype(f32)) * a + _sigmoid(gb_ref[...].astype(f32)) * b
    y = jnp.dot(merged.astype(bf16), wo_ref[...], preferred_element_type=f32)
    x1 = x_ref[...] + _rms(y, gpost_ref[...])
    h = _rms(x1, gpre_ref[...])
    xo_ref[...] = x1
    ho_ref[...] = h.astype(ho_ref.dtype)
    if routed:
        route_ref[...] = _route_top2(h, rw_ref)


def _mix_out(ya, yb, proj, x, wa, wb, wo, g_post, g_pre, router_w=None, tm=512):
    t = x.shape[0]
    routed = router_w is not None
    row = lambda i: (i, 0)
    in_specs = [pl.BlockSpec((tm, A_WIDTH), row),
                pl.BlockSpec((tm, B_V_WIDTH), row),
                pl.BlockSpec((tm, D_MODEL), lambda i: (i, COL_GA // D_MODEL)),
                pl.BlockSpec((tm, D_MODEL), lambda i: (i, COL_GB // D_MODEL)),
                pl.BlockSpec((tm, D_MODEL), row),
                _const_spec((A_WIDTH, D_MODEL)),
                _const_spec((B_V_WIDTH, D_MODEL)),
                _const_spec((D_MODEL, D_MODEL)),
                _const_spec((1, D_MODEL)),
                _const_spec((1, D_MODEL))]
    args = [ya, yb, proj, proj, x, wa, wb, wo, g_post, g_pre]
    out_specs = [pl.BlockSpec((tm, D_MODEL), row), pl.BlockSpec((tm, D_MODEL), row)]
    out_shape = [jax.ShapeDtypeStruct((t, D_MODEL), f32),
                 jax.ShapeDtypeStruct((t, D_MODEL), f32 if routed else bf16)]
    if routed:
        in_specs.append(_const_spec((D_MODEL, LANES)))
        args.append(router_w)
        out_specs.append(pl.BlockSpec((tm, LANES), row))
        out_shape.append(jax.ShapeDtypeStruct((t, LANES), f32))
    return pl.pallas_call(
        functools.partial(_mix_out_kernel, routed=routed),
        grid=(t // tm,),
        in_specs=in_specs,
        out_specs=out_specs,
        out_shape=out_shape,
        compiler_params=_params("parallel"),
        name="mix_out",
    )(*args)


def _post_ffn(y, x1, p, gpost_ref, wpp_ref, wpg_ref, gple_ref):
    x2 = x1 + _rms(y, gpost_ref[...])
    e = jnp.dot(p.astype(bf16), wpp_ref[...], preferred_element_type=f32)
    e = e * _sigmoid(jnp.dot(x2.astype(bf16), wpg_ref[...], preferred_element_type=f32))
    return x2 + _rms(e, gple_ref[...])


def _swiglu_chunks(x, wg_ref, wu_ref, wd_ref, lead, width):
    acc = None
    for c0, c1 in _chunks(width, FF_CHUNK):
        g = jnp.dot(x, wg_ref[lead + (slice(None), slice(c0, c1))], preferred_element_type=f32)
        u = jnp.dot(x, wu_ref[lead + (slice(None), slice(c0, c1))], preferred_element_type=f32)
        act = (_silu(g) * u).astype(bf16)
        part = jnp.dot(act, wd_ref[lead + (slice(c0, c1), slice(None))], preferred_element_type=f32)
        acc = part if acc is None else acc + part
    return acc


def _dense_ffn_kernel(h_ref, x_ref, p_ref, wg_ref, wu_ref, wd_ref, gpost_ref, wpp_ref, wpg_ref, gple_ref, o_ref):
    y = _swiglu_chunks(h_ref[...], wg_ref, wu_ref, wd_ref, (), wg_ref.shape[1])
    o_ref[...] = _post_ffn(y, x_ref[...], p_ref[...], gpost_ref, wpp_ref, wpg_ref, gple_ref)


def _dense_ffn(h, x1, p, wg, wu, wd, g_post, wpp, wpg, g_ple, tm=512):
    t = x1.shape[0]
    ff = wg.shape[1]
    row = lambda i: (i, 0)
    return pl.pallas_call(
        _dense_ffn_kernel,
        grid=(t // tm,),
        in_specs=[pl.BlockSpec((tm, D_MODEL), row),
                  pl.BlockSpec((tm, D_MODEL), row),
                  pl.BlockSpec((tm, PLE_DIM), row),
                  _const_spec((D_MODEL, ff)),
                  _const_spec((D_MODEL, ff)),
                  _const_spec((ff, D_MODEL)),
                  _const_spec((1, D_MODEL)),
                  _const_spec((PLE_DIM, D_MODEL)),
                  _const_spec((D_MODEL, D_MODEL)),
                  _const_spec((1, D_MODEL))],
        out_specs=pl.BlockSpec((tm, D_MODEL), row),
        out_shape=jax.ShapeDtypeStruct((t, D_MODEL), f32),
        compiler_params=_params("parallel"),
        name="dense_ffn",
    )(h, x1, p, wg, wu, wd, g_post, wpp, wpg, g_ple)


def _row_copy(src, s, dst, d, sem):
    return pltpu.make_async_copy(src.at[pl.ds(s, 1)], dst.at[pl.ds(d, 1)], sem)


def _scatter_kernel(pos_ref, h_ref, xs_in, xs_out, sem):
    del xs_in
    rows = h_ref.shape[0]

    def issue(r, c):
        for k in range(2):
            _row_copy(h_ref, r, xs_out, pos_ref[2 * r + k], sem.at[k]).start()
        return c

    lax.fori_loop(0, rows, issue, 0, unroll=ISSUE_UNROLL)
    for k in range(2):
        pltpu.make_async_copy(h_ref, xs_out.at[pl.ds(0, rows)], sem.at[k]).wait()


def _scatter_rows(pos, h, xs):
    t = h.shape[0]
    return pl.pallas_call(
        _scatter_kernel,
        grid=(t // ROW_BLK,),
        in_specs=[pl.BlockSpec((2 * ROW_BLK,), lambda i: (i,), memory_space=pltpu.SMEM),
                  pl.BlockSpec((ROW_BLK, D_MODEL), lambda i: (i, 0)),
                  pl.BlockSpec(memory_space=pl.ANY)],
        out_specs=pl.BlockSpec(memory_space=pl.ANY),
        out_shape=jax.ShapeDtypeStruct(xs.shape, xs.dtype),
        scratch_shapes=[pltpu.SemaphoreType.DMA((2,))],
        input_output_aliases={2: 0},
        compiler_params=_params("arbitrary"),
        name="moe_scatter",
    )(pos, h, xs)


def _expert_kernel(te_ref, nu_ref, x_ref, wg_ref, wu_ref, wd_ref, o_ref, acc_ref):
    del te_ref
    i = pl.program_id(0)
    f = pl.program_id(1)

    @pl.when(i < nu_ref[0])
    def _():
        y = _swiglu_chunks(x_ref[...].astype(bf16), wg_ref, wu_ref, wd_ref, (0,), wg_ref.shape[2])

        @pl.when(f == 0)
        def _():
            acc_ref[...] = y

        @pl.when(f > 0)
        def _():
            acc_ref[...] += y

    last = f == pl.num_programs(1) - 1

    @pl.when(jnp.logical_and(last, i < nu_ref[0]))
    def _():
        o_ref[...] = acc_ref[...]

    @pl.when(jnp.logical_and(last, i >= nu_ref[0]))
    def _():
        o_ref[...] = jnp.zeros_like(o_ref)


def _expert_ffn(tile_expert, n_used, xs, wg, wu, wd, tf=1792):
    n_rows = xs.shape[0]
    ff = wg.shape[2]
    nf = ff // tf

    def f_idx(i, f, nu):
        return jnp.where(i < nu[0], f, nf - 1)

    return pl.pallas_call(
        _expert_kernel,
        grid_spec=pltpu.PrefetchScalarGridSpec(
            num_scalar_prefetch=2,
            grid=(n_rows // MOE_TM, nf),
            in_specs=[pl.BlockSpec((MOE_TM, D_MODEL), lambda i, f, te, nu: (i, 0)),
                      pl.BlockSpec((1, D_MODEL, tf), lambda i, f, te, nu: (te[i], 0, f_idx(i, f, nu))),
                      pl.BlockSpec((1, D_MODEL, tf), lambda i, f, te, nu: (te[i], 0, f_idx(i, f, nu))),
                      pl.BlockSpec((1, tf, D_MODEL), lambda i, f, te, nu: (te[i], f_idx(i, f, nu), 0))],
            out_specs=pl.BlockSpec((MOE_TM, D_MODEL), lambda i, f, te, nu: (i, 0)),
            scratch_shapes=[pltpu.VMEM((MOE_TM, D_MODEL), f32)]),
        out_shape=jax.ShapeDtypeStruct((n_rows, D_MODEL), f32),
        compiler_params=_params("arbitrary", "arbitrary"),
        name="moe_experts",
    )(tile_expert, n_used, xs, wg, wu, wd)


def _combine_kernel(pos_ref, ys_hbm, route_ref, x_ref, p_ref, gpost_ref, wpp_ref, wpg_ref, gple_ref,
                    o_ref, buf, sem):
    rows = x_ref.shape[0]

    def issue(r, c):
        for k in range(2):
            _row_copy(ys_hbm, pos_ref[2 * r + k], buf.at[k], r, sem.at[k]).start()
        return c

    lax.fori_loop(0, rows, issue, 0, unroll=ISSUE_UNROLL)
    for k in range(2):
        pltpu.make_async_copy(ys_hbm.at[pl.ds(0, rows)], buf.at[k], sem.at[k]).wait()
    route = route_ref[...]
    y = route[:, 2:3] * buf[0] + route[:, 3:4] * buf[1]
    o_ref[...] = _post_ffn(y, x_ref[...], p_ref[...], gpost_ref, wpp_ref, wpg_ref, gple_ref)


def _combine(pos, ys, route, x1, p, g_post, wpp, wpg, g_ple):
    t = x1.shape[0]
    row = lambda i: (i, 0)
    return pl.pallas_call(
        _combine_kernel,
        grid=(t // ROW_BLK,),
        in_specs=[pl.BlockSpec((2 * ROW_BLK,), lambda i: (i,), memory_space=pltpu.SMEM),
                  pl.BlockSpec(memory_space=pl.ANY),
                  pl.BlockSpec((ROW_BLK, LANES), row),
                  pl.BlockSpec((ROW_BLK, D_MODEL), row),
                  pl.BlockSpec((ROW_BLK, PLE_DIM), row),
                  _const_spec((1, D_MODEL)),
                  _const_spec((PLE_DIM, D_MODEL)),
                  _const_spec((D_MODEL, D_MODEL)),
                  _const_spec((1, D_MODEL))],
        out_specs=pl.BlockSpec((ROW_BLK, D_MODEL), row),
        out_shape=jax.ShapeDtypeStruct((t, D_MODEL), f32),
        scratch_shapes=[pltpu.VMEM((2, ROW_BLK, D_MODEL), f32), pltpu.SemaphoreType.DMA((2,))],
        compiler_params=_params("arbitrary"),
        name="moe_combine",
    )(pos, ys, route, x1, p, g_post, wpp, wpg, g_ple)


def _route_slots(route, n_tiles):
    ids = route[:, 0:2].astype(jnp.int32).reshape(-1)
    onehot = (ids[None, :] == jnp.arange(N_EXPERTS, dtype=jnp.int32)[:, None]).astype(jnp.int32)
    csum = jnp.cumsum(onehot, axis=1)
    rank = jnp.sum(onehot * csum, axis=0) - 1
    counts = csum[:, -1]
    padded = ((counts + MOE_TM - 1) // MOE_TM) * MOE_TM
    ends = jnp.cumsum(padded)
    pos = (ends - padded)[ids] + rank
    tile_start = jnp.arange(n_tiles, dtype=jnp.int32) * MOE_TM
    tile_expert = jnp.minimum(jnp.sum((tile_start[:, None] >= ends[None, :]).astype(jnp.int32), axis=1),
                              N_EXPERTS - 1)
    n_used = (ends[-1] // MOE_TM).reshape(1)
    return pos.astype(jnp.int32), tile_expert.astype(jnp.int32), n_used.astype(jnp.int32)


def _moe(h, route, x1, p, wg, wu, wd, g_post, wpp, wpg, g_ple):
    t = h.shape[0]
    n_tiles = 2 * t // MOE_TM + N_EXPERTS
    pos, tile_expert, n_used = _route_slots(route, n_tiles)
    xs = _scatter_rows(pos, h, jnp.zeros((n_tiles * MOE_TM, D_MODEL), f32))
    ys = _expert_ffn(tile_expert, n_used, xs, wg, wu, wd)
    return _combine(pos, ys, route, x1, p, g_post, wpp, wpg, g_ple)


def _prep_w_in(w):
    gd0 = COL_R
    gd1 = gd0 + GATE_RANK
    w = jnp.concatenate([w[:, :gd0], w[:, gd1:], w[:, gd0:gd1]], axis=1)
    return jnp.pad(w, ((0, 0), (0, IN_WIDTH_PAD - w.shape[1]))).astype(bf16)


def kernel(x, p, w_in, rel_bias, w_gla_gate_up, b_gla_gate, gla_norm, w_branch_a, w_branch_b, w_out, norm_mix_pre, norm_mix_post, norm_ffn_pre, norm_ffn_post, ffn_w_gate, ffn_w_up, ffn_w_down, router_w, moe_w_gate, moe_w_up, moe_w_down, ple_w_proj, ple_w_gate, ple_norm):
    b, s, d = x.shape
    t = b * s
    depth = w_in.shape[0]
    x = x.reshape(t, d)
    vec = lambda a: a.reshape(1, -1).astype(f32)
    for i in range(depth):
        proj = _in_proj(x, vec(norm_mix_pre[i]), _prep_w_in(w_in[i]))
        proj3 = proj.reshape(b, s, IN_WIDTH_PAD)
        ya = _attention(proj3, _attn_bias(rel_bias[i])).reshape(t, A_WIDTH)
        w_up = jnp.pad(w_gla_gate_up[i], ((0, LANES - GATE_RANK), (0, 0))).astype(bf16)
        yb = _gla(proj3, w_up, vec(b_gla_gate[i]), vec(gla_norm[i])).reshape(t, B_V_WIDTH)
        j = i // 2
        routed = i % 2 == 1
        rw = jnp.pad(router_w[j], ((0, 0), (0, LANES - N_EXPERTS))).astype(bf16) if routed else None
        outs = _mix_out(ya, yb, proj, x, w_branch_a[i].astype(bf16), w_branch_b[i].astype(bf16),
                        w_out[i].astype(bf16), vec(norm_mix_post[i]), vec(norm_ffn_pre[i]), rw)
        p_i = p[i].reshape(t, PLE_DIM)
        tail = (vec(norm_ffn_post[i]), ple_w_proj[i].astype(bf16), ple_w_gate[i].astype(bf16), vec(ple_norm[i]))
        if routed:
            x1, h, route = outs
            x = _moe(h, route, x1, p_i, moe_w_gate[j].astype(bf16), moe_w_up[j].astype(bf16),
                     moe_w_down[j].astype(bf16), *tail)
        else:
            x1, h = outs
            x = _dense_ffn(h, x1, p_i, ffn_w_gate[j].astype(bf16), ffn_w_up[j].astype(bf16),
                           ffn_w_down[j].astype(bf16), *tail)
    return x.reshape(b, s, d)
```

```python
import functools

import numpy as np
import jax
import jax.numpy as jnp
from jax import lax
from jax.experimental import pallas as pl
from jax.experimental.pallas import tpu as pltpu

f32 = jnp.float32
bf16 = jnp.bfloat16

D_MODEL = 1024
CHUNK = 64
N_PREV_CHUNKS = 8
A_HEADS = 8
A_HEAD_DIM = 64
A_WIDTH = A_HEADS * A_HEAD_DIM
REL_CLIP = 128
B_HEADS = 4
B_KEY_DIM = 64
B_VAL_DIM = 128
B_K_WIDTH = B_HEADS * B_KEY_DIM
B_V_WIDTH = B_HEADS * B_VAL_DIM
GATE_RANK = 16
GATE_TEMP = 16.0
N_EXPERTS = 8
PLE_DIM = 256
NORM_EPS = 1e-6

LANES = 128
VMEM_LIMIT = 56 * 1024 * 1024

COL_QA, COL_KA, COL_VA = 0, 512, 1024
COL_QB, COL_KB, COL_VB = 1536, 1792, 2048
COL_R, COL_GA, COL_GB, COL_GD = 2560, 3072, 4096, 5120
IN_WIDTH_PAD = 5376
IN_TILE_N = 1792

QBLK = 2 * CHUNK
KBAND = (N_PREV_CHUNKS + 2) * CHUNK
KPAD = N_PREV_CHUNKS * CHUNK

MOE_TM = 512
ROW_BLK = 512
FF_CHUNK = 512
ISSUE_UNROLL = 8


def _params(*sem):
    return pltpu.CompilerParams(dimension_semantics=sem, vmem_limit_bytes=VMEM_LIMIT)


def _rms(x, g):
    return x * lax.rsqrt(jnp.mean(x * x, axis=-1, keepdims=True) + NORM_EPS) * g


def _sigmoid(x):
    return 1.0 / (1.0 + jnp.exp(-x))


def _silu(x):
    return x * _sigmoid(x)


def _chunks(n, c):
    return [(s, min(s + c, n)) for s in range(0, n, c)]


def _const_spec(shape):
    nd = len(shape)
    return pl.BlockSpec(shape, lambda *_: (0,) * nd, pipeline_mode=pl.Buffered(1))


def _in_proj_kernel(x_ref, g_ref, w_ref, o_ref, h_ref):
    @pl.when(pl.program_id(1) == 0)
    def _():
        h_ref[...] = _rms(x_ref[...], g_ref[...]).astype(bf16)

    o_ref[...] = jnp.dot(h_ref[...], w_ref[...], preferred_element_type=f32).astype(o_ref.dtype)


def _in_proj(x, g, w, tm=1024):
    t = x.shape[0]
    n = w.shape[1]
    return pl.pallas_call(
        _in_proj_kernel,
        grid=(t // tm, n // IN_TILE_N),
        in_specs=[pl.BlockSpec((tm, D_MODEL), lambda i, j: (i, 0)),
                  pl.BlockSpec((1, D_MODEL), lambda i, j: (0, 0)),
                  pl.BlockSpec((D_MODEL, IN_TILE_N), lambda i, j: (0, j))],
        out_specs=pl.BlockSpec((tm, IN_TILE_N), lambda i, j: (i, j)),
        out_shape=jax.ShapeDtypeStruct((t, n), bf16),
        scratch_shapes=[pltpu.VMEM((tm, D_MODEL), bf16)],
        compiler_params=_params("parallel", "arbitrary"),
        name="in_proj",
    )(x, g, w)


def _attn_kernel(qt_ref, k_ref, vt_ref, bias_ref, o_ref):
    i = pl.program_id(1)
    start = pl.multiple_of(i * QBLK, QBLK)
    row = lax.broadcasted_iota(jnp.int32, (LANES, QBLK), 0)
    upper = row >= A_HEAD_DIM
    n_kt = KBAND // LANES
    for hp in range(A_WIDTH // LANES):
        pair = slice(hp * LANES, (hp + 1) * LANES)
        qt = qt_ref[0, pair, :] * (A_HEAD_DIM ** -0.5)
        kp = k_ref[0, pl.ds(start, KBAND), pair]
        for s in range(2):
            h = 2 * hp + s
            qm = jnp.where(upper == bool(s), qt, jnp.zeros_like(qt))
            sc = jnp.dot(kp, qm, preferred_element_type=f32)
            tiles = []
            for kt in range(n_kt):
                t = sc[kt * LANES:(kt + 1) * LANES] + bias_ref[h, kt]
                tiles.append(jnp.where(i + kt >= KPAD // LANES, t, -1e30))
            m = tiles[0]
            for t in tiles[1:]:
                m = jnp.maximum(m, t)
            m = jnp.max(m, axis=0, keepdims=True)
            l = None
            o = None
            for kt in range(n_kt):
                p = jnp.exp(tiles[kt] - m)
                ls = jnp.sum(p, axis=0, keepdims=True)
                vt = vt_ref[0, i + kt, h * A_HEAD_DIM:(h + 1) * A_HEAD_DIM, :]
                part = jnp.dot(vt, p.astype(bf16), preferred_element_type=f32)
                l = ls if l is None else l + ls
                o = part if o is None else o + part
            o_ref[0, h * A_HEAD_DIM:(h + 1) * A_HEAD_DIM, :] = (o / l).astype(o_ref.dtype)


def _attention(proj3, bias_t):
    b, s, _ = proj3.shape
    n_kt_all = (KPAD + s) // LANES
    qt = jnp.swapaxes(proj3[:, :, COL_QA:COL_QA + A_WIDTH], 1, 2)
    kpad = jnp.pad(proj3[:, :, COL_KA:COL_KA + A_WIDTH], ((0, 0), (KPAD, 0), (0, 0)))
    vpad = jnp.pad(proj3[:, :, COL_VA:COL_VA + A_WIDTH], ((0, 0), (KPAD, 0), (0, 0)))
    vt = jnp.swapaxes(vpad.reshape(b, n_kt_all, LANES, A_WIDTH), 2, 3)
    return pl.pallas_call(
        _attn_kernel,
        grid=(b, s // QBLK),
        in_specs=[pl.BlockSpec((1, A_WIDTH, QBLK), lambda bi, i: (bi, 0, i)),
                  pl.BlockSpec((1, KPAD + s, A_WIDTH), lambda bi, i: (bi, 0, 0)),
                  pl.BlockSpec((1, n_kt_all, A_WIDTH, LANES), lambda bi, i: (bi, 0, 0, 0)),
                  _const_spec((A_HEADS, KBAND // LANES, LANES, QBLK))],
        out_specs=pl.BlockSpec((1, A_WIDTH, QBLK), lambda bi, i: (bi, 0, i)),
        out_shape=jax.ShapeDtypeStruct((b, A_WIDTH, s), bf16),
        compiler_params=_params("parallel", "arbitrary"),
        name="chunk_attention",
    )(qt, kpad, vt, bias_t)


def _attn_bias(rel_bias):
    h = rel_bias.shape[0]
    r = np.arange(QBLK)[:, None]
    k = np.arange(KBAND)[None, :]
    visible = np.where(r < CHUNK, k < KBAND - CHUNK, k >= CHUNK)
    span = QBLK + KBAND - 1
    n_far = span - 2 * REL_CLIP
    line = jnp.concatenate([jnp.broadcast_to(rel_bias[:, -1:], (h, n_far)), rel_bias[:, :0:-1]], axis=1)
    line = jnp.pad(line.astype(f32), ((0, 0), (0, 1)))
    skew = jnp.tile(line, (1, QBLK))[:, :QBLK * span].reshape(h, QBLK, span)
    bias = jnp.where(visible[None], skew[:, :, QBLK - 1:], -1e30)
    return jnp.swapaxes(bias, 1, 2).reshape(h, KBAND // LANES, LANES, QBLK)


def _gla_kernel(q_ref, k_ref, v_ref, gd_ref, r_ref, wup_ref, b_ref, gn_ref, o_ref, state):
    @pl.when(pl.program_id(1) == 0)
    def _():
        state[...] = jnp.zeros_like(state)

    tq = q_ref.shape[1]
    tn = (((0,), (0,)), ((), ()))
    nt = (((1,), (1,)), ((), ()))
    row = lax.broadcasted_iota(jnp.int32, (tq, tq), 0)
    col = lax.broadcasted_iota(jnp.int32, (tq, tq), 1)
    later = jnp.logical_and(col > row, col // CHUNK == row // CHUNK)
    later = jnp.where(later, 1.0, 0.0).astype(bf16)
    z = jnp.dot(gd_ref[0], wup_ref[...], preferred_element_type=f32) + b_ref[...]
    log_a = (jnp.minimum(z, 0.0) - jnp.log(1.0 + jnp.exp(-jnp.abs(z)))) / GATE_TEMP
    la_hi = log_a.astype(bf16)
    la_lo = (log_a - la_hi.astype(f32)).astype(bf16)
    rev = (jnp.dot(later, la_hi, preferred_element_type=f32)
           + jnp.dot(later, la_lo, preferred_element_type=f32))
    kdec = k_ref[0].astype(f32) * jnp.exp(rev)
    lane = lax.broadcasted_iota(jnp.int32, (CHUNK, LANES), 1)
    upper = lane >= B_KEY_DIM
    st = [state[pr] for pr in range(B_HEADS // 2)]
    for c in range(tq // CHUNK):
        rows = slice(c * CHUNK, (c + 1) * CHUNK)
        decay = jnp.exp(jnp.sum(log_a[rows], axis=0, keepdims=True))
        for pr in range(B_HEADS // 2):
            cols = slice(pr * LANES, (pr + 1) * LANES)
            kd = kdec[rows, cols]
            d_state = None
            for s in range(2):
                h = 2 * pr + s
                kdm = jnp.where(upper == bool(s), kd, 0.0).astype(bf16)
                vh = v_ref[0, rows, h * B_VAL_DIM:(h + 1) * B_VAL_DIM]
                part = lax.dot_general(vh, kdm, tn, preferred_element_type=f32)
                d_state = part if d_state is None else d_state + part
            st[pr] = decay[:, cols] * st[pr] + d_state
            stb = st[pr].astype(bf16)
            qp = q_ref[0, rows, cols] * (B_KEY_DIM ** -0.5)
            for s in range(2):
                h = 2 * pr + s
                vcols = slice(h * B_VAL_DIM, (h + 1) * B_VAL_DIM)
                qm = jnp.where(upper == bool(s), qp, jnp.zeros_like(qp))
                o = lax.dot_general(qm, stb, nt, preferred_element_type=f32)
                o = _rms(o, gn_ref[:, vcols])
                r = r_ref[0, rows, vcols].astype(f32)
                o_ref[0, rows, vcols] = (o * _silu(r)).astype(o_ref.dtype)
    for pr in range(B_HEADS // 2):
        state[pr] = st[pr]


def _gla(proj3, w_up, b_gate, gla_norm, tq=256):
    b, s, _ = proj3.shape
    return pl.pallas_call(
        _gla_kernel,
        grid=(b, s // tq),
        in_specs=[pl.BlockSpec((1, tq, B_K_WIDTH), lambda bi, i: (bi, i, COL_QB // B_K_WIDTH)),
                  pl.BlockSpec((1, tq, B_K_WIDTH), lambda bi, i: (bi, i, COL_KB // B_K_WIDTH)),
                  pl.BlockSpec((1, tq, B_V_WIDTH), lambda bi, i: (bi, i, COL_VB // B_V_WIDTH)),
                  pl.BlockSpec((1, tq, LANES), lambda bi, i: (bi, i, COL_GD // LANES)),
                  pl.BlockSpec((1, tq, B_V_WIDTH), lambda bi, i: (bi, i, COL_R // B_V_WIDTH)),
                  _const_spec((LANES, B_K_WIDTH)),
                  _const_spec((1, B_K_WIDTH)),
                  _const_spec((1, B_V_WIDTH))],
        out_specs=pl.BlockSpec((1, tq, B_V_WIDTH), lambda bi, i: (bi, i, 0)),
        out_shape=jax.ShapeDtypeStruct((b, s, B_V_WIDTH), bf16),
        scratch_shapes=[pltpu.VMEM((B_HEADS // 2, LANES, B_VAL_DIM), f32)],
        compiler_params=_params("parallel", "arbitrary"),
        name="gla",
    )(proj3, proj3, proj3, proj3, proj3, w_up, b_gate, gla_norm)


def _route_top2(h, rw_ref):
    logits = jnp.dot(h.astype(bf16), rw_ref[...], preferred_element_type=f32)
    lane = lax.broadcasted_iota(jnp.int32, logits.shape, 1).astype(f32)
    neg = -jnp.inf
    lg = jnp.where(lane < N_EXPERTS, logits, neg)
    m1 = jnp.max(lg, axis=-1, keepdims=True)
    i1 = jnp.min(jnp.where(lg == m1, lane, float(LANES)), axis=-1, keepdims=True)
    lg2 = jnp.where(lane == i1, neg, lg)
    m2 = jnp.max(lg2, axis=-1, keepdims=True)
    i2 = jnp.min(jnp.where(lg2 == m2, lane, float(LANES)), axis=-1, keepdims=True)
    e2 = jnp.exp(m2 - m1)
    w1 = 1.0 / (1.0 + e2)
    w2 = e2 / (1.0 + e2)
    return jnp.where(lane == 0, i1, jnp.where(lane == 1, i2, jnp.where(lane == 2, w1, jnp.where(lane == 3, w2, 0.0))))


def _mix_out_kernel(*refs, routed):
    if routed:
        (ya_ref, yb_ref, ga_ref, gb_ref, x_ref, wa_ref, wb_ref, wo_ref, gpost_ref, gpre_ref, rw_ref,
         xo_ref, ho_ref, route_ref) = refs
    else:
        (ya_ref, yb_ref, ga_ref, gb_ref, x_ref, wa_ref, wb_ref, wo_ref, gpost_ref, gpre_ref,
         xo_ref, ho_ref) = refs
    a = jnp.dot(ya_ref[...], wa_ref[...], preferred_element_type=f32)
    b = jnp.dot(yb_ref[...], wb_ref[...], preferred_element_type=f32)
    merged = _sigmoid(ga_ref[...].astype(f32)) * a + _sigmoid(gb_ref[...].astype(f32)) * b
    y = jnp.dot(merged.astype(bf16), wo_ref[...], preferred_element_type=f32)
    x1 = x_ref[...] + _rms(y, gpost_ref[...])
    h = _rms(x1, gpre_ref[...])
    xo_ref[...] = x1
    ho_ref[...] = h.astype(ho_ref.dtype)
    if routed:
        route_ref[...] = _route_top2(h, rw_ref)


def _mix_out(ya, yb, proj, x, wa, wb, wo, g_post, g_pre, router_w=None, tm=512):
    t = x.shape[0]
    routed = router_w is not None
    row = lambda i: (i, 0)
    in_specs = [pl.BlockSpec((tm, A_WIDTH), row),
                pl.BlockSpec((tm, B_V_WIDTH), row),
                pl.BlockSpec((tm, D_MODEL), lambda i: (i, COL_GA // D_MODEL)),
                pl.BlockSpec((tm, D_MODEL), lambda i: (i, COL_GB // D_MODEL)),
                pl.BlockSpec((tm, D_MODEL), row),
                _const_spec((A_WIDTH, D_MODEL)),
                _const_spec((B_V_WIDTH, D_MODEL)),
                _const_spec((D_MODEL, D_MODEL)),
                _const_spec((1, D_MODEL)),
                _const_spec((1, D_MODEL))]
    args = [ya, yb, proj, proj, x, wa, wb, wo, g_post, g_pre]
    out_specs = [pl.BlockSpec((tm, D_MODEL), row), pl.BlockSpec((tm, D_MODEL), row)]
    out_shape = [jax.ShapeDtypeStruct((t, D_MODEL), f32),
                 jax.ShapeDtypeStruct((t, D_MODEL), f32 if routed else bf16)]
    if routed:
        in_specs.append(_const_spec((D_MODEL, LANES)))
        args.append(router_w)
        out_specs.append(pl.BlockSpec((tm, LANES), row))
        out_shape.append(jax.ShapeDtypeStruct((t, LANES), f32))
    return pl.pallas_call(
        functools.partial(_mix_out_kernel, routed=routed),
        grid=(t // tm,),
        in_specs=in_specs,
        out_specs=out_specs,
        out_shape=out_shape,
        compiler_params=_params("parallel"),
        name="mix_out",
    )(*args)


def _post_ffn(y, x1, p, gpost_ref, wpp_ref, wpg_ref, gple_ref):
    x2 = x1 + _rms(y, gpost_ref[...])
    e = jnp.dot(p.astype(bf16), wpp_ref[...], preferred_element_type=f32)
    e = e * _sigmoid(jnp.dot(x2.astype(bf16), wpg_ref[...], preferred_element_type=f32))
    return x2 + _rms(e, gple_ref[...])


def _swiglu_chunks(x, wg_ref, wu_ref, wd_ref, lead, width):
    acc = None
    for c0, c1 in _chunks(width, FF_CHUNK):
        g = jnp.dot(x, wg_ref[lead + (slice(None), slice(c0, c1))], preferred_element_type=f32)
        u = jnp.dot(x, wu_ref[lead + (slice(None), slice(c0, c1))], preferred_element_type=f32)
        act = (_silu(g) * u).astype(bf16)
        part = jnp.dot(act, wd_ref[lead + (slice(c0, c1), slice(None))], preferred_element_type=f32)
        acc = part if acc is None else acc + part
    return acc


def _dense_ffn_kernel(h_ref, x_ref, p_ref, wg_ref, wu_ref, wd_ref, gpost_ref, wpp_ref, wpg_ref, gple_ref, o_ref):
    y = _swiglu_chunks(h_ref[...], wg_ref, wu_ref, wd_ref, (), wg_ref.shape[1])
    o_ref[...] = _post_ffn(y, x_ref[...], p_ref[...], gpost_ref, wpp_ref, wpg_ref, gple_ref)


def _dense_ffn(h, x1, p, wg, wu, wd, g_post, wpp, wpg, g_ple, tm=512):
    t = x1.shape[0]
    ff = wg.shape[1]
    row = lambda i: (i, 0)
    return pl.pallas_call(
        _dense_ffn_kernel,
        grid=(t // tm,),
        in_specs=[pl.BlockSpec((tm, D_MODEL), row),
                  pl.BlockSpec((tm, D_MODEL), row),
                  pl.BlockSpec((tm, PLE_DIM), row),
                  _const_spec((D_MODEL, ff)),
                  _const_spec((D_MODEL, ff)),
                  _const_spec((ff, D_MODEL)),
                  _const_spec((1, D_MODEL)),
                  _const_spec((PLE_DIM, D_MODEL)),
                  _const_spec((D_MODEL, D_MODEL)),
                  _const_spec((1, D_MODEL))],
        out_specs=pl.BlockSpec((tm, D_MODEL), row),
        out_shape=jax.ShapeDtypeStruct((t, D_MODEL), f32),
        compiler_params=_params("parallel"),
        name="dense_ffn",
    )(h, x1, p, wg, wu, wd, g_post, wpp, wpg, g_ple)


def _row_copy(src, s, dst, d, sem):
    return pltpu.make_async_copy(src.at[pl.ds(s, 1)], dst.at[pl.ds(d, 1)], sem)


def _scatter_kernel(pos_ref, h_ref, xs_in, xs_out, sem):
    del xs_in
    rows = h_ref.shape[0]

    def issue(r, c):
        for k in range(2):
            _row_copy(h_ref, r, xs_out, pos_ref[2 * r + k], sem.at[k]).start()
        return c

    lax.fori_loop(0, rows, issue, 0, unroll=ISSUE_UNROLL)
    for k in range(2):
        pltpu.make_async_copy(h_ref, xs_out.at[pl.ds(0, rows)], sem.at[k]).wait()


def _scatter_rows(pos, h, xs):
    t = h.shape[0]
    return pl.pallas_call(
        _scatter_kernel,
        grid=(t // ROW_BLK,),
        in_specs=[pl.BlockSpec((2 * ROW_BLK,), lambda i: (i,), memory_space=pltpu.SMEM),
                  pl.BlockSpec((ROW_BLK, D_MODEL), lambda i: (i, 0)),
                  pl.BlockSpec(memory_space=pl.ANY)],
        out_specs=pl.BlockSpec(memory_space=pl.ANY),
        out_shape=jax.ShapeDtypeStruct(xs.shape, xs.dtype),
        scratch_shapes=[pltpu.SemaphoreType.DMA((2,))],
        input_output_aliases={2: 0},
        compiler_params=_params("arbitrary"),
        name="moe_scatter",
    )(pos, h, xs)


def _expert_kernel(te_ref, nu_ref, x_ref, wg_ref, wu_ref, wd_ref, o_ref, acc_ref):
    del te_ref
    i = pl.program_id(0)
    f = pl.program_id(1)

    @pl.when(i < nu_ref[0])
    def _():
        y = _swiglu_chunks(x_ref[...].astype(bf16), wg_ref, wu_ref, wd_ref, (0,), wg_ref.shape[2])

        @pl.when(f == 0)
        def _():
            acc_ref[...] = y

        @pl.when(f > 0)
        def _():
            acc_ref[...] += y

    last = f == pl.num_programs(1) - 1

    @pl.when(jnp.logical_and(last, i < nu_ref[0]))
    def _():
        o_ref[...] = acc_ref[...]

    @pl.when(jnp.logical_and(last, i >= nu_ref[0]))
    def _():
        o_ref[...] = jnp.zeros_like(o_ref)


def _expert_ffn(tile_expert, n_used, xs, wg, wu, wd, tf=1792):
    n_rows = xs.shape[0]
    ff = wg.shape[2]
    nf = ff // tf

    def f_idx(i, f, nu):
        return jnp.where(i < nu[0], f, nf - 1)

    return pl.pallas_call(
        _expert_kernel,
        grid_spec=pltpu.PrefetchScalarGridSpec(
            num_scalar_prefetch=2,
            grid=(n_rows // MOE_TM, nf),
            in_specs=[pl.BlockSpec((MOE_TM, D_MODEL), lambda i, f, te, nu: (i, 0)),
                      pl.BlockSpec((1, D_MODEL, tf), lambda i, f, te, nu: (te[i], 0, f_idx(i, f, nu))),
                      pl.BlockSpec((1, D_MODEL, tf), lambda i, f, te, nu: (te[i], 0, f_idx(i, f, nu))),
                      pl.BlockSpec((1, tf, D_MODEL), lambda i, f, te, nu: (te[i], f_idx(i, f, nu), 0))],
            out_specs=pl.BlockSpec((MOE_TM, D_MODEL), lambda i, f, te, nu: (i, 0)),
            scratch_shapes=[pltpu.VMEM((MOE_TM, D_MODEL), f32)]),
        out_shape=jax.ShapeDtypeStruct((n_rows, D_MODEL), f32),
        compiler_params=_params("arbitrary", "arbitrary"),
        name="moe_experts",
    )(tile_expert, n_used, xs, wg, wu, wd)


def _combine_kernel(pos_ref, ys_hbm, route_ref, x_ref, p_ref, gpost_ref, wpp_ref, wpg_ref, gple_ref,
                    o_ref, buf, sem):
    rows = x_ref.shape[0]

    def issue(r, c):
        for k in range(2):
            _row_copy(ys_hbm, pos_ref[2 * r + k], buf.at[k], r, sem.at[k]).start()
        return c

    lax.fori_loop(0, rows, issue, 0, unroll=ISSUE_UNROLL)
    for k in range(2):
        pltpu.make_async_copy(ys_hbm.at[pl.ds(0, rows)], buf.at[k], sem.at[k]).wait()
    route = route_ref[...]
    y = route[:, 2:3] * buf[0] + route[:, 3:4] * buf[1]
    o_ref[...] = _post_ffn(y, x_ref[...], p_ref[...], gpost_ref, wpp_ref, wpg_ref, gple_ref)


def _combine(pos, ys, route, x1, p, g_post, wpp, wpg, g_ple):
    t = x1.shape[0]
    row = lambda i: (i, 0)
    return pl.pallas_call(
        _combine_kernel,
        grid=(t // ROW_BLK,),
        in_specs=[pl.BlockSpec((2 * ROW_BLK,), lambda i: (i,), memory_space=pltpu.SMEM),
                  pl.BlockSpec(memory_space=pl.ANY),
                  pl.BlockSpec((ROW_BLK, LANES), row),
                  pl.BlockSpec((ROW_BLK, D_MODEL), row),
                  pl.BlockSpec((ROW_BLK, PLE_DIM), row),
                  _const_spec((1, D_MODEL)),
                  _const_spec((PLE_DIM, D_MODEL)),
                  _const_spec((D_MODEL, D_MODEL)),
                  _const_spec((1, D_MODEL))],
        out_specs=pl.BlockSpec((ROW_BLK, D_MODEL), row),
        out_shape=jax.ShapeDtypeStruct((t, D_MODEL), f32),
        scratch_shapes=[pltpu.VMEM((2, ROW_BLK, D_MODEL), f32), pltpu.SemaphoreType.DMA((2,))],
        compiler_params=_params("arbitrary"),
        name="moe_combine",
    )(pos, ys, route, x1, p, g_post, wpp, wpg, g_ple)


def _route_slots(route, n_tiles):
    ids = route[:, 0:2].astype(jnp.int32).reshape(-1)
    onehot = (ids[None, :] == jnp.arange(N_EXPERTS, dtype=jnp.int32)[:, None]).astype(jnp.int32)
    csum = jnp.cumsum(onehot, axis=1)
    rank = jnp.sum(onehot * csum, axis=0) - 1
    counts = csum[:, -1]
    padded = ((counts + MOE_TM - 1) // MOE_TM) * MOE_TM
    ends = jnp.cumsum(padded)
    pos = (ends - padded)[ids] + rank
    tile_start = jnp.arange(n_tiles, dtype=jnp.int32) * MOE_TM
    tile_expert = jnp.minimum(jnp.sum((tile_start[:, None] >= ends[None, :]).astype(jnp.int32), axis=1),
                              N_EXPERTS - 1)
    n_used = (ends[-1] // MOE_TM).reshape(1)
    return pos.astype(jnp.int32), tile_expert.astype(jnp.int32), n_used.astype(jnp.int32)


def _moe(h, route, x1, p, wg, wu, wd, g_post, wpp, wpg, g_ple):
    t = h.shape[0]
    n_tiles = 2 * t // MOE_TM + N_EXPERTS
    pos, tile_expert, n_used = _route_slots(route, n_tiles)
    xs = _scatter_rows(pos, h, jnp.zeros((n_tiles * MOE_TM, D_MODEL), f32))
    ys = _expert_ffn(tile_expert, n_used, xs, wg, wu, wd)
    return _combine(pos, ys, route, x1, p, g_post, wpp, wpg, g_ple)


def _prep_w_in(w):
    gd0 = COL_R
    gd1 = gd0 + GATE_RANK
    w = jnp.concatenate([w[:, :gd0], w[:, gd1:], w[:, gd0:gd1]], axis=1)
    return jnp.pad(w, ((0, 0), (0, IN_WIDTH_PAD - w.shape[1]))).astype(bf16)


def kernel(x, p, w_in, rel_bias, w_gla_gate_up, b_gla_gate, gla_norm, w_branch_a, w_branch_b, w_out, norm_mix_pre, norm_mix_post, norm_ffn_pre, norm_ffn_post, ffn_w_gate, ffn_w_up, ffn_w_down, router_w, moe_w_gate, moe_w_up, moe_w_down, ple_w_proj, ple_w_gate, ple_norm):
    b, s, d = x.shape
    t = b * s
    depth = w_in.shape[0]
    x = x.reshape(t, d)
    vec = lambda a: a.reshape(1, -1).astype(f32)
    for i in range(depth):
        proj = _in_proj(x, vec(norm_mix_pre[i]), _prep_w_in(w_in[i]))
        proj3 = proj.reshape(b, s, IN_WIDTH_PAD)
        ya = jnp.swapaxes(_attention(proj3, _attn_bias(rel_bias[i])), 1, 2).reshape(t, A_WIDTH)
        w_up = jnp.pad(w_gla_gate_up[i], ((0, LANES - GATE_RANK), (0, 0))).astype(bf16)
        yb = _gla(proj3, w_up, vec(b_gla_gate[i]), vec(gla_norm[i])).reshape(t, B_V_WIDTH)
        j = i // 2
        routed = i % 2 == 1
        rw = jnp.pad(router_w[j], ((0, 0), (0, LANES - N_EXPERTS))).astype(bf16) if routed else None
        outs = _mix_out(ya, yb, proj, x, w_branch_a[i].astype(bf16), w_branch_b[i].astype(bf16),
                        w_out[i].astype(bf16), vec(norm_mix_post[i]), vec(norm_ffn_pre[i]), rw)
        p_i = p[i].reshape(t, PLE_DIM)
        tail = (vec(norm_ffn_post[i]), ple_w_proj[i].astype(bf16), ple_w_gate[i].astype(bf16), vec(ple_norm[i]))
        if routed:
            x1, h, route = outs
            x = _moe(h, route, x1, p_i, moe_w_gate[j].astype(bf16), moe_w_up[j].astype(bf16),
                     moe_w_down[j].astype(bf16), *tail)
        else:
            x1, h = outs
            x = _dense_ffn(h, x1, p_i, ffn_w_gate[j].astype(bf16), ffn_w_up[j].astype(bf16),
                           ffn_w_down[j].astype(bf16), *tail)
    return x.reshape(b, s, d)
```

```python
import functools

import numpy as np
import jax
import jax.numpy as jnp
from jax import lax
from jax.experimental import pallas as pl
from jax.experimental.pallas import tpu as pltpu

f32 = jnp.float32
bf16 = jnp.bfloat16

D_MODEL = 1024
CHUNK = 64
N_PREV_CHUNKS = 8
A_HEADS = 8
A_HEAD_DIM = 64
A_WIDTH = A_HEADS * A_HEAD_DIM
REL_CLIP = 128
B_HEADS = 4
B_KEY_DIM = 64
B_VAL_DIM = 128
B_K_WIDTH = B_HEADS * B_KEY_DIM
B_V_WIDTH = B_HEADS * B_VAL_DIM
GATE_RANK = 16
GATE_TEMP = 16.0
N_EXPERTS = 8
PLE_DIM = 256
NORM_EPS = 1e-6

LANES = 128
VMEM_LIMIT = 56 * 1024 * 1024

SRC_QA, SRC_KA, SRC_VA, SRC_QB, SRC_GD, SRC_R = 0, 512, 1024, 1536, 2560, 2576
SRC_END = 5136
COL_KA, COL_QB, COL_KB, COL_VB, COL_R, COL_GA, COL_GB = 0, 512, 768, 1024, 1536, 2048, 3072
MAIN_WIDTH = 4096
IN_TILE_N = 2048

QBLK = 2 * CHUNK
KBAND = (N_PREV_CHUNKS + 2) * CHUNK
KPAD = N_PREV_CHUNKS * CHUNK

MOE_TM = 512
ROW_BLK = 512
FF_CHUNK = 512
ISSUE_UNROLL = 8


def _params(*sem):
    return pltpu.CompilerParams(dimension_semantics=sem, vmem_limit_bytes=VMEM_LIMIT)


def _rms(x, g):
    return x * lax.rsqrt(jnp.mean(x * x, axis=-1, keepdims=True) + NORM_EPS) * g


def _sigmoid(x):
    return 1.0 / (1.0 + jnp.exp(-x))


def _silu(x):
    return x * _sigmoid(x)


def _chunks(n, c):
    return [(s, min(s + c, n)) for s in range(0, n, c)]


def _const_spec(shape):
    nd = len(shape)
    return pl.BlockSpec(shape, lambda *_: (0,) * nd, pipeline_mode=pl.Buffered(1))


def _in_proj_kernel(x_ref, g_ref, w_ref, o_ref, h_ref):
    @pl.when(pl.program_id(1) == 0)
    def _():
        h_ref[...] = _rms(x_ref[...], g_ref[...]).astype(bf16)

    o_ref[...] = jnp.dot(h_ref[...], w_ref[...], preferred_element_type=f32).astype(o_ref.dtype)


def _in_proj(x, g, w, tm=1024):
    t = x.shape[0]
    n = w.shape[1]
    return pl.pallas_call(
        _in_proj_kernel,
        grid=(t // tm, n // IN_TILE_N),
        in_specs=[pl.BlockSpec((tm, D_MODEL), lambda i, j: (i, 0)),
                  pl.BlockSpec((1, D_MODEL), lambda i, j: (0, 0)),
                  pl.BlockSpec((D_MODEL, IN_TILE_N), lambda i, j: (0, j))],
        out_specs=pl.BlockSpec((tm, IN_TILE_N), lambda i, j: (i, j)),
        out_shape=jax.ShapeDtypeStruct((t, n), bf16),
        scratch_shapes=[pltpu.VMEM((tm, D_MODEL), bf16)],
        compiler_params=_params("parallel", "arbitrary"),
        name="in_proj",
    )(x, g, w)


def _qv_proj_kernel(x_ref, g_ref, wq_ref, wv_ref, wgd_ref, qt_ref, vt_ref, gd_ref):
    nt = (((1,), (1,)), ((), ()))
    h = _rms(x_ref[...], g_ref[...]).astype(bf16)
    qt_ref[0] = lax.dot_general(wq_ref[...], h, nt, preferred_element_type=f32).astype(qt_ref.dtype)
    vt = lax.dot_general(wv_ref[...], h, nt, preferred_element_type=f32).astype(vt_ref.dtype)
    for kt in range(vt_ref.shape[1]):
        vt_ref[0, kt] = vt[:, kt * LANES:(kt + 1) * LANES]
    gd_ref[...] = jnp.dot(h, wgd_ref[...], preferred_element_type=f32).astype(gd_ref.dtype)


def _qv_proj(x, g, wq_t, wv_t, w_gd, b, s, tm=1024):
    t = x.shape[0]
    per_b = s // tm
    return pl.pallas_call(
        _qv_proj_kernel,
        grid=(t // tm,),
        in_specs=[pl.BlockSpec((tm, D_MODEL), lambda i: (i, 0)),
                  _const_spec((1, D_MODEL)),
                  _const_spec((A_WIDTH, D_MODEL)),
                  _const_spec((A_WIDTH, D_MODEL)),
                  _const_spec((D_MODEL, LANES))],
        out_specs=[pl.BlockSpec((1, A_WIDTH, tm), lambda i: (i // per_b, 0, i % per_b)),
                   pl.BlockSpec((1, tm // LANES, A_WIDTH, LANES), lambda i: (i // per_b, i % per_b, 0, 0)),
                   pl.BlockSpec((tm, LANES), lambda i: (i, 0))],
        out_shape=[jax.ShapeDtypeStruct((b, A_WIDTH, s), bf16),
                   jax.ShapeDtypeStruct((b, s // LANES, A_WIDTH, LANES), bf16),
                   jax.ShapeDtypeStruct((t, LANES), bf16)],
        compiler_params=_params("parallel"),
        name="qv_proj",
    )(x, g, wq_t, wv_t, w_gd)


def _attn_kernel(qt_ref, k_ref, vt_ref, bias_ref, o_ref):
    i = pl.program_id(1)
    row = lax.broadcasted_iota(jnp.int32, (LANES, QBLK), 0)
    upper = row >= A_HEAD_DIM
    n_kt = KBAND // LANES
    kt_seq = [i + kt - KPAD // LANES for kt in range(n_kt)]
    kt_clamped = [jnp.maximum(j, 0) for j in kt_seq]
    for hp in range(A_WIDTH // LANES):
        pair = slice(hp * LANES, (hp + 1) * LANES)
        qt = qt_ref[0, pair, :] * (A_HEAD_DIM ** -0.5)
        kps = [k_ref[0, pl.ds(pl.multiple_of(j * LANES, LANES), LANES), pair] for j in kt_clamped]
        outs = []
        for s in range(2):
            h = 2 * hp + s
            qm = jnp.where(upper == bool(s), qt, jnp.zeros_like(qt))
            tiles = []
            for kt in range(n_kt):
                t = jnp.dot(kps[kt], qm, preferred_element_type=f32) + bias_ref[h, kt]
                tiles.append(jnp.where(kt_seq[kt] >= 0, t, -1e30))
            m = tiles[0]
            for t in tiles[1:]:
                m = jnp.maximum(m, t)
            m = jnp.max(m, axis=0, keepdims=True)
            l = None
            o = None
            for kt in range(n_kt):
                p = jnp.exp(tiles[kt] - m)
                ls = jnp.sum(p, axis=0, keepdims=True)
                vt = vt_ref[0, kt_clamped[kt], h * A_HEAD_DIM:(h + 1) * A_HEAD_DIM, :]
                part = jnp.dot(vt, p.astype(bf16), preferred_element_type=f32)
                l = ls if l is None else l + ls
                o = part if o is None else o + part
            outs.append(o / l)
        o_ref[0, :, pair] = jnp.concatenate(outs, axis=0).T.astype(o_ref.dtype)


def _attention(qt, proj3, vt, bias_t):
    b, s, _ = proj3.shape
    return pl.pallas_call(
        _attn_kernel,
        grid=(b, s // QBLK),
        in_specs=[pl.BlockSpec((1, A_WIDTH, QBLK), lambda bi, i: (bi, 0, i)),
                  pl.BlockSpec((1, s, A_WIDTH), lambda bi, i: (bi, 0, COL_KA // A_WIDTH)),
                  pl.BlockSpec((1, s // LANES, A_WIDTH, LANES), lambda bi, i: (bi, 0, 0, 0)),
                  _const_spec((A_HEADS, KBAND // LANES, LANES, QBLK))],
        out_specs=pl.BlockSpec((1, QBLK, A_WIDTH), lambda bi, i: (bi, i, 0)),
        out_shape=jax.ShapeDtypeStruct((b, s, A_WIDTH), bf16),
        compiler_params=_params("parallel", "arbitrary"),
        name="chunk_attention",
    )(qt, proj3, vt, bias_t)


def _attn_bias(rel_bias):
    h = rel_bias.shape[0]
    r = np.arange(QBLK)[:, None]
    k = np.arange(KBAND)[None, :]
    visible = np.where(r < CHUNK, k < KBAND - CHUNK, k >= CHUNK)
    span = QBLK + KBAND - 1
    n_far = span - 2 * REL_CLIP
    line = jnp.concatenate([jnp.broadcast_to(rel_bias[:, -1:], (h, n_far)), rel_bias[:, :0:-1]], axis=1)
    line = jnp.pad(line.astype(f32), ((0, 0), (0, 1)))
    skew = jnp.tile(line, (1, QBLK))[:, :QBLK * span].reshape(h, QBLK, span)
    bias = jnp.where(visible[None], skew[:, :, QBLK - 1:], -1e30)
    return jnp.swapaxes(bias, 1, 2).reshape(h, KBAND // LANES, LANES, QBLK)


def _gla_kernel(q_ref, k_ref, v_ref, gd_ref, r_ref, wup_ref, b_ref, gn_ref, o_ref, state):
    @pl.when(pl.program_id(1) == 0)
    def _():
        state[...] = jnp.zeros_like(state)

    tq = q_ref.shape[1]
    tn = (((0,), (0,)), ((), ()))
    nt = (((1,), (1,)), ((), ()))
    row = lax.broadcasted_iota(jnp.int32, (tq, tq), 0)
    col = lax.broadcasted_iota(jnp.int32, (tq, tq), 1)
    later = jnp.logical_and(col > row, col // CHUNK == row // CHUNK)
    later = jnp.where(later, 1.0, 0.0).astype(bf16)
    z = jnp.dot(gd_ref[0], wup_ref[...], preferred_element_type=f32) + b_ref[...]
    log_a = (jnp.minimum(z, 0.0) - jnp.log(1.0 + jnp.exp(-jnp.abs(z)))) / GATE_TEMP
    la_hi = log_a.astype(bf16)
    la_lo = (log_a - la_hi.astype(f32)).astype(bf16)
    rev = (jnp.dot(later, la_hi, preferred_element_type=f32)
           + jnp.dot(later, la_lo, preferred_element_type=f32))
    kdec = k_ref[0].astype(f32) * jnp.exp(rev)
    lane = lax.broadcasted_iota(jnp.int32, (CHUNK, LANES), 1)
    upper = lane >= B_KEY_DIM
    st = [state[pr] for pr in range(B_HEADS // 2)]
    for c in range(tq // CHUNK):
        rows = slice(c * CHUNK, (c + 1) * CHUNK)
        decay = jnp.exp(jnp.sum(log_a[rows], axis=0, keepdims=True))
        for pr in range(B_HEADS // 2):
            cols = slice(pr * LANES, (pr + 1) * LANES)
            kd = kdec[rows, cols]
            d_state = None
            for s in range(2):
                h = 2 * pr + s
                kdm = jnp.where(upper == bool(s), kd, 0.0).astype(bf16)
                vh = v_ref[0, rows, h * B_VAL_DIM:(h + 1) * B_VAL_DIM]
                part = lax.dot_general(vh, kdm, tn, preferred_element_type=f32)
                d_state = part if d_state is None else d_state + part
            st[pr] = decay[:, cols] * st[pr] + d_state
            stb = st[pr].astype(bf16)
            qp = q_ref[0, rows, cols] * (B_KEY_DIM ** -0.5)
            for s in range(2):
                h = 2 * pr + s
                vcols = slice(h * B_VAL_DIM, (h + 1) * B_VAL_DIM)
                qm = jnp.where(upper == bool(s), qp, jnp.zeros_like(qp))
                o = lax.dot_general(qm, stb, nt, preferred_element_type=f32)
                o = _rms(o, gn_ref[:, vcols])
                r = r_ref[0, rows, vcols].astype(f32)
                o_ref[0, rows, vcols] = (o * _silu(r)).astype(o_ref.dtype)
    for pr in range(B_HEADS // 2):
        state[pr] = st[pr]


def _gla(proj3, gd3, w_up, b_gate, gla_norm, tq=256):
    b, s, _ = proj3.shape
    return pl.pallas_call(
        _gla_kernel,
        grid=(b, s // tq),
        in_specs=[pl.BlockSpec((1, tq, B_K_WIDTH), lambda bi, i: (bi, i, COL_QB // B_K_WIDTH)),
                  pl.BlockSpec((1, tq, B_K_WIDTH), lambda bi, i: (bi, i, COL_KB // B_K_WIDTH)),
                  pl.BlockSpec((1, tq, B_V_WIDTH), lambda bi, i: (bi, i, COL_VB // B_V_WIDTH)),
                  pl.BlockSpec((1, tq, LANES), lambda bi, i: (bi, i, 0)),
                  pl.BlockSpec((1, tq, B_V_WIDTH), lambda bi, i: (bi, i, COL_R // B_V_WIDTH)),
                  _const_spec((LANES, B_K_WIDTH)),
                  _const_spec((1, B_K_WIDTH)),
                  _const_spec((1, B_V_WIDTH))],
        out_specs=pl.BlockSpec((1, tq, B_V_WIDTH), lambda bi, i: (bi, i, 0)),
        out_shape=jax.ShapeDtypeStruct((b, s, B_V_WIDTH), bf16),
        scratch_shapes=[pltpu.VMEM((B_HEADS // 2, LANES, B_VAL_DIM), f32)],
        compiler_params=_params("parallel", "arbitrary"),
        name="gla",
    )(proj3, proj3, proj3, gd3, proj3, w_up, b_gate, gla_norm)


def _route_top2(h, rw_ref):
    logits = jnp.dot(h.astype(bf16), rw_ref[...], preferred_element_type=f32)
    lane = lax.broadcasted_iota(jnp.int32, logits.shape, 1).astype(f32)
    neg = -jnp.inf
    lg = jnp.where(lane < N_EXPERTS, logits, neg)
    m1 = jnp.max(lg, axis=-1, keepdims=True)
    i1 = jnp.min(jnp.where(lg == m1, lane, float(LANES)), axis=-1, keepdims=True)
    lg2 = jnp.where(lane == i1, neg, lg)
    m2 = jnp.max(lg2, axis=-1, keepdims=True)
    i2 = jnp.min(jnp.where(lg2 == m2, lane, float(LANES)), axis=-1, keepdims=True)
    e2 = jnp.exp(m2 - m1)
    w1 = 1.0 / (1.0 + e2)
    w2 = e2 / (1.0 + e2)
    return jnp.where(lane == 0, i1, jnp.where(lane == 1, i2, jnp.where(lane == 2, w1, jnp.where(lane == 3, w2, 0.0))))


def _mix_out_kernel(*refs, routed):
    if routed:
        (ya_ref, yb_ref, ga_ref, gb_ref, x_ref, wa_ref, wb_ref, wo_ref, gpost_ref, gpre_ref, rw_ref,
         xo_ref, ho_ref, route_ref) = refs
    else:
        (ya_ref, yb_ref, ga_ref, gb_ref, x_ref, wa_ref, wb_ref, wo_ref, gpost_ref, gpre_ref,
         xo_ref, ho_ref) = refs
    a = jnp.dot(ya_ref[...], wa_ref[...], preferred_element_type=f32)
    b = jnp.dot(yb_ref[...], wb_ref[...], preferred_element_type=f32)
    merged = _sigmoid(ga_ref[...].astype(f32)) * a + _sigmoid(gb_ref[...].astype(f32)) * b
    y = jnp.dot(merged.astype(bf16), wo_ref[...], preferred_element_type=f32)
    x1 = x_ref[...] + _rms(y, gpost_ref[...])
    h = _rms(x1, gpre_ref[...])
    xo_ref[...] = x1
    ho_ref[...] = h.astype(ho_ref.dtype)
    if routed:
        route_ref[...] = _route_top2(h, rw_ref)


def _mix_out(ya, yb, proj, x, wa, wb, wo, g_post, g_pre, router_w=None, tm=512):
    t = x.shape[0]
    routed = router_w is not None
    row = lambda i: (i, 0)
    in_specs = [pl.BlockSpec((tm, A_WIDTH), row),
                pl.BlockSpec((tm, B_V_WIDTH), row),
                pl.BlockSpec((tm, D_MODEL), lambda i: (i, COL_GA // D_MODEL)),
                pl.BlockSpec((tm, D_MODEL), lambda i: (i, COL_GB // D_MODEL)),
                pl.BlockSpec((tm, D_MODEL), row),
                _const_spec((A_WIDTH, D_MODEL)),
                _const_spec((B_V_WIDTH, D_MODEL)),
                _const_spec((D_MODEL, D_MODEL)),
                _const_spec((1, D_MODEL)),
                _const_spec((1, D_MODEL))]
    args = [ya, yb, proj, proj, x, wa, wb, wo, g_post, g_pre]
    out_specs = [pl.BlockSpec((tm, D_MODEL), row), pl.BlockSpec((tm, D_MODEL), row)]
    out_shape = [jax.ShapeDtypeStruct((t, D_MODEL), f32),
                 jax.ShapeDtypeStruct((t, D_MODEL), f32 if routed else bf16)]
    if routed:
        in_specs.append(_const_spec((D_MODEL, LANES)))
        args.append(router_w)
        out_specs.append(pl.BlockSpec((tm, LANES), row))
        out_shape.append(jax.ShapeDtypeStruct((t, LANES), f32))
    return pl.pallas_call(
        functools.partial(_mix_out_kernel, routed=routed),
        grid=(t // tm,),
        in_specs=in_specs,
        out_specs=out_specs,
        out_shape=out_shape,
        compiler_params=_params("parallel"),
        name="mix_out",
    )(*args)


def _post_ffn(y, x1, p, gpost_ref, wpp_ref, wpg_ref, gple_ref):
    x2 = x1 + _rms(y, gpost_ref[...])
    e = jnp.dot(p.astype(bf16), wpp_ref[...], preferred_element_type=f32)
    e = e * _sigmoid(jnp.dot(x2.astype(bf16), wpg_ref[...], preferred_element_type=f32))
    return x2 + _rms(e, gple_ref[...])


def _swiglu_chunks(x, wg_ref, wu_ref, wd_ref, lead, width):
    acc = None
    for c0, c1 in _chunks(width, FF_CHUNK):
        g = jnp.dot(x, wg_ref[lead + (slice(None), slice(c0, c1))], preferred_element_type=f32)
        u = jnp.dot(x, wu_ref[lead + (slice(None), slice(c0, c1))], preferred_element_type=f32)
        act = (_silu(g) * u).astype(bf16)
        part = jnp.dot(act, wd_ref[lead + (slice(c0, c1), slice(None))], preferred_element_type=f32)
        acc = part if acc is None else acc + part
    return acc


def _dense_ffn_kernel(h_ref, x_ref, p_ref, wg_ref, wu_ref, wd_ref, gpost_ref, wpp_ref, wpg_ref, gple_ref, o_ref):
    y = _swiglu_chunks(h_ref[...], wg_ref, wu_ref, wd_ref, (), wg_ref.shape[1])
    o_ref[...] = _post_ffn(y, x_ref[...], p_ref[...], gpost_ref, wpp_ref, wpg_ref, gple_ref)


def _dense_ffn(h, x1, p, wg, wu, wd, g_post, wpp, wpg, g_ple, tm=512):
    t = x1.shape[0]
    ff = wg.shape[1]
    row = lambda i: (i, 0)
    return pl.pallas_call(
        _dense_ffn_kernel,
        grid=(t // tm,),
        in_specs=[pl.BlockSpec((tm, D_MODEL), row),
                  pl.BlockSpec((tm, D_MODEL), row),
                  pl.BlockSpec((tm, PLE_DIM), row),
                  _const_spec((D_MODEL, ff)),
                  _const_spec((D_MODEL, ff)),
                  _const_spec((ff, D_MODEL)),
                  _const_spec((1, D_MODEL)),
                  _const_spec((PLE_DIM, D_MODEL)),
                  _const_spec((D_MODEL, D_MODEL)),
                  _const_spec((1, D_MODEL))],
        out_specs=pl.BlockSpec((tm, D_MODEL), row),
        out_shape=jax.ShapeDtypeStruct((t, D_MODEL), f32),
        compiler_params=_params("parallel"),
        name="dense_ffn",
    )(h, x1, p, wg, wu, wd, g_post, wpp, wpg, g_ple)


def _row_copy(src, s, dst, d, sem):
    return pltpu.make_async_copy(src.at[pl.ds(s, 1)], dst.at[pl.ds(d, 1)], sem)


def _scatter_kernel(pos_ref, h_ref, xs_in, xs_out, sem):
    del xs_in
    rows = h_ref.shape[0]

    def issue(r, c):
        for k in range(2):
            _row_copy(h_ref, r, xs_out, pos_ref[2 * r + k], sem.at[k]).start()
        return c

    lax.fori_loop(0, rows, issue, 0, unroll=ISSUE_UNROLL)
    for k in range(2):
        pltpu.make_async_copy(h_ref, xs_out.at[pl.ds(0, rows)], sem.at[k]).wait()


def _scatter_rows(pos, h, xs):
    t = h.shape[0]
    return pl.pallas_call(
        _scatter_kernel,
        grid=(t // ROW_BLK,),
        in_specs=[pl.BlockSpec((2 * ROW_BLK,), lambda i: (i,), memory_space=pltpu.SMEM),
                  pl.BlockSpec((ROW_BLK, D_MODEL), lambda i: (i, 0)),
                  pl.BlockSpec(memory_space=pl.ANY)],
        out_specs=pl.BlockSpec(memory_space=pl.ANY),
        out_shape=jax.ShapeDtypeStruct(xs.shape, xs.dtype),
        scratch_shapes=[pltpu.SemaphoreType.DMA((2,))],
        input_output_aliases={2: 0},
        compiler_params=_params("arbitrary"),
        name="moe_scatter",
    )(pos, h, xs)


def _expert_kernel(te_ref, nu_ref, x_ref, wg_ref, wu_ref, wd_ref, o_ref, acc_ref):
    del te_ref
    i = pl.program_id(0)
    f = pl.program_id(1)

    @pl.when(i < nu_ref[0])
    def _():
        y = _swiglu_chunks(x_ref[...].astype(bf16), wg_ref, wu_ref, wd_ref, (0,), wg_ref.shape[2])

        @pl.when(f == 0)
        def _():
            acc_ref[...] = y

        @pl.when(f > 0)
        def _():
            acc_ref[...] += y

    last = f == pl.num_programs(1) - 1

    @pl.when(jnp.logical_and(last, i < nu_ref[0]))
    def _():
        o_ref[...] = acc_ref[...]

    @pl.when(jnp.logical_and(last, i >= nu_ref[0]))
    def _():
        o_ref[...] = jnp.zeros_like(o_ref)


def _expert_ffn(tile_expert, n_used, xs, wg, wu, wd, tf=1792):
    n_rows = xs.shape[0]
    ff = wg.shape[2]
    nf = ff // tf

    def f_idx(i, f, nu):
        return jnp.where(i < nu[0], f, nf - 1)

    return pl.pallas_call(
        _expert_kernel,
        grid_spec=pltpu.PrefetchScalarGridSpec(
            num_scalar_prefetch=2,
            grid=(n_rows // MOE_TM, nf),
            in_specs=[pl.BlockSpec((MOE_TM, D_MODEL), lambda i, f, te, nu: (i, 0)),
                      pl.BlockSpec((1, D_MODEL, tf), lambda i, f, te, nu: (te[i], 0, f_idx(i, f, nu))),
                      pl.BlockSpec((1, D_MODEL, tf), lambda i, f, te, nu: (te[i], 0, f_idx(i, f, nu))),
                      pl.BlockSpec((1, tf, D_MODEL), lambda i, f, te, nu: (te[i], f_idx(i, f, nu), 0))],
            out_specs=pl.BlockSpec((MOE_TM, D_MODEL), lambda i, f, te, nu: (i, 0)),
            scratch_shapes=[pltpu.VMEM((MOE_TM, D_MODEL), f32)]),
        out_shape=jax.ShapeDtypeStruct((n_rows, D_MODEL), f32),
        compiler_params=_params("arbitrary", "arbitrary"),
        name="moe_experts",
    )(tile_expert, n_used, xs, wg, wu, wd)


def _combine_kernel(pos_ref, ys_hbm, route_ref, x_ref, p_ref, gpost_ref, wpp_ref, wpg_ref, gple_ref,
                    o_ref, buf, sem):
    rows = x_ref.shape[0]

    def issue(r, c):
        for k in range(2):
            _row_copy(ys_hbm, pos_ref[2 * r + k], buf.at[k], r, sem.at[k]).start()
        return c

    lax.fori_loop(0, rows, issue, 0, unroll=ISSUE_UNROLL)
    for k in range(2):
        pltpu.make_async_copy(ys_hbm.at[pl.ds(0, rows)], buf.at[k], sem.at[k]).wait()
    route = route_ref[...]
    y = route[:, 2:3] * buf[0] + route[:, 3:4] * buf[1]
    o_ref[...] = _post_ffn(y, x_ref[...], p_ref[...], gpost_ref, wpp_ref, wpg_ref, gple_ref)


def _combine(pos, ys, route, x1, p, g_post, wpp, wpg, g_ple):
    t = x1.shape[0]
    row = lambda i: (i, 0)
    return pl.pallas_call(
        _combine_kernel,
        grid=(t // ROW_BLK,),
        in_specs=[pl.BlockSpec((2 * ROW_BLK,), lambda i: (i,), memory_space=pltpu.SMEM),
                  pl.BlockSpec(memory_space=pl.ANY),
                  pl.BlockSpec((ROW_BLK, LANES), row),
                  pl.BlockSpec((ROW_BLK, D_MODEL), row),
                  pl.BlockSpec((ROW_BLK, PLE_DIM), row),
                  _const_spec((1, D_MODEL)),
                  _const_spec((PLE_DIM, D_MODEL)),
                  _const_spec((D_MODEL, D_MODEL)),
                  _const_spec((1, D_MODEL))],
        out_specs=pl.BlockSpec((ROW_BLK, D_MODEL), row),
        out_shape=jax.ShapeDtypeStruct((t, D_MODEL), f32),
        scratch_shapes=[pltpu.VMEM((2, ROW_BLK, D_MODEL), f32), pltpu.SemaphoreType.DMA((2,))],
        compiler_params=_params("arbitrary"),
        name="moe_combine",
    )(pos, ys, route, x1, p, g_post, wpp, wpg, g_ple)


def _route_slots(route, n_tiles):
    ids = route[:, 0:2].astype(jnp.int32).reshape(-1)
    onehot = (ids[None, :] == jnp.arange(N_EXPERTS, dtype=jnp.int32)[:, None]).astype(jnp.int32)
    csum = jnp.cumsum(onehot, axis=1)
    rank = jnp.sum(onehot * csum, axis=0) - 1
    counts = csum[:, -1]
    padded = ((counts + MOE_TM - 1) // MOE_TM) * MOE_TM
    ends = jnp.cumsum(padded)
    pos = (ends - padded)[ids] + rank
    tile_start = jnp.arange(n_tiles, dtype=jnp.int32) * MOE_TM
    tile_expert = jnp.minimum(jnp.sum((tile_start[:, None] >= ends[None, :]).astype(jnp.int32), axis=1),
                              N_EXPERTS - 1)
    n_used = (ends[-1] // MOE_TM).reshape(1)
    return pos.astype(jnp.int32), tile_expert.astype(jnp.int32), n_used.astype(jnp.int32)


def _moe(h, route, x1, p, wg, wu, wd, g_post, wpp, wpg, g_ple):
    t = h.shape[0]
    n_tiles = 2 * t // MOE_TM + N_EXPERTS
    pos, tile_expert, n_used = _route_slots(route, n_tiles)
    xs = _scatter_rows(pos, h, jnp.zeros((n_tiles * MOE_TM, D_MODEL), f32))
    ys = _expert_ffn(tile_expert, n_used, xs, wg, wu, wd)
    return _combine(pos, ys, route, x1, p, g_post, wpp, wpg, g_ple)


def _prep_w_in(w):
    main = jnp.concatenate([w[:, SRC_KA:SRC_VA], w[:, SRC_QB:SRC_GD], w[:, SRC_R:SRC_END]], axis=1)
    wq_t = w[:, SRC_QA:SRC_KA].T
    wv_t = w[:, SRC_VA:SRC_QB].T
    w_gd = jnp.pad(w[:, SRC_GD:SRC_R], ((0, 0), (0, LANES - GATE_RANK)))
    return main.astype(bf16), wq_t.astype(bf16), wv_t.astype(bf16), w_gd.astype(bf16)


def kernel(x, p, w_in, rel_bias, w_gla_gate_up, b_gla_gate, gla_norm, w_branch_a, w_branch_b, w_out, norm_mix_pre, norm_mix_post, norm_ffn_pre, norm_ffn_post, ffn_w_gate, ffn_w_up, ffn_w_down, router_w, moe_w_gate, moe_w_up, moe_w_down, ple_w_proj, ple_w_gate, ple_norm):
    b, s, d = x.shape
    t = b * s
    depth = w_in.shape[0]
    x = x.reshape(t, d)
    vec = lambda a: a.reshape(1, -1).astype(f32)
    for i in range(depth):
        w_main, wq_t, wv_t, w_gd = _prep_w_in(w_in[i])
        g_pre = vec(norm_mix_pre[i])
        proj = _in_proj(x, g_pre, w_main)
        qt, vt, gd = _qv_proj(x, g_pre, wq_t, wv_t, w_gd, b, s)
        proj3 = proj.reshape(b, s, MAIN_WIDTH)
        ya = _attention(qt, proj3, vt, _attn_bias(rel_bias[i])).reshape(t, A_WIDTH)
        w_up = jnp.pad(w_gla_gate_up[i], ((0, LANES - GATE_RANK), (0, 0))).astype(bf16)
        yb = _gla(proj3, gd.reshape(b, s, LANES), w_up, vec(b_gla_gate[i]), vec(gla_norm[i])).reshape(t, B_V_WIDTH)
        j = i // 2
        routed = i % 2 == 1
        rw = jnp.pad(router_w[j], ((0, 0), (0, LANES - N_EXPERTS))).astype(bf16) if routed else None
        outs = _mix_out(ya, yb, proj, x, w_branch_a[i].astype(bf16), w_branch_b[i].astype(bf16),
                        w_out[i].astype(bf16), vec(norm_mix_post[i]), vec(norm_ffn_pre[i]), rw)
        p_i = p[i].reshape(t, PLE_DIM)
        tail = (vec(norm_ffn_post[i]), ple_w_proj[i].astype(bf16), ple_w_gate[i].astype(bf16), vec(ple_norm[i]))
        if routed:
            x1, h, route = outs
            x = _moe(h, route, x1, p_i, moe_w_gate[j].astype(bf16), moe_w_up[j].astype(bf16),
                     moe_w_down[j].astype(bf16), *tail)
        else:
            x1, h = outs
            x = _dense_ffn(h, x1, p_i, ffn_w_gate[j].astype(bf16), ffn_w_up[j].astype(bf16),
                           ffn_w_down[j].astype(bf16), *tail)
    return x.reshape(b, s, d)
```

```python
import functools

import numpy as np
import jax
import jax.numpy as jnp
from jax import lax
from jax.experimental import pallas as pl
from jax.experimental.pallas import tpu as pltpu

f32 = jnp.float32
bf16 = jnp.bfloat16

D_MODEL = 1024
CHUNK = 64
N_PREV_CHUNKS = 8
A_HEADS = 8
A_HEAD_DIM = 64
A_WIDTH = A_HEADS * A_HEAD_DIM
REL_CLIP = 128
B_HEADS = 4
B_KEY_DIM = 64
B_VAL_DIM = 128
B_K_WIDTH = B_HEADS * B_KEY_DIM
B_V_WIDTH = B_HEADS * B_VAL_DIM
GATE_RANK = 16
GATE_TEMP = 16.0
N_EXPERTS = 8
PLE_DIM = 256
NORM_EPS = 1e-6

LANES = 128
VMEM_LIMIT = 56 * 1024 * 1024

SRC_QA, SRC_KA, SRC_VA, SRC_QB, SRC_GD, SRC_R = 0, 512, 1024, 1536, 2560, 2576
SRC_END = 5136
COL_KA, COL_QB, COL_KB, COL_VB, COL_R, COL_GA, COL_GB = 0, 512, 768, 1024, 1536, 2048, 3072
MAIN_WIDTH = 4096
IN_TILE_N = 2048

QBLK = 2 * CHUNK
KBAND = (N_PREV_CHUNKS + 2) * CHUNK
KPAD = N_PREV_CHUNKS * CHUNK

SUBLANES = 8
MOE_TM = 512
ZERO_ROWS = MOE_TM + SUBLANES
ROW_BLK = 512
FF_CHUNK = 512
ISSUE_UNROLL = 8


def _params(*sem):
    return pltpu.CompilerParams(dimension_semantics=sem, vmem_limit_bytes=VMEM_LIMIT)


def _rms(x, g):
    return x * lax.rsqrt(jnp.mean(x * x, axis=-1, keepdims=True) + NORM_EPS) * g


def _sigmoid(x):
    return 1.0 / (1.0 + jnp.exp(-x))


def _silu(x):
    return x * _sigmoid(x)


def _chunks(n, c):
    return [(s, min(s + c, n)) for s in range(0, n, c)]


def _const_spec(shape):
    nd = len(shape)
    return pl.BlockSpec(shape, lambda *_: (0,) * nd, pipeline_mode=pl.Buffered(1))


def _in_proj_kernel(x_ref, g_ref, w_ref, o_ref, h_ref):
    @pl.when(pl.program_id(1) == 0)
    def _():
        h_ref[...] = _rms(x_ref[...], g_ref[...]).astype(bf16)

    o_ref[...] = jnp.dot(h_ref[...], w_ref[...], preferred_element_type=f32).astype(o_ref.dtype)


def _in_proj(x, g, w, tm=1024):
    t = x.shape[0]
    n = w.shape[1]
    return pl.pallas_call(
        _in_proj_kernel,
        grid=(t // tm, n // IN_TILE_N),
        in_specs=[pl.BlockSpec((tm, D_MODEL), lambda i, j: (i, 0)),
                  pl.BlockSpec((1, D_MODEL), lambda i, j: (0, 0)),
                  pl.BlockSpec((D_MODEL, IN_TILE_N), lambda i, j: (0, j))],
        out_specs=pl.BlockSpec((tm, IN_TILE_N), lambda i, j: (i, j)),
        out_shape=jax.ShapeDtypeStruct((t, n), bf16),
        scratch_shapes=[pltpu.VMEM((tm, D_MODEL), bf16)],
        compiler_params=_params("parallel", "arbitrary"),
        name="in_proj",
    )(x, g, w)


def _qv_proj_kernel(x_ref, g_ref, wq_ref, wv_ref, wgd_ref, qt_ref, vt_ref, gd_ref):
    nt = (((1,), (1,)), ((), ()))
    h = _rms(x_ref[...], g_ref[...]).astype(bf16)
    qt_ref[0] = lax.dot_general(wq_ref[...], h, nt, preferred_element_type=f32).astype(qt_ref.dtype)
    vt = lax.dot_general(wv_ref[...], h, nt, preferred_element_type=f32).astype(vt_ref.dtype)
    for kt in range(vt_ref.shape[1]):
        vt_ref[0, kt] = vt[:, kt * LANES:(kt + 1) * LANES]
    gd_ref[...] = jnp.dot(h, wgd_ref[...], preferred_element_type=f32).astype(gd_ref.dtype)


def _qv_proj(x, g, wq_t, wv_t, w_gd, b, s, tm=1024):
    t = x.shape[0]
    per_b = s // tm
    return pl.pallas_call(
        _qv_proj_kernel,
        grid=(t // tm,),
        in_specs=[pl.BlockSpec((tm, D_MODEL), lambda i: (i, 0)),
                  _const_spec((1, D_MODEL)),
                  _const_spec((A_WIDTH, D_MODEL)),
                  _const_spec((A_WIDTH, D_MODEL)),
                  _const_spec((D_MODEL, LANES))],
        out_specs=[pl.BlockSpec((1, A_WIDTH, tm), lambda i: (i // per_b, 0, i % per_b)),
                   pl.BlockSpec((1, tm // LANES, A_WIDTH, LANES), lambda i: (i // per_b, i % per_b, 0, 0)),
                   pl.BlockSpec((tm, LANES), lambda i: (i, 0))],
        out_shape=[jax.ShapeDtypeStruct((b, A_WIDTH, s), bf16),
                   jax.ShapeDtypeStruct((b, s // LANES, A_WIDTH, LANES), bf16),
                   jax.ShapeDtypeStruct((t, LANES), bf16)],
        compiler_params=_params("parallel"),
        name="qv_proj",
    )(x, g, wq_t, wv_t, w_gd)


def _attn_kernel(qt_ref, k_ref, vt_ref, bias_ref, o_ref):
    i = pl.program_id(1)
    row = lax.broadcasted_iota(jnp.int32, (LANES, QBLK), 0)
    upper = row >= A_HEAD_DIM
    n_kt = KBAND // LANES
    kt_seq = [i + kt - KPAD // LANES for kt in range(n_kt)]
    kt_clamped = [jnp.maximum(j, 0) for j in kt_seq]
    for hp in range(A_WIDTH // LANES):
        pair = slice(hp * LANES, (hp + 1) * LANES)
        qt = qt_ref[0, pair, :] * (A_HEAD_DIM ** -0.5)
        kps = [k_ref[0, pl.ds(pl.multiple_of(j * LANES, LANES), LANES), pair] for j in kt_clamped]
        outs = []
        for s in range(2):
            h = 2 * hp + s
            qm = jnp.where(upper == bool(s), qt, jnp.zeros_like(qt))
            tiles = []
            for kt in range(n_kt):
                t = jnp.dot(kps[kt], qm, preferred_element_type=f32) + bias_ref[h, kt]
                tiles.append(jnp.where(kt_seq[kt] >= 0, t, -1e30))
            m = tiles[0]
            for t in tiles[1:]:
                m = jnp.maximum(m, t)
            m = jnp.max(m, axis=0, keepdims=True)
            l = None
            o = None
            for kt in range(n_kt):
                p = jnp.exp(tiles[kt] - m)
                ls = jnp.sum(p, axis=0, keepdims=True)
                vt = vt_ref[0, kt_clamped[kt], h * A_HEAD_DIM:(h + 1) * A_HEAD_DIM, :]
                part = jnp.dot(vt, p.astype(bf16), preferred_element_type=f32)
                l = ls if l is None else l + ls
                o = part if o is None else o + part
            outs.append(o / l)
        o_ref[0, :, pair] = jnp.concatenate(outs, axis=0).T.astype(o_ref.dtype)


def _attention(qt, proj3, vt, bias_t):
    b, s, _ = proj3.shape
    return pl.pallas_call(
        _attn_kernel,
        grid=(b, s // QBLK),
        in_specs=[pl.BlockSpec((1, A_WIDTH, QBLK), lambda bi, i: (bi, 0, i)),
                  pl.BlockSpec((1, s, A_WIDTH), lambda bi, i: (bi, 0, COL_KA // A_WIDTH)),
                  pl.BlockSpec((1, s // LANES, A_WIDTH, LANES), lambda bi, i: (bi, 0, 0, 0)),
                  _const_spec((A_HEADS, KBAND // LANES, LANES, QBLK))],
        out_specs=pl.BlockSpec((1, QBLK, A_WIDTH), lambda bi, i: (bi, i, 0)),
        out_shape=jax.ShapeDtypeStruct((b, s, A_WIDTH), bf16),
        compiler_params=_params("parallel", "arbitrary"),
        name="chunk_attention",
    )(qt, proj3, vt, bias_t)


def _attn_bias(rel_bias):
    h = rel_bias.shape[0]
    r = np.arange(QBLK)[:, None]
    k = np.arange(KBAND)[None, :]
    visible = np.where(r < CHUNK, k < KBAND - CHUNK, k >= CHUNK)
    span = QBLK + KBAND - 1
    n_far = span - 2 * REL_CLIP
    line = jnp.concatenate([jnp.broadcast_to(rel_bias[:, -1:], (h, n_far)), rel_bias[:, :0:-1]], axis=1)
    line = jnp.pad(line.astype(f32), ((0, 0), (0, 1)))
    skew = jnp.tile(line, (1, QBLK))[:, :QBLK * span].reshape(h, QBLK, span)
    bias = jnp.where(visible[None], skew[:, :, QBLK - 1:], -1e30)
    return jnp.swapaxes(bias, 1, 2).reshape(h, KBAND // LANES, LANES, QBLK)


def _gla_kernel(q_ref, k_ref, v_ref, gd_ref, r_ref, wup_ref, b_ref, gn_ref, o_ref, state):
    @pl.when(pl.program_id(1) == 0)
    def _():
        state[...] = jnp.zeros_like(state)

    tq = q_ref.shape[1]
    tn = (((0,), (0,)), ((), ()))
    nt = (((1,), (1,)), ((), ()))
    row = lax.broadcasted_iota(jnp.int32, (tq, tq), 0)
    col = lax.broadcasted_iota(jnp.int32, (tq, tq), 1)
    later = jnp.logical_and(col > row, col // CHUNK == row // CHUNK)
    later = jnp.where(later, 1.0, 0.0).astype(bf16)
    z = jnp.dot(gd_ref[0], wup_ref[...], preferred_element_type=f32) + b_ref[...]
    log_a = (jnp.minimum(z, 0.0) - jnp.log(1.0 + jnp.exp(-jnp.abs(z)))) / GATE_TEMP
    la_hi = log_a.astype(bf16)
    la_lo = (log_a - la_hi.astype(f32)).astype(bf16)
    rev = (jnp.dot(later, la_hi, preferred_element_type=f32)
           + jnp.dot(later, la_lo, preferred_element_type=f32))
    kdec = k_ref[0].astype(f32) * jnp.exp(rev)
    lane = lax.broadcasted_iota(jnp.int32, (CHUNK, LANES), 1)
    upper = lane >= B_KEY_DIM
    st = [state[pr] for pr in range(B_HEADS // 2)]
    for c in range(tq // CHUNK):
        rows = slice(c * CHUNK, (c + 1) * CHUNK)
        decay = jnp.exp(jnp.sum(log_a[rows], axis=0, keepdims=True))
        for pr in range(B_HEADS // 2):
            cols = slice(pr * LANES, (pr + 1) * LANES)
            kd = kdec[rows, cols]
            d_state = None
            for s in range(2):
                h = 2 * pr + s
                kdm = jnp.where(upper == bool(s), kd, 0.0).astype(bf16)
                vh = v_ref[0, rows, h * B_VAL_DIM:(h + 1) * B_VAL_DIM]
                part = lax.dot_general(vh, kdm, tn, preferred_element_type=f32)
                d_state = part if d_state is None else d_state + part
            st[pr] = decay[:, cols] * st[pr] + d_state
            stb = st[pr].astype(bf16)
            qp = q_ref[0, rows, cols] * (B_KEY_DIM ** -0.5)
            for s in range(2):
                h = 2 * pr + s
                vcols = slice(h * B_VAL_DIM, (h + 1) * B_VAL_DIM)
                qm = jnp.where(upper == bool(s), qp, jnp.zeros_like(qp))
                o = lax.dot_general(qm, stb, nt, preferred_element_type=f32)
                o = _rms(o, gn_ref[:, vcols])
                r = r_ref[0, rows, vcols].astype(f32)
                o_ref[0, rows, vcols] = (o * _silu(r)).astype(o_ref.dtype)
    for pr in range(B_HEADS // 2):
        state[pr] = st[pr]


def _gla(proj3, gd3, w_up, b_gate, gla_norm, tq=256):
    b, s, _ = proj3.shape
    return pl.pallas_call(
        _gla_kernel,
        grid=(b, s // tq),
        in_specs=[pl.BlockSpec((1, tq, B_K_WIDTH), lambda bi, i: (bi, i, COL_QB // B_K_WIDTH)),
                  pl.BlockSpec((1, tq, B_K_WIDTH), lambda bi, i: (bi, i, COL_KB // B_K_WIDTH)),
                  pl.BlockSpec((1, tq, B_V_WIDTH), lambda bi, i: (bi, i, COL_VB // B_V_WIDTH)),
                  pl.BlockSpec((1, tq, LANES), lambda bi, i: (bi, i, 0)),
                  pl.BlockSpec((1, tq, B_V_WIDTH), lambda bi, i: (bi, i, COL_R // B_V_WIDTH)),
                  _const_spec((LANES, B_K_WIDTH)),
                  _const_spec((1, B_K_WIDTH)),
                  _const_spec((1, B_V_WIDTH))],
        out_specs=pl.BlockSpec((1, tq, B_V_WIDTH), lambda bi, i: (bi, i, 0)),
        out_shape=jax.ShapeDtypeStruct((b, s, B_V_WIDTH), bf16),
        scratch_shapes=[pltpu.VMEM((B_HEADS // 2, LANES, B_VAL_DIM), f32)],
        compiler_params=_params("parallel", "arbitrary"),
        name="gla",
    )(proj3, proj3, proj3, gd3, proj3, w_up, b_gate, gla_norm)


def _route_top2(h, rw_ref):
    logits = jnp.dot(h.astype(bf16), rw_ref[...], preferred_element_type=f32)
    lane = lax.broadcasted_iota(jnp.int32, logits.shape, 1).astype(f32)
    neg = -jnp.inf
    lg = jnp.where(lane < N_EXPERTS, logits, neg)
    m1 = jnp.max(lg, axis=-1, keepdims=True)
    i1 = jnp.min(jnp.where(lg == m1, lane, float(LANES)), axis=-1, keepdims=True)
    lg2 = jnp.where(lane == i1, neg, lg)
    m2 = jnp.max(lg2, axis=-1, keepdims=True)
    i2 = jnp.min(jnp.where(lg2 == m2, lane, float(LANES)), axis=-1, keepdims=True)
    e2 = jnp.exp(m2 - m1)
    w1 = 1.0 / (1.0 + e2)
    w2 = e2 / (1.0 + e2)
    return jnp.where(lane == 0, i1, jnp.where(lane == 1, i2, jnp.where(lane == 2, w1, jnp.where(lane == 3, w2, 0.0))))


def _mix_out_kernel(*refs, routed):
    if routed:
        (ya_ref, yb_ref, ga_ref, gb_ref, x_ref, wa_ref, wb_ref, wo_ref, gpost_ref, gpre_ref, rw_ref,
         xo_ref, ho_ref, route_ref) = refs
    else:
        (ya_ref, yb_ref, ga_ref, gb_ref, x_ref, wa_ref, wb_ref, wo_ref, gpost_ref, gpre_ref,
         xo_ref, ho_ref) = refs
    a = jnp.dot(ya_ref[...], wa_ref[...], preferred_element_type=f32)
    b = jnp.dot(yb_ref[...], wb_ref[...], preferred_element_type=f32)
    merged = _sigmoid(ga_ref[...].astype(f32)) * a + _sigmoid(gb_ref[...].astype(f32)) * b
    y = jnp.dot(merged.astype(bf16), wo_ref[...], preferred_element_type=f32)
    x1 = x_ref[...] + _rms(y, gpost_ref[...])
    h = _rms(x1, gpre_ref[...])
    xo_ref[...] = x1
    ho_ref[...] = h.astype(ho_ref.dtype)
    if routed:
        route_ref[...] = _route_top2(h, rw_ref)


def _mix_out(ya, yb, proj, x, wa, wb, wo, g_post, g_pre, router_w=None, tm=512):
    t = x.shape[0]
    routed = router_w is not None
    row = lambda i: (i, 0)
    in_specs = [pl.BlockSpec((tm, A_WIDTH), row),
                pl.BlockSpec((tm, B_V_WIDTH), row),
                pl.BlockSpec((tm, D_MODEL), lambda i: (i, COL_GA // D_MODEL)),
                pl.BlockSpec((tm, D_MODEL), lambda i: (i, COL_GB // D_MODEL)),
                pl.BlockSpec((tm, D_MODEL), row),
                _const_spec((A_WIDTH, D_MODEL)),
                _const_spec((B_V_WIDTH, D_MODEL)),
                _const_spec((D_MODEL, D_MODEL)),
                _const_spec((1, D_MODEL)),
                _const_spec((1, D_MODEL))]
    args = [ya, yb, proj, proj, x, wa, wb, wo, g_post, g_pre]
    out_specs = [pl.BlockSpec((tm, D_MODEL), row), pl.BlockSpec((tm, D_MODEL), row)]
    out_shape = [jax.ShapeDtypeStruct((t, D_MODEL), f32),
                 jax.ShapeDtypeStruct((t, D_MODEL), f32 if routed else bf16)]
    if routed:
        in_specs.append(_const_spec((D_MODEL, LANES)))
        args.append(router_w)
        out_specs.append(pl.BlockSpec((tm, LANES), row))
        out_shape.append(jax.ShapeDtypeStruct((t, LANES), f32))
    return pl.pallas_call(
        functools.partial(_mix_out_kernel, routed=routed),
        grid=(t // tm,),
        in_specs=in_specs,
        out_specs=out_specs,
        out_shape=out_shape,
        compiler_params=_params("parallel"),
        name="mix_out",
    )(*args)


def _post_ffn(y, x1, p, gpost_ref, wpp_ref, wpg_ref, gple_ref):
    x2 = x1 + _rms(y, gpost_ref[...])
    e = jnp.dot(p.astype(bf16), wpp_ref[...], preferred_element_type=f32)
    e = e * _sigmoid(jnp.dot(x2.astype(bf16), wpg_ref[...], preferred_element_type=f32))
    return x2 + _rms(e, gple_ref[...])


def _swiglu_chunks(x, wg_ref, wu_ref, wd_ref, lead, width):
    acc = None
    for c0, c1 in _chunks(width, FF_CHUNK):
        g = jnp.dot(x, wg_ref[lead + (slice(None), slice(c0, c1))], preferred_element_type=f32)
        u = jnp.dot(x, wu_ref[lead + (slice(None), slice(c0, c1))], preferred_element_type=f32)
        act = (_silu(g) * u).astype(bf16)
        part = jnp.dot(act, wd_ref[lead + (slice(c0, c1), slice(None))], preferred_element_type=f32)
        acc = part if acc is None else acc + part
    return acc


def _dense_ffn_kernel(h_ref, x_ref, p_ref, wg_ref, wu_ref, wd_ref, gpost_ref, wpp_ref, wpg_ref, gple_ref, o_ref):
    y = _swiglu_chunks(h_ref[...], wg_ref, wu_ref, wd_ref, (), wg_ref.shape[1])
    o_ref[...] = _post_ffn(y, x_ref[...], p_ref[...], gpost_ref, wpp_ref, wpg_ref, gple_ref)


def _dense_ffn(h, x1, p, wg, wu, wd, g_post, wpp, wpg, g_ple, tm=512):
    t = x1.shape[0]
    ff = wg.shape[1]
    row = lambda i: (i, 0)
    return pl.pallas_call(
        _dense_ffn_kernel,
        grid=(t // tm,),
        in_specs=[pl.BlockSpec((tm, D_MODEL), row),
                  pl.BlockSpec((tm, D_MODEL), row),
                  pl.BlockSpec((tm, PLE_DIM), row),
                  _const_spec((D_MODEL, ff)),
                  _const_spec((D_MODEL, ff)),
                  _const_spec((ff, D_MODEL)),
                  _const_spec((1, D_MODEL)),
                  _const_spec((PLE_DIM, D_MODEL)),
                  _const_spec((D_MODEL, D_MODEL)),
                  _const_spec((1, D_MODEL))],
        out_specs=pl.BlockSpec((tm, D_MODEL), row),
        out_shape=jax.ShapeDtypeStruct((t, D_MODEL), f32),
        compiler_params=_params("parallel"),
        name="dense_ffn",
    )(h, x1, p, wg, wu, wd, g_post, wpp, wpg, g_ple)


def _row_copy(src, s, dst, d, sem):
    return pltpu.make_async_copy(src.at[pl.ds(s, 1)], dst.at[pl.ds(d, 1)], sem)


def _scatter_kernel(pad_ref, pos_ref, h_ref, xs_out, zbuf, sem, zsem):
    rows = h_ref.shape[0]

    @pl.when(pl.program_id(0) == 0)
    def _():
        zbuf[...] = jnp.zeros_like(zbuf)
        for e in range(N_EXPERTS):
            start = pl.multiple_of(pad_ref[e], SUBLANES)
            zero_fill = pltpu.make_async_copy(zbuf, xs_out.at[pl.ds(start, ZERO_ROWS)], zsem)
            zero_fill.start()
            zero_fill.wait()

        def zero_tile(tile, c):
            start = pl.multiple_of(tile * MOE_TM, MOE_TM)
            fill = pltpu.make_async_copy(zbuf.at[pl.ds(0, MOE_TM)], xs_out.at[pl.ds(start, MOE_TM)], zsem)
            fill.start()
            fill.wait()
            return c

        lax.fori_loop(pad_ref[N_EXPERTS], xs_out.shape[0] // MOE_TM, zero_tile, 0)

    def issue(r, c):
        for k in range(2):
            _row_copy(h_ref, r, xs_out, pos_ref[2 * r + k], sem.at[k]).start(priority=k)
        return c

    lax.fori_loop(0, rows, issue, 0, unroll=ISSUE_UNROLL)
    for k in range(2):
        pltpu.make_async_copy(h_ref, xs_out.at[pl.ds(0, rows)], sem.at[k]).wait()


def _scatter_rows(pad_start, pos, h, n_rows):
    t = h.shape[0]
    return pl.pallas_call(
        _scatter_kernel,
        grid_spec=pltpu.PrefetchScalarGridSpec(
            num_scalar_prefetch=1,
            grid=(t // ROW_BLK,),
            in_specs=[pl.BlockSpec((2 * ROW_BLK,), lambda i, pad: (i,), memory_space=pltpu.SMEM),
                      pl.BlockSpec((ROW_BLK, D_MODEL), lambda i, pad: (i, 0))],
            out_specs=pl.BlockSpec(memory_space=pl.ANY),
            scratch_shapes=[pltpu.VMEM((ZERO_ROWS, D_MODEL), f32),
                            pltpu.SemaphoreType.DMA((2,)),
                            pltpu.SemaphoreType.DMA(())]),
        out_shape=jax.ShapeDtypeStruct((n_rows, D_MODEL), f32),
        compiler_params=_params("arbitrary"),
        name="moe_scatter",
    )(pad_start, pos, h)


def _expert_kernel(te_ref, nu_ref, x_ref, wg_ref, wu_ref, wd_ref, o_ref, acc_ref):
    del te_ref
    i = pl.program_id(0)
    f = pl.program_id(1)

    @pl.when(i < nu_ref[0])
    def _():
        y = _swiglu_chunks(x_ref[...].astype(bf16), wg_ref, wu_ref, wd_ref, (0,), wg_ref.shape[2])

        @pl.when(f == 0)
        def _():
            acc_ref[...] = y

        @pl.when(f > 0)
        def _():
            acc_ref[...] += y

    last = f == pl.num_programs(1) - 1

    @pl.when(jnp.logical_and(last, i < nu_ref[0]))
    def _():
        o_ref[...] = acc_ref[...]

    @pl.when(jnp.logical_and(last, i >= nu_ref[0]))
    def _():
        o_ref[...] = jnp.zeros_like(o_ref)


def _expert_ffn(tile_expert, n_used, xs, wg, wu, wd, tf=1792):
    n_rows = xs.shape[0]
    ff = wg.shape[2]
    nf = ff // tf

    def f_idx(i, f, nu):
        return jnp.where(i < nu[0], f, nf - 1)

    return pl.pallas_call(
        _expert_kernel,
        grid_spec=pltpu.PrefetchScalarGridSpec(
            num_scalar_prefetch=2,
            grid=(n_rows // MOE_TM, nf),
            in_specs=[pl.BlockSpec((MOE_TM, D_MODEL), lambda i, f, te, nu: (i, 0)),
                      pl.BlockSpec((1, D_MODEL, tf), lambda i, f, te, nu: (te[i], 0, f_idx(i, f, nu))),
                      pl.BlockSpec((1, D_MODEL, tf), lambda i, f, te, nu: (te[i], 0, f_idx(i, f, nu))),
                      pl.BlockSpec((1, tf, D_MODEL), lambda i, f, te, nu: (te[i], f_idx(i, f, nu), 0))],
            out_specs=pl.BlockSpec((MOE_TM, D_MODEL), lambda i, f, te, nu: (i, 0)),
            scratch_shapes=[pltpu.VMEM((MOE_TM, D_MODEL), f32)]),
        out_shape=jax.ShapeDtypeStruct((n_rows, D_MODEL), f32),
        compiler_params=_params("arbitrary", "arbitrary"),
        name="moe_experts",
    )(tile_expert, n_used, xs, wg, wu, wd)


def _combine_kernel(pos_ref, ys_hbm, route_ref, x_ref, p_ref, gpost_ref, wpp_ref, wpg_ref, gple_ref,
                    o_ref, buf, sem):
    rows = x_ref.shape[0]

    def issue(r, c):
        for k in range(2):
            _row_copy(ys_hbm, pos_ref[2 * r + k], buf.at[k], r, sem.at[k]).start(priority=k)
        return c

    lax.fori_loop(0, rows, issue, 0, unroll=ISSUE_UNROLL)
    for k in range(2):
        pltpu.make_async_copy(ys_hbm.at[pl.ds(0, rows)], buf.at[k], sem.at[k]).wait()
    route = route_ref[...]
    y = route[:, 2:3] * buf[0] + route[:, 3:4] * buf[1]
    o_ref[...] = _post_ffn(y, x_ref[...], p_ref[...], gpost_ref, wpp_ref, wpg_ref, gple_ref)


def _combine(pos, ys, route, x1, p, g_post, wpp, wpg, g_ple):
    t = x1.shape[0]
    row = lambda i: (i, 0)
    return pl.pallas_call(
        _combine_kernel,
        grid=(t // ROW_BLK,),
        in_specs=[pl.BlockSpec((2 * ROW_BLK,), lambda i: (i,), memory_space=pltpu.SMEM),
                  pl.BlockSpec(memory_space=pl.ANY),
                  pl.BlockSpec((ROW_BLK, LANES), row),
                  pl.BlockSpec((ROW_BLK, D_MODEL), row),
                  pl.BlockSpec((ROW_BLK, PLE_DIM), row),
                  _const_spec((1, D_MODEL)),
                  _const_spec((PLE_DIM, D_MODEL)),
                  _const_spec((D_MODEL, D_MODEL)),
                  _const_spec((1, D_MODEL))],
        out_specs=pl.BlockSpec((ROW_BLK, D_MODEL), row),
        out_shape=jax.ShapeDtypeStruct((t, D_MODEL), f32),
        scratch_shapes=[pltpu.VMEM((2, ROW_BLK, D_MODEL), f32), pltpu.SemaphoreType.DMA((2,))],
        compiler_params=_params("arbitrary"),
        name="moe_combine",
    )(pos, ys, route, x1, p, g_post, wpp, wpg, g_ple)


def _route_slots(route, n_tiles):
    ids = route[:, 0:2].astype(jnp.int32).reshape(-1)
    onehot = (ids[None, :] == jnp.arange(N_EXPERTS, dtype=jnp.int32)[:, None]).astype(jnp.int32)
    csum = jnp.cumsum(onehot, axis=1)
    rank = jnp.sum(onehot * csum, axis=0) - 1
    counts = csum[:, -1]
    padded = ((counts + MOE_TM - 1) // MOE_TM) * MOE_TM
    ends = jnp.cumsum(padded)
    pos = (ends - padded)[ids] + rank
    tile_start = jnp.arange(n_tiles, dtype=jnp.int32) * MOE_TM
    tile_expert = jnp.minimum(jnp.sum((tile_start[:, None] >= ends[None, :]).astype(jnp.int32), axis=1),
                              N_EXPERTS - 1)
    n_used = (ends[-1] // MOE_TM).reshape(1)
    pad_start = ((ends - padded + counts) // SUBLANES) * SUBLANES
    pad_start = jnp.minimum(pad_start, n_tiles * MOE_TM - ZERO_ROWS)
    fill_plan = jnp.concatenate([pad_start, n_used])
    return (pos.astype(jnp.int32), tile_expert.astype(jnp.int32), n_used.astype(jnp.int32),
            fill_plan.astype(jnp.int32))


def _moe(h, route, x1, p, wg, wu, wd, g_post, wpp, wpg, g_ple):
    t = h.shape[0]
    n_tiles = 2 * t // MOE_TM + N_EXPERTS
    pos, tile_expert, n_used, pad_start = _route_slots(route, n_tiles)
    xs = _scatter_rows(pad_start, pos, h, n_tiles * MOE_TM)
    ys = _expert_ffn(tile_expert, n_used, xs, wg, wu, wd)
    return _combine(pos, ys, route, x1, p, g_post, wpp, wpg, g_ple)


def _prep_w_in(w):
    main = jnp.concatenate([w[:, SRC_KA:SRC_VA], w[:, SRC_QB:SRC_GD], w[:, SRC_R:SRC_END]], axis=1)
    wq_t = w[:, SRC_QA:SRC_KA].T
    wv_t = w[:, SRC_VA:SRC_QB].T
    w_gd = jnp.pad(w[:, SRC_GD:SRC_R], ((0, 0), (0, LANES - GATE_RANK)))
    return main.astype(bf16), wq_t.astype(bf16), wv_t.astype(bf16), w_gd.astype(bf16)


def kernel(x, p, w_in, rel_bias, w_gla_gate_up, b_gla_gate, gla_norm, w_branch_a, w_branch_b, w_out, norm_mix_pre, norm_mix_post, norm_ffn_pre, norm_ffn_post, ffn_w_gate, ffn_w_up, ffn_w_down, router_w, moe_w_gate, moe_w_up, moe_w_down, ple_w_proj, ple_w_gate, ple_norm):
    b, s, d = x.shape
    t = b * s
    depth = w_in.shape[0]
    x = x.reshape(t, d)
    vec = lambda a: a.reshape(1, -1).astype(f32)
    for i in range(depth):
        w_main, wq_t, wv_t, w_gd = _prep_w_in(w_in[i])
        g_pre = vec(norm_mix_pre[i])
        proj = _in_proj(x, g_pre, w_main)
        qt, vt, gd = _qv_proj(x, g_pre, wq_t, wv_t, w_gd, b, s)
        proj3 = proj.reshape(b, s, MAIN_WIDTH)
        ya = _attention(qt, proj3, vt, _attn_bias(rel_bias[i])).reshape(t, A_WIDTH)
        w_up = jnp.pad(w_gla_gate_up[i], ((0, LANES - GATE_RANK), (0, 0))).astype(bf16)
        yb = _gla(proj3, gd.reshape(b, s, LANES), w_up, vec(b_gla_gate[i]), vec(gla_norm[i])).reshape(t, B_V_WIDTH)
        j = i // 2
        routed = i % 2 == 1
        rw = jnp.pad(router_w[j], ((0, 0), (0, LANES - N_EXPERTS))).astype(bf16) if routed else None
        outs = _mix_out(ya, yb, proj, x, w_branch_a[i].astype(bf16), w_branch_b[i].astype(bf16),
                        w_out[i].astype(bf16), vec(norm_mix_post[i]), vec(norm_ffn_pre[i]), rw)
        p_i = p[i].reshape(t, PLE_DIM)
        tail = (vec(norm_ffn_post[i]), ple_w_proj[i].astype(bf16), ple_w_gate[i].astype(bf16), vec(ple_norm[i]))
        if routed:
            x1, h, route = outs
            x = _moe(h, route, x1, p_i, moe_w_gate[j].astype(bf16), moe_w_up[j].astype(bf16),
                     moe_w_down[j].astype(bf16), *tail)
        else:
            x1, h = outs
            x = _dense_ffn(h, x1, p_i, ffn_w_gate[j].astype(bf16), ffn_w_up[j].astype(bf16),
                           ffn_w_down[j].astype(bf16), *tail)
    return x.reshape(b, s, d)
```

```python
import functools

import numpy as np
import jax
import jax.numpy as jnp
from jax import lax
from jax.experimental import pallas as pl
from jax.experimental.pallas import tpu as pltpu

f32 = jnp.float32
bf16 = jnp.bfloat16

D_MODEL = 1024
CHUNK = 64
N_PREV_CHUNKS = 8
A_HEADS = 8
A_HEAD_DIM = 64
A_WIDTH = A_HEADS * A_HEAD_DIM
REL_CLIP = 128
B_HEADS = 4
B_KEY_DIM = 64
B_VAL_DIM = 128
B_K_WIDTH = B_HEADS * B_KEY_DIM
B_V_WIDTH = B_HEADS * B_VAL_DIM
GATE_RANK = 16
GATE_TEMP = 16.0
N_EXPERTS = 8
PLE_DIM = 256
NORM_EPS = 1e-6

LANES = 128
VMEM_LIMIT = 56 * 1024 * 1024

SRC_QA, SRC_KA, SRC_VA, SRC_QB, SRC_GD, SRC_R = 0, 512, 1024, 1536, 2560, 2576
SRC_END = 5136
COL_KA, COL_QB, COL_KB, COL_VB, COL_R, COL_GA, COL_GB = 0, 512, 768, 1024, 1536, 2048, 3072
MAIN_WIDTH = 4096
IN_TILE_N = 2048

QBLK = 2 * CHUNK
KBAND = (N_PREV_CHUNKS + 2) * CHUNK
KPAD = N_PREV_CHUNKS * CHUNK

SUBLANES = 8
ROW_TILE = D_MODEL // LANES
assert ROW_TILE == SUBLANES
MOE_TM = 512
ROW_BLK = 512
FF_CHUNK = 512
ISSUE_UNROLL = 8


def _params(*sem):
    return pltpu.CompilerParams(dimension_semantics=sem, vmem_limit_bytes=VMEM_LIMIT)


def _rms(x, g):
    return x * lax.rsqrt(jnp.mean(x * x, axis=-1, keepdims=True) + NORM_EPS) * g


def _sigmoid(x):
    return 1.0 / (1.0 + jnp.exp(-x))


def _silu(x):
    return x * _sigmoid(x)


def _chunks(n, c):
    return [(s, min(s + c, n)) for s in range(0, n, c)]


def _store_row_tiles(ref, x):
    rows = x.shape[0]
    for c in range(ROW_TILE):
        ref[pl.ds(c, rows, stride=ROW_TILE), :] = x[:, c * LANES:(c + 1) * LANES].astype(ref.dtype)


def _load_row_tiles(ref, rows):
    return jnp.concatenate([ref[pl.ds(c, rows, stride=ROW_TILE), :] for c in range(ROW_TILE)], axis=1)


def _tile_rows(ref, r):
    return ref.at[pl.ds(pl.multiple_of(r * ROW_TILE, ROW_TILE), ROW_TILE)]


def _const_spec(shape):
    nd = len(shape)
    return pl.BlockSpec(shape, lambda *_: (0,) * nd, pipeline_mode=pl.Buffered(1))


def _in_proj_kernel(x_ref, g_ref, w_ref, o_ref, h_ref):
    @pl.when(pl.program_id(1) == 0)
    def _():
        h_ref[...] = _rms(x_ref[...], g_ref[...]).astype(bf16)

    o_ref[...] = jnp.dot(h_ref[...], w_ref[...], preferred_element_type=f32).astype(o_ref.dtype)


def _in_proj(x, g, w, tm=1024):
    t = x.shape[0]
    n = w.shape[1]
    return pl.pallas_call(
        _in_proj_kernel,
        grid=(t // tm, n // IN_TILE_N),
        in_specs=[pl.BlockSpec((tm, D_MODEL), lambda i, j: (i, 0)),
                  pl.BlockSpec((1, D_MODEL), lambda i, j: (0, 0)),
                  pl.BlockSpec((D_MODEL, IN_TILE_N), lambda i, j: (0, j))],
        out_specs=pl.BlockSpec((tm, IN_TILE_N), lambda i, j: (i, j)),
        out_shape=jax.ShapeDtypeStruct((t, n), bf16),
        scratch_shapes=[pltpu.VMEM((tm, D_MODEL), bf16)],
        compiler_params=_params("parallel", "arbitrary"),
        name="in_proj",
    )(x, g, w)


def _qv_proj_kernel(x_ref, g_ref, wq_ref, wv_ref, wgd_ref, qt_ref, vt_ref, gd_ref):
    nt = (((1,), (1,)), ((), ()))
    h = _rms(x_ref[...], g_ref[...]).astype(bf16)
    qt_ref[0] = lax.dot_general(wq_ref[...], h, nt, preferred_element_type=f32).astype(qt_ref.dtype)
    vt = lax.dot_general(wv_ref[...], h, nt, preferred_element_type=f32).astype(vt_ref.dtype)
    for kt in range(vt_ref.shape[1]):
        vt_ref[0, kt] = vt[:, kt * LANES:(kt + 1) * LANES]
    gd_ref[...] = jnp.dot(h, wgd_ref[...], preferred_element_type=f32).astype(gd_ref.dtype)


def _qv_proj(x, g, wq_t, wv_t, w_gd, b, s, tm=1024):
    t = x.shape[0]
    per_b = s // tm
    return pl.pallas_call(
        _qv_proj_kernel,
        grid=(t // tm,),
        in_specs=[pl.BlockSpec((tm, D_MODEL), lambda i: (i, 0)),
                  _const_spec((1, D_MODEL)),
                  _const_spec((A_WIDTH, D_MODEL)),
                  _const_spec((A_WIDTH, D_MODEL)),
                  _const_spec((D_MODEL, LANES))],
        out_specs=[pl.BlockSpec((1, A_WIDTH, tm), lambda i: (i // per_b, 0, i % per_b)),
                   pl.BlockSpec((1, tm // LANES, A_WIDTH, LANES), lambda i: (i // per_b, i % per_b, 0, 0)),
                   pl.BlockSpec((tm, LANES), lambda i: (i, 0))],
        out_shape=[jax.ShapeDtypeStruct((b, A_WIDTH, s), bf16),
                   jax.ShapeDtypeStruct((b, s // LANES, A_WIDTH, LANES), bf16),
                   jax.ShapeDtypeStruct((t, LANES), bf16)],
        compiler_params=_params("parallel"),
        name="qv_proj",
    )(x, g, wq_t, wv_t, w_gd)


def _attn_kernel(qt_ref, k_ref, vt_ref, bias_ref, o_ref):
    i = pl.program_id(1)
    row = lax.broadcasted_iota(jnp.int32, (LANES, QBLK), 0)
    upper = row >= A_HEAD_DIM
    n_kt = KBAND // LANES
    kt_seq = [i + kt - KPAD // LANES for kt in range(n_kt)]
    kt_clamped = [jnp.maximum(j, 0) for j in kt_seq]
    for hp in range(A_WIDTH // LANES):
        pair = slice(hp * LANES, (hp + 1) * LANES)
        qt = qt_ref[0, pair, :] * (A_HEAD_DIM ** -0.5)
        kps = [k_ref[0, pl.ds(pl.multiple_of(j * LANES, LANES), LANES), pair] for j in kt_clamped]
        outs = []
        for s in range(2):
            h = 2 * hp + s
            qm = jnp.where(upper == bool(s), qt, jnp.zeros_like(qt))
            tiles = []
            for kt in range(n_kt):
                t = jnp.dot(kps[kt], qm, preferred_element_type=f32) + bias_ref[h, kt]
                tiles.append(jnp.where(kt_seq[kt] >= 0, t, -1e30))
            m = tiles[0]
            for t in tiles[1:]:
                m = jnp.maximum(m, t)
            m = jnp.max(m, axis=0, keepdims=True)
            l = None
            o = None
            for kt in range(n_kt):
                p = jnp.exp(tiles[kt] - m)
                ls = jnp.sum(p, axis=0, keepdims=True)
                vt = vt_ref[0, kt_clamped[kt], h * A_HEAD_DIM:(h + 1) * A_HEAD_DIM, :]
                part = jnp.dot(vt, p.astype(bf16), preferred_element_type=f32)
                l = ls if l is None else l + ls
                o = part if o is None else o + part
            outs.append(o / l)
        o_ref[0, :, pair] = jnp.concatenate(outs, axis=0).T.astype(o_ref.dtype)


def _attention(qt, proj3, vt, bias_t):
    b, s, _ = proj3.shape
    return pl.pallas_call(
        _attn_kernel,
        grid=(b, s // QBLK),
        in_specs=[pl.BlockSpec((1, A_WIDTH, QBLK), lambda bi, i: (bi, 0, i)),
                  pl.BlockSpec((1, s, A_WIDTH), lambda bi, i: (bi, 0, COL_KA // A_WIDTH)),
                  pl.BlockSpec((1, s // LANES, A_WIDTH, LANES), lambda bi, i: (bi, 0, 0, 0)),
                  _const_spec((A_HEADS, KBAND // LANES, LANES, QBLK))],
        out_specs=pl.BlockSpec((1, QBLK, A_WIDTH), lambda bi, i: (bi, i, 0)),
        out_shape=jax.ShapeDtypeStruct((b, s, A_WIDTH), bf16),
        compiler_params=_params("parallel", "arbitrary"),
        name="chunk_attention",
    )(qt, proj3, vt, bias_t)


def _attn_bias(rel_bias):
    h = rel_bias.shape[0]
    r = np.arange(QBLK)[:, None]
    k = np.arange(KBAND)[None, :]
    visible = np.where(r < CHUNK, k < KBAND - CHUNK, k >= CHUNK)
    span = QBLK + KBAND - 1
    n_far = span - 2 * REL_CLIP
    line = jnp.concatenate([jnp.broadcast_to(rel_bias[:, -1:], (h, n_far)), rel_bias[:, :0:-1]], axis=1)
    line = jnp.pad(line.astype(f32), ((0, 0), (0, 1)))
    skew = jnp.tile(line, (1, QBLK))[:, :QBLK * span].reshape(h, QBLK, span)
    bias = jnp.where(visible[None], skew[:, :, QBLK - 1:], -1e30)
    return jnp.swapaxes(bias, 1, 2).reshape(h, KBAND // LANES, LANES, QBLK)


def _gla_kernel(q_ref, k_ref, v_ref, gd_ref, r_ref, wup_ref, b_ref, gn_ref, o_ref, state):
    @pl.when(pl.program_id(1) == 0)
    def _():
        state[...] = jnp.zeros_like(state)

    tq = q_ref.shape[1]
    tn = (((0,), (0,)), ((), ()))
    nt = (((1,), (1,)), ((), ()))
    row = lax.broadcasted_iota(jnp.int32, (tq, tq), 0)
    col = lax.broadcasted_iota(jnp.int32, (tq, tq), 1)
    later = jnp.logical_and(col > row, col // CHUNK == row // CHUNK)
    later = jnp.where(later, 1.0, 0.0).astype(bf16)
    z = jnp.dot(gd_ref[0], wup_ref[...], preferred_element_type=f32) + b_ref[...]
    log_a = (jnp.minimum(z, 0.0) - jnp.log(1.0 + jnp.exp(-jnp.abs(z)))) / GATE_TEMP
    la_hi = log_a.astype(bf16)
    la_lo = (log_a - la_hi.astype(f32)).astype(bf16)
    rev = (jnp.dot(later, la_hi, preferred_element_type=f32)
           + jnp.dot(later, la_lo, preferred_element_type=f32))
    kdec = k_ref[0].astype(f32) * jnp.exp(rev)
    lane = lax.broadcasted_iota(jnp.int32, (CHUNK, LANES), 1)
    upper = lane >= B_KEY_DIM
    st = [state[pr] for pr in range(B_HEADS // 2)]
    for c in range(tq // CHUNK):
        rows = slice(c * CHUNK, (c + 1) * CHUNK)
        decay = jnp.exp(jnp.sum(log_a[rows], axis=0, keepdims=True))
        for pr in range(B_HEADS // 2):
            cols = slice(pr * LANES, (pr + 1) * LANES)
            kd = kdec[rows, cols]
            d_state = None
            for s in range(2):
                h = 2 * pr + s
                kdm = jnp.where(upper == bool(s), kd, 0.0).astype(bf16)
                vh = v_ref[0, rows, h * B_VAL_DIM:(h + 1) * B_VAL_DIM]
                part = lax.dot_general(vh, kdm, tn, preferred_element_type=f32)
                d_state = part if d_state is None else d_state + part
            st[pr] = decay[:, cols] * st[pr] + d_state
            stb = st[pr].astype(bf16)
            qp = q_ref[0, rows, cols] * (B_KEY_DIM ** -0.5)
            for s in range(2):
                h = 2 * pr + s
                vcols = slice(h * B_VAL_DIM, (h + 1) * B_VAL_DIM)
                qm = jnp.where(upper == bool(s), qp, jnp.zeros_like(qp))
                o = lax.dot_general(qm, stb, nt, preferred_element_type=f32)
                o = _rms(o, gn_ref[:, vcols])
                r = r_ref[0, rows, vcols].astype(f32)
                o_ref[0, rows, vcols] = (o * _silu(r)).astype(o_ref.dtype)
    for pr in range(B_HEADS // 2):
        state[pr] = st[pr]


def _gla(proj3, gd3, w_up, b_gate, gla_norm, tq=256):
    b, s, _ = proj3.shape
    return pl.pallas_call(
        _gla_kernel,
        grid=(b, s // tq),
        in_specs=[pl.BlockSpec((1, tq, B_K_WIDTH), lambda bi, i: (bi, i, COL_QB // B_K_WIDTH)),
                  pl.BlockSpec((1, tq, B_K_WIDTH), lambda bi, i: (bi, i, COL_KB // B_K_WIDTH)),
                  pl.BlockSpec((1, tq, B_V_WIDTH), lambda bi, i: (bi, i, COL_VB // B_V_WIDTH)),
                  pl.BlockSpec((1, tq, LANES), lambda bi, i: (bi, i, 0)),
                  pl.BlockSpec((1, tq, B_V_WIDTH), lambda bi, i: (bi, i, COL_R // B_V_WIDTH)),
                  _const_spec((LANES, B_K_WIDTH)),
                  _const_spec((1, B_K_WIDTH)),
                  _const_spec((1, B_V_WIDTH))],
        out_specs=pl.BlockSpec((1, tq, B_V_WIDTH), lambda bi, i: (bi, i, 0)),
        out_shape=jax.ShapeDtypeStruct((b, s, B_V_WIDTH), bf16),
        scratch_shapes=[pltpu.VMEM((B_HEADS // 2, LANES, B_VAL_DIM), f32)],
        compiler_params=_params("parallel", "arbitrary"),
        name="gla",
    )(proj3, proj3, proj3, gd3, proj3, w_up, b_gate, gla_norm)


def _route_top2(h, rw_ref):
    logits = jnp.dot(h.astype(bf16), rw_ref[...], preferred_element_type=f32)
    lane = lax.broadcasted_iota(jnp.int32, logits.shape, 1).astype(f32)
    neg = -jnp.inf
    lg = jnp.where(lane < N_EXPERTS, logits, neg)
    m1 = jnp.max(lg, axis=-1, keepdims=True)
    i1 = jnp.min(jnp.where(lg == m1, lane, float(LANES)), axis=-1, keepdims=True)
    lg2 = jnp.where(lane == i1, neg, lg)
    m2 = jnp.max(lg2, axis=-1, keepdims=True)
    i2 = jnp.min(jnp.where(lg2 == m2, lane, float(LANES)), axis=-1, keepdims=True)
    e2 = jnp.exp(m2 - m1)
    w1 = 1.0 / (1.0 + e2)
    w2 = e2 / (1.0 + e2)
    return jnp.where(lane == 0, i1, jnp.where(lane == 1, i2, jnp.where(lane == 2, w1, jnp.where(lane == 3, w2, 0.0))))


def _mix_out_kernel(*refs, routed):
    if routed:
        (ya_ref, yb_ref, ga_ref, gb_ref, x_ref, wa_ref, wb_ref, wo_ref, gpost_ref, gpre_ref, rw_ref,
         xo_ref, ho_ref, route_ref) = refs
    else:
        (ya_ref, yb_ref, ga_ref, gb_ref, x_ref, wa_ref, wb_ref, wo_ref, gpost_ref, gpre_ref,
         xo_ref, ho_ref) = refs
    a = jnp.dot(ya_ref[...], wa_ref[...], preferred_element_type=f32)
    b = jnp.dot(yb_ref[...], wb_ref[...], preferred_element_type=f32)
    merged = _sigmoid(ga_ref[...].astype(f32)) * a + _sigmoid(gb_ref[...].astype(f32)) * b
    y = jnp.dot(merged.astype(bf16), wo_ref[...], preferred_element_type=f32)
    x1 = x_ref[...] + _rms(y, gpost_ref[...])
    h = _rms(x1, gpre_ref[...])
    xo_ref[...] = x1
    if routed:
        _store_row_tiles(ho_ref, h)
        route_ref[...] = _route_top2(h, rw_ref)
    else:
        ho_ref[...] = h.astype(ho_ref.dtype)


def _mix_out(ya, yb, proj, x, wa, wb, wo, g_post, g_pre, router_w=None, tm=512):
    t = x.shape[0]
    routed = router_w is not None
    row = lambda i: (i, 0)
    in_specs = [pl.BlockSpec((tm, A_WIDTH), row),
                pl.BlockSpec((tm, B_V_WIDTH), row),
                pl.BlockSpec((tm, D_MODEL), lambda i: (i, COL_GA // D_MODEL)),
                pl.BlockSpec((tm, D_MODEL), lambda i: (i, COL_GB // D_MODEL)),
                pl.BlockSpec((tm, D_MODEL), row),
                _const_spec((A_WIDTH, D_MODEL)),
                _const_spec((B_V_WIDTH, D_MODEL)),
                _const_spec((D_MODEL, D_MODEL)),
                _const_spec((1, D_MODEL)),
                _const_spec((1, D_MODEL))]
    args = [ya, yb, proj, proj, x, wa, wb, wo, g_post, g_pre]
    if routed:
        out_specs = [pl.BlockSpec((tm, D_MODEL), row), pl.BlockSpec((tm * ROW_TILE, LANES), row)]
        out_shape = [jax.ShapeDtypeStruct((t, D_MODEL), f32), jax.ShapeDtypeStruct((t * ROW_TILE, LANES), f32)]
    else:
        out_specs = [pl.BlockSpec((tm, D_MODEL), row), pl.BlockSpec((tm, D_MODEL), row)]
        out_shape = [jax.ShapeDtypeStruct((t, D_MODEL), f32), jax.ShapeDtypeStruct((t, D_MODEL), bf16)]
    if routed:
        in_specs.append(_const_spec((D_MODEL, LANES)))
        args.append(router_w)
        out_specs.append(pl.BlockSpec((tm, LANES), row))
        out_shape.append(jax.ShapeDtypeStruct((t, LANES), f32))
    return pl.pallas_call(
        functools.partial(_mix_out_kernel, routed=routed),
        grid=(t // tm,),
        in_specs=in_specs,
        out_specs=out_specs,
        out_shape=out_shape,
        compiler_params=_params("parallel"),
        name="mix_out",
    )(*args)


def _post_ffn(y, x1, p, gpost_ref, wpp_ref, wpg_ref, gple_ref):
    x2 = x1 + _rms(y, gpost_ref[...])
    e = jnp.dot(p.astype(bf16), wpp_ref[...], preferred_element_type=f32)
    e = e * _sigmoid(jnp.dot(x2.astype(bf16), wpg_ref[...], preferred_element_type=f32))
    return x2 + _rms(e, gple_ref[...])


def _swiglu_chunks(x, wg_ref, wu_ref, wd_ref, lead, width):
    acc = None
    for c0, c1 in _chunks(width, FF_CHUNK):
        g = jnp.dot(x, wg_ref[lead + (slice(None), slice(c0, c1))], preferred_element_type=f32)
        u = jnp.dot(x, wu_ref[lead + (slice(None), slice(c0, c1))], preferred_element_type=f32)
        act = (_silu(g) * u).astype(bf16)
        part = jnp.dot(act, wd_ref[lead + (slice(c0, c1), slice(None))], preferred_element_type=f32)
        acc = part if acc is None else acc + part
    return acc


def _dense_ffn_kernel(h_ref, x_ref, p_ref, wg_ref, wu_ref, wd_ref, gpost_ref, wpp_ref, wpg_ref, gple_ref, o_ref):
    y = _swiglu_chunks(h_ref[...], wg_ref, wu_ref, wd_ref, (), wg_ref.shape[1])
    o_ref[...] = _post_ffn(y, x_ref[...], p_ref[...], gpost_ref, wpp_ref, wpg_ref, gple_ref)


def _dense_ffn(h, x1, p, wg, wu, wd, g_post, wpp, wpg, g_ple, tm=512):
    t = x1.shape[0]
    ff = wg.shape[1]
    row = lambda i: (i, 0)
    return pl.pallas_call(
        _dense_ffn_kernel,
        grid=(t // tm,),
        in_specs=[pl.BlockSpec((tm, D_MODEL), row),
                  pl.BlockSpec((tm, D_MODEL), row),
                  pl.BlockSpec((tm, PLE_DIM), row),
                  _const_spec((D_MODEL, ff)),
                  _const_spec((D_MODEL, ff)),
                  _const_spec((ff, D_MODEL)),
                  _const_spec((1, D_MODEL)),
                  _const_spec((PLE_DIM, D_MODEL)),
                  _const_spec((D_MODEL, D_MODEL)),
                  _const_spec((1, D_MODEL))],
        out_specs=pl.BlockSpec((tm, D_MODEL), row),
        out_shape=jax.ShapeDtypeStruct((t, D_MODEL), f32),
        compiler_params=_params("parallel"),
        name="dense_ffn",
    )(h, x1, p, wg, wu, wd, g_post, wpp, wpg, g_ple)


def _scatter_kernel(fill_ref, pos_ref, h_ref, xs_out, zbuf, sem, zsem):
    rows = h_ref.shape[0] // ROW_TILE

    @pl.when(pl.program_id(0) == 0)
    def _():
        zbuf[...] = jnp.zeros_like(zbuf)

        def zero_rows(first_row):
            start = pl.multiple_of(first_row * ROW_TILE, ROW_TILE)
            fill = pltpu.make_async_copy(zbuf, xs_out.at[pl.ds(start, MOE_TM * ROW_TILE)], zsem)
            fill.start()
            fill.wait()

        for e in range(N_EXPERTS):
            zero_rows(fill_ref[e])

        def zero_tile(tile, c):
            zero_rows(tile * MOE_TM)
            return c

        lax.fori_loop(fill_ref[N_EXPERTS], xs_out.shape[0] // (MOE_TM * ROW_TILE), zero_tile, 0)

    def issue(r, c):
        for k in range(2):
            pltpu.make_async_copy(_tile_rows(h_ref, r), _tile_rows(xs_out, pos_ref[2 * r + k]),
                                  sem.at[k]).start(priority=k)
        return c

    lax.fori_loop(0, rows, issue, 0, unroll=ISSUE_UNROLL)
    for k in range(2):
        pltpu.make_async_copy(h_ref, xs_out.at[pl.ds(0, rows * ROW_TILE)], sem.at[k]).wait()


def _scatter_rows(fill_plan, pos, h, n_rows):
    t = h.shape[0] // ROW_TILE
    return pl.pallas_call(
        _scatter_kernel,
        grid_spec=pltpu.PrefetchScalarGridSpec(
            num_scalar_prefetch=1,
            grid=(t // ROW_BLK,),
            in_specs=[pl.BlockSpec((2 * ROW_BLK,), lambda i, fill: (i,), memory_space=pltpu.SMEM),
                      pl.BlockSpec((ROW_BLK * ROW_TILE, LANES), lambda i, fill: (i, 0))],
            out_specs=pl.BlockSpec(memory_space=pl.ANY),
            scratch_shapes=[pltpu.VMEM((MOE_TM * ROW_TILE, LANES), f32),
                            pltpu.SemaphoreType.DMA((2,)),
                            pltpu.SemaphoreType.DMA(())]),
        out_shape=jax.ShapeDtypeStruct((n_rows * ROW_TILE, LANES), f32),
        compiler_params=_params("arbitrary"),
        name="moe_scatter",
    )(fill_plan, pos, h)


def _expert_kernel(te_ref, nu_ref, x_ref, wg_ref, wu_ref, wd_ref, o_ref, acc_ref):
    del te_ref
    i = pl.program_id(0)
    f = pl.program_id(1)

    @pl.when(i < nu_ref[0])
    def _():
        x = _load_row_tiles(x_ref, MOE_TM).astype(bf16)
        y = _swiglu_chunks(x, wg_ref, wu_ref, wd_ref, (0,), wg_ref.shape[2])

        @pl.when(f == 0)
        def _():
            acc_ref[...] = y

        @pl.when(f > 0)
        def _():
            acc_ref[...] += y

    last = f == pl.num_programs(1) - 1

    @pl.when(jnp.logical_and(last, i < nu_ref[0]))
    def _():
        _store_row_tiles(o_ref, acc_ref[...])

    @pl.when(jnp.logical_and(last, i >= nu_ref[0]))
    def _():
        o_ref[...] = jnp.zeros_like(o_ref)


def _expert_ffn(tile_expert, n_used, xs, wg, wu, wd, tf=1792):
    n_rows = xs.shape[0] // ROW_TILE
    ff = wg.shape[2]
    nf = ff // tf

    def f_idx(i, f, nu):
        return jnp.where(i < nu[0], f, nf - 1)

    return pl.pallas_call(
        _expert_kernel,
        grid_spec=pltpu.PrefetchScalarGridSpec(
            num_scalar_prefetch=2,
            grid=(n_rows // MOE_TM, nf),
            in_specs=[pl.BlockSpec((MOE_TM * ROW_TILE, LANES), lambda i, f, te, nu: (i, 0)),
                      pl.BlockSpec((1, D_MODEL, tf), lambda i, f, te, nu: (te[i], 0, f_idx(i, f, nu))),
                      pl.BlockSpec((1, D_MODEL, tf), lambda i, f, te, nu: (te[i], 0, f_idx(i, f, nu))),
                      pl.BlockSpec((1, tf, D_MODEL), lambda i, f, te, nu: (te[i], f_idx(i, f, nu), 0))],
            out_specs=pl.BlockSpec((MOE_TM * ROW_TILE, LANES), lambda i, f, te, nu: (i, 0)),
            scratch_shapes=[pltpu.VMEM((MOE_TM, D_MODEL), f32)]),
        out_shape=jax.ShapeDtypeStruct((n_rows * ROW_TILE, LANES), f32),
        compiler_params=_params("arbitrary", "arbitrary"),
        name="moe_experts",
    )(tile_expert, n_used, xs, wg, wu, wd)


def _combine_kernel(pos_ref, ys_hbm, route_ref, x_ref, p_ref, gpost_ref, wpp_ref, wpg_ref, gple_ref,
                    o_ref, buf, sem):
    rows = x_ref.shape[0]

    def issue(r, c):
        for k in range(2):
            pltpu.make_async_copy(_tile_rows(ys_hbm, pos_ref[2 * r + k]), _tile_rows(buf.at[k], r),
                                  sem.at[k]).start(priority=k)
        return c

    lax.fori_loop(0, rows, issue, 0, unroll=ISSUE_UNROLL)
    for k in range(2):
        pltpu.make_async_copy(ys_hbm.at[pl.ds(0, rows * ROW_TILE)], buf.at[k], sem.at[k]).wait()
    route = route_ref[...]
    y = route[:, 2:3] * _load_row_tiles(buf.at[0], rows) + route[:, 3:4] * _load_row_tiles(buf.at[1], rows)
    o_ref[...] = _post_ffn(y, x_ref[...], p_ref[...], gpost_ref, wpp_ref, wpg_ref, gple_ref)


def _combine(pos, ys, route, x1, p, g_post, wpp, wpg, g_ple):
    t = x1.shape[0]
    row = lambda i: (i, 0)
    return pl.pallas_call(
        _combine_kernel,
        grid=(t // ROW_BLK,),
        in_specs=[pl.BlockSpec((2 * ROW_BLK,), lambda i: (i,), memory_space=pltpu.SMEM),
                  pl.BlockSpec(memory_space=pl.ANY),
                  pl.BlockSpec((ROW_BLK, LANES), row),
                  pl.BlockSpec((ROW_BLK, D_MODEL), row),
                  pl.BlockSpec((ROW_BLK, PLE_DIM), row),
                  _const_spec((1, D_MODEL)),
                  _const_spec((PLE_DIM, D_MODEL)),
                  _const_spec((D_MODEL, D_MODEL)),
                  _const_spec((1, D_MODEL))],
        out_specs=pl.BlockSpec((ROW_BLK, D_MODEL), row),
        out_shape=jax.ShapeDtypeStruct((t, D_MODEL), f32),
        scratch_shapes=[pltpu.VMEM((2, ROW_BLK * ROW_TILE, LANES), f32),
                        pltpu.SemaphoreType.DMA((2,))],
        compiler_params=_params("arbitrary"),
        name="moe_combine",
    )(pos, ys, route, x1, p, g_post, wpp, wpg, g_ple)


def _route_slots(route, n_tiles):
    ids = route[:, 0:2].astype(jnp.int32).reshape(-1)
    onehot = (ids[None, :] == jnp.arange(N_EXPERTS, dtype=jnp.int32)[:, None]).astype(jnp.int32)
    csum = jnp.cumsum(onehot, axis=1)
    rank = jnp.sum(onehot * csum, axis=0) - 1
    counts = csum[:, -1]
    padded = ((counts + MOE_TM - 1) // MOE_TM) * MOE_TM
    ends = jnp.cumsum(padded)
    pos = (ends - padded)[ids] + rank
    tile_start = jnp.arange(n_tiles, dtype=jnp.int32) * MOE_TM
    tile_expert = jnp.minimum(jnp.sum((tile_start[:, None] >= ends[None, :]).astype(jnp.int32), axis=1),
                              N_EXPERTS - 1)
    n_used = (ends[-1] // MOE_TM).reshape(1)
    fill_plan = jnp.concatenate([ends - padded + counts, n_used])
    return (pos.astype(jnp.int32), tile_expert.astype(jnp.int32), n_used.astype(jnp.int32),
            fill_plan.astype(jnp.int32))


def _moe(h, route, x1, p, wg, wu, wd, g_post, wpp, wpg, g_ple):
    t = x1.shape[0]
    n_tiles = 2 * t // MOE_TM + N_EXPERTS
    pos, tile_expert, n_used, fill_plan = _route_slots(route, n_tiles)
    xs = _scatter_rows(fill_plan, pos, h, n_tiles * MOE_TM)
    ys = _expert_ffn(tile_expert, n_used, xs, wg, wu, wd)
    return _combine(pos, ys, route, x1, p, g_post, wpp, wpg, g_ple)


def _prep_w_in(w):
    main = jnp.concatenate([w[:, SRC_KA:SRC_VA], w[:, SRC_QB:SRC_GD], w[:, SRC_R:SRC_END]], axis=1)
    wq_t = w[:, SRC_QA:SRC_KA].T
    wv_t = w[:, SRC_VA:SRC_QB].T
    w_gd = jnp.pad(w[:, SRC_GD:SRC_R], ((0, 0), (0, LANES - GATE_RANK)))
    return main.astype(bf16), wq_t.astype(bf16), wv_t.astype(bf16), w_gd.astype(bf16)


def kernel(x, p, w_in, rel_bias, w_gla_gate_up, b_gla_gate, gla_norm, w_branch_a, w_branch_b, w_out, norm_mix_pre, norm_mix_post, norm_ffn_pre, norm_ffn_post, ffn_w_gate, ffn_w_up, ffn_w_down, router_w, moe_w_gate, moe_w_up, moe_w_down, ple_w_proj, ple_w_gate, ple_norm):
    b, s, d = x.shape
    t = b * s
    depth = w_in.shape[0]
    x = x.reshape(t, d)
    vec = lambda a: a.reshape(1, -1).astype(f32)
    for i in range(depth):
        w_main, wq_t, wv_t, w_gd = _prep_w_in(w_in[i])
        g_pre = vec(norm_mix_pre[i])
        proj = _in_proj(x, g_pre, w_main)
        qt, vt, gd = _qv_proj(x, g_pre, wq_t, wv_t, w_gd, b, s)
        proj3 = proj.reshape(b, s, MAIN_WIDTH)
        ya = _attention(qt, proj3, vt, _attn_bias(rel_bias[i])).reshape(t, A_WIDTH)
        w_up = jnp.pad(w_gla_gate_up[i], ((0, LANES - GATE_RANK), (0, 0))).astype(bf16)
        yb = _gla(proj3, gd.reshape(b, s, LANES), w_up, vec(b_gla_gate[i]), vec(gla_norm[i])).reshape(t, B_V_WIDTH)
        j = i // 2
        routed = i % 2 == 1
        rw = jnp.pad(router_w[j], ((0, 0), (0, LANES - N_EXPERTS))).astype(bf16) if routed else None
        outs = _mix_out(ya, yb, proj, x, w_branch_a[i].astype(bf16), w_branch_b[i].astype(bf16),
                        w_out[i].astype(bf16), vec(norm_mix_post[i]), vec(norm_ffn_pre[i]), rw)
        p_i = p[i].reshape(t, PLE_DIM)
        tail = (vec(norm_ffn_post[i]), ple_w_proj[i].astype(bf16), ple_w_gate[i].astype(bf16), vec(ple_norm[i]))
        if routed:
            x1, h, route = outs
            x = _moe(h, route, x1, p_i, moe_w_gate[j].astype(bf16), moe_w_up[j].astype(bf16),
                     moe_w_down[j].astype(bf16), *tail)
        else:
            x1, h = outs
            x = _dense_ffn(h, x1, p_i, ffn_w_gate[j].astype(bf16), ffn_w_up[j].astype(bf16),
                           ffn_w_down[j].astype(bf16), *tail)
    return x.reshape(b, s, d)
```

```python
import functools

import numpy as np
import jax
import jax.numpy as jnp
from jax import lax
from jax.experimental import pallas as pl
from jax.experimental.pallas import tpu as pltpu

f32 = jnp.float32
bf16 = jnp.bfloat16

D_MODEL = 1024
CHUNK = 64
N_PREV_CHUNKS = 8
A_HEADS = 8
A_HEAD_DIM = 64
A_WIDTH = A_HEADS * A_HEAD_DIM
REL_CLIP = 128
B_HEADS = 4
B_KEY_DIM = 64
B_VAL_DIM = 128
B_K_WIDTH = B_HEADS * B_KEY_DIM
B_V_WIDTH = B_HEADS * B_VAL_DIM
GATE_RANK = 16
GATE_TEMP = 16.0
N_EXPERTS = 8
PLE_DIM = 256
NORM_EPS = 1e-6

LANES = 128
VMEM_LIMIT = 56 * 1024 * 1024

SRC_QA, SRC_KA, SRC_VA, SRC_QB, SRC_GD, SRC_R = 0, 512, 1024, 1536, 2560, 2576
SRC_END = 5136
COL_KA, COL_QB, COL_KB, COL_VB, COL_R, COL_GA, COL_GB = 0, 512, 768, 1024, 1536, 2048, 3072
MAIN_WIDTH = 4096
IN_TILE_N = 2048

QBLK = 2 * CHUNK
KBAND = (N_PREV_CHUNKS + 2) * CHUNK
KPAD = N_PREV_CHUNKS * CHUNK

SUBLANES = 8
ROW_TILE = D_MODEL // LANES
assert ROW_TILE == SUBLANES
MOE_TM = 512
ROW_BLK = 512
FF_CHUNK = 512
ISSUE_UNROLL = 8
GLA_NB = 2


def _params(*sem):
    return pltpu.CompilerParams(dimension_semantics=sem, vmem_limit_bytes=VMEM_LIMIT)


def _rms(x, g):
    return x * lax.rsqrt(jnp.mean(x * x, axis=-1, keepdims=True) + NORM_EPS) * g


def _sigmoid(x):
    return 1.0 / (1.0 + jnp.exp(-x))


def _silu(x):
    return x * _sigmoid(x)


def _chunks(n, c):
    return [(s, min(s + c, n)) for s in range(0, n, c)]


def _store_row_tiles(ref, x):
    rows = x.shape[0]
    for c in range(ROW_TILE):
        ref[pl.ds(c, rows, stride=ROW_TILE), :] = x[:, c * LANES:(c + 1) * LANES].astype(ref.dtype)


def _load_row_tiles(ref, rows):
    return jnp.concatenate([ref[pl.ds(c, rows, stride=ROW_TILE), :] for c in range(ROW_TILE)], axis=1)


def _tile_rows(ref, r):
    return ref.at[pl.ds(pl.multiple_of(r * ROW_TILE, ROW_TILE), ROW_TILE)]


def _const_spec(shape):
    nd = len(shape)
    return pl.BlockSpec(shape, lambda *_: (0,) * nd, pipeline_mode=pl.Buffered(1))


def _in_proj_kernel(x_ref, g_ref, w_ref, o_ref, h_ref):
    @pl.when(pl.program_id(1) == 0)
    def _():
        h_ref[...] = _rms(x_ref[...], g_ref[...]).astype(bf16)

    o_ref[...] = jnp.dot(h_ref[...], w_ref[...], preferred_element_type=f32).astype(o_ref.dtype)


def _in_proj(x, g, w, tm=1024):
    t = x.shape[0]
    n = w.shape[1]
    return pl.pallas_call(
        _in_proj_kernel,
        grid=(t // tm, n // IN_TILE_N),
        in_specs=[pl.BlockSpec((tm, D_MODEL), lambda i, j: (i, 0)),
                  pl.BlockSpec((1, D_MODEL), lambda i, j: (0, 0)),
                  pl.BlockSpec((D_MODEL, IN_TILE_N), lambda i, j: (0, j))],
        out_specs=pl.BlockSpec((tm, IN_TILE_N), lambda i, j: (i, j)),
        out_shape=jax.ShapeDtypeStruct((t, n), bf16),
        scratch_shapes=[pltpu.VMEM((tm, D_MODEL), bf16)],
        compiler_params=_params("parallel", "arbitrary"),
        name="in_proj",
    )(x, g, w)


def _qv_proj_kernel(x_ref, g_ref, wq_ref, wv_ref, wgd_ref, qt_ref, vt_ref, gd_ref):
    nt = (((1,), (1,)), ((), ()))
    h = _rms(x_ref[...], g_ref[...]).astype(bf16)
    qt_ref[0] = lax.dot_general(wq_ref[...], h, nt, preferred_element_type=f32).astype(qt_ref.dtype)
    vt = lax.dot_general(wv_ref[...], h, nt, preferred_element_type=f32).astype(vt_ref.dtype)
    for kt in range(vt_ref.shape[1]):
        vt_ref[0, kt] = vt[:, kt * LANES:(kt + 1) * LANES]
    gd_ref[...] = jnp.dot(h, wgd_ref[...], preferred_element_type=f32).astype(gd_ref.dtype)


def _qv_proj(x, g, wq_t, wv_t, w_gd, b, s, tm=1024):
    t = x.shape[0]
    per_b = s // tm
    return pl.pallas_call(
        _qv_proj_kernel,
        grid=(t // tm,),
        in_specs=[pl.BlockSpec((tm, D_MODEL), lambda i: (i, 0)),
                  _const_spec((1, D_MODEL)),
                  _const_spec((A_WIDTH, D_MODEL)),
                  _const_spec((A_WIDTH, D_MODEL)),
                  _const_spec((D_MODEL, LANES))],
        out_specs=[pl.BlockSpec((1, A_WIDTH, tm), lambda i: (i // per_b, 0, i % per_b)),
                   pl.BlockSpec((1, tm // LANES, A_WIDTH, LANES), lambda i: (i // per_b, i % per_b, 0, 0)),
                   pl.BlockSpec((tm, LANES), lambda i: (i, 0))],
        out_shape=[jax.ShapeDtypeStruct((b, A_WIDTH, s), bf16),
                   jax.ShapeDtypeStruct((b, s // LANES, A_WIDTH, LANES), bf16),
                   jax.ShapeDtypeStruct((t, LANES), bf16)],
        compiler_params=_params("parallel"),
        name="qv_proj",
    )(x, g, wq_t, wv_t, w_gd)


def _attn_kernel(qt_ref, k_ref, vt_ref, bias_ref, o_ref):
    i = pl.program_id(1)
    row = lax.broadcasted_iota(jnp.int32, (LANES, QBLK), 0)
    upper = row >= A_HEAD_DIM
    n_kt = KBAND // LANES
    kt_seq = [i + kt - KPAD // LANES for kt in range(n_kt)]
    kt_clamped = [jnp.maximum(j, 0) for j in kt_seq]
    for hp in range(A_WIDTH // LANES):
        pair = slice(hp * LANES, (hp + 1) * LANES)
        qt = qt_ref[0, pair, :] * (A_HEAD_DIM ** -0.5)
        kps = [k_ref[0, pl.ds(pl.multiple_of(j * LANES, LANES), LANES), pair] for j in kt_clamped]
        outs = []
        for s in range(2):
            h = 2 * hp + s
            qm = jnp.where(upper == bool(s), qt, jnp.zeros_like(qt))
            tiles = []
            for kt in range(n_kt):
                t = jnp.dot(kps[kt], qm, preferred_element_type=f32) + bias_ref[h, kt]
                tiles.append(jnp.where(kt_seq[kt] >= 0, t, -1e30))
            m = tiles[0]
            for t in tiles[1:]:
                m = jnp.maximum(m, t)
            m = jnp.max(m, axis=0, keepdims=True)
            l = None
            o = None
            for kt in range(n_kt):
                p = jnp.exp(tiles[kt] - m)
                ls = jnp.sum(p, axis=0, keepdims=True)
                vt = vt_ref[0, kt_clamped[kt], h * A_HEAD_DIM:(h + 1) * A_HEAD_DIM, :]
                part = jnp.dot(vt, p.astype(bf16), preferred_element_type=f32)
                l = ls if l is None else l + ls
                o = part if o is None else o + part
            outs.append(o / l)
        o_ref[0, :, pair] = jnp.concatenate(outs, axis=0).T.astype(o_ref.dtype)


def _attention(qt, proj3, vt, bias_t):
    b, s, _ = proj3.shape
    return pl.pallas_call(
        _attn_kernel,
        grid=(b, s // QBLK),
        in_specs=[pl.BlockSpec((1, A_WIDTH, QBLK), lambda bi, i: (bi, 0, i)),
                  pl.BlockSpec((1, s, A_WIDTH), lambda bi, i: (bi, 0, COL_KA // A_WIDTH)),
                  pl.BlockSpec((1, s // LANES, A_WIDTH, LANES), lambda bi, i: (bi, 0, 0, 0)),
                  _const_spec((A_HEADS, KBAND // LANES, LANES, QBLK))],
        out_specs=pl.BlockSpec((1, QBLK, A_WIDTH), lambda bi, i: (bi, i, 0)),
        out_shape=jax.ShapeDtypeStruct((b, s, A_WIDTH), bf16),
        compiler_params=_params("parallel", "arbitrary"),
        name="chunk_attention",
    )(qt, proj3, vt, bias_t)


def _attn_bias(rel_bias):
    h = rel_bias.shape[0]
    r = np.arange(QBLK)[:, None]
    k = np.arange(KBAND)[None, :]
    visible = np.where(r < CHUNK, k < KBAND - CHUNK, k >= CHUNK)
    span = QBLK + KBAND - 1
    n_far = span - 2 * REL_CLIP
    line = jnp.concatenate([jnp.broadcast_to(rel_bias[:, -1:], (h, n_far)), rel_bias[:, :0:-1]], axis=1)
    line = jnp.pad(line.astype(f32), ((0, 0), (0, 1)))
    skew = jnp.tile(line, (1, QBLK))[:, :QBLK * span].reshape(h, QBLK, span)
    bias = jnp.where(visible[None], skew[:, :, QBLK - 1:], -1e30)
    return jnp.swapaxes(bias, 1, 2).reshape(h, KBAND // LANES, LANES, QBLK)


def _gla_kernel(q_ref, k_ref, v_ref, gd_ref, r_ref, wup_ref, b_ref, gn_ref, o_ref, state):
    @pl.when(pl.program_id(1) == 0)
    def _():
        state[...] = jnp.zeros_like(state)

    nb, tq = q_ref.shape[0], q_ref.shape[1]
    tn = (((0,), (0,)), ((), ()))
    nt = (((1,), (1,)), ((), ()))
    row = lax.broadcasted_iota(jnp.int32, (tq, tq), 0)
    col = lax.broadcasted_iota(jnp.int32, (tq, tq), 1)
    later = jnp.logical_and(col > row, col // CHUNK == row // CHUNK)
    later = jnp.where(later, 1.0, 0.0).astype(bf16)
    log_a, kdec = [], []
    for bb in range(nb):
        z = jnp.dot(gd_ref[bb], wup_ref[...], preferred_element_type=f32) + b_ref[...]
        la = (jnp.minimum(z, 0.0) - jnp.log(1.0 + jnp.exp(-jnp.abs(z)))) / GATE_TEMP
        la_hi = la.astype(bf16)
        la_lo = (la - la_hi.astype(f32)).astype(bf16)
        rev = (jnp.dot(later, la_hi, preferred_element_type=f32)
               + jnp.dot(later, la_lo, preferred_element_type=f32))
        log_a.append(la)
        kdec.append((k_ref[bb].astype(f32) * jnp.exp(rev)).astype(bf16))
    hrow = lax.broadcasted_iota(jnp.int32, (B_HEADS * CHUNK, B_K_WIDTH), 0) // CHUNK
    hlane = lax.broadcasted_iota(jnp.int32, (B_HEADS * CHUNK, B_K_WIDTH), 1) // B_KEY_DIM
    own = hrow == hlane
    st = [state[bb] for bb in range(nb)]
    for c in range(tq // CHUNK):
        rows = slice(c * CHUNK, (c + 1) * CHUNK)
        for bb in range(nb):
            decay = jnp.exp(jnp.sum(log_a[bb][rows], axis=0, keepdims=True))
            kd = kdec[bb][rows]
            kd4 = jnp.where(own, jnp.concatenate([kd] * B_HEADS, axis=0), jnp.zeros((), bf16))
            v4 = jnp.concatenate([v_ref[bb, rows, h * B_VAL_DIM:(h + 1) * B_VAL_DIM] for h in range(B_HEADS)],
                                 axis=0)
            d_state = lax.dot_general(v4, kd4, tn, preferred_element_type=f32)
            st[bb] = decay * st[bb] + d_state
            qc = q_ref[bb, rows, :] * (B_KEY_DIM ** -0.5)
            q4 = jnp.where(own, jnp.concatenate([qc] * B_HEADS, axis=0), jnp.zeros((), bf16))
            o4 = lax.dot_general(q4, st[bb].astype(bf16), nt, preferred_element_type=f32)
            for h in range(B_HEADS):
                vcols = slice(h * B_VAL_DIM, (h + 1) * B_VAL_DIM)
                o = _rms(o4[h * CHUNK:(h + 1) * CHUNK], gn_ref[:, vcols])
                r = r_ref[bb, rows, vcols].astype(f32)
                o_ref[bb, rows, vcols] = (o * _silu(r)).astype(o_ref.dtype)
    for bb in range(nb):
        state[bb] = st[bb]


def _gla(proj3, gd3, w_up, b_gate, gla_norm, tq=256, nb=GLA_NB):
    b, s, _ = proj3.shape
    return pl.pallas_call(
        _gla_kernel,
        grid=(b // nb, s // tq),
        in_specs=[pl.BlockSpec((nb, tq, B_K_WIDTH), lambda bi, i: (bi, i, COL_QB // B_K_WIDTH)),
                  pl.BlockSpec((nb, tq, B_K_WIDTH), lambda bi, i: (bi, i, COL_KB // B_K_WIDTH)),
                  pl.BlockSpec((nb, tq, B_V_WIDTH), lambda bi, i: (bi, i, COL_VB // B_V_WIDTH)),
                  pl.BlockSpec((nb, tq, LANES), lambda bi, i: (bi, i, 0)),
                  pl.BlockSpec((nb, tq, B_V_WIDTH), lambda bi, i: (bi, i, COL_R // B_V_WIDTH)),
                  _const_spec((LANES, B_K_WIDTH)),
                  _const_spec((1, B_K_WIDTH)),
                  _const_spec((1, B_V_WIDTH))],
        out_specs=pl.BlockSpec((nb, tq, B_V_WIDTH), lambda bi, i: (bi, i, 0)),
        out_shape=jax.ShapeDtypeStruct((b, s, B_V_WIDTH), bf16),
        scratch_shapes=[pltpu.VMEM((nb, B_VAL_DIM, B_K_WIDTH), f32)],
        compiler_params=_params("parallel", "arbitrary"),
        name="gla",
    )(proj3, proj3, proj3, gd3, proj3, w_up, b_gate, gla_norm)


def _route_top2(h, rw_ref):
    logits = jnp.dot(h.astype(bf16), rw_ref[...], preferred_element_type=f32)
    lane = lax.broadcasted_iota(jnp.int32, logits.shape, 1).astype(f32)
    neg = -jnp.inf
    lg = jnp.where(lane < N_EXPERTS, logits, neg)
    m1 = jnp.max(lg, axis=-1, keepdims=True)
    i1 = jnp.min(jnp.where(lg == m1, lane, float(LANES)), axis=-1, keepdims=True)
    lg2 = jnp.where(lane == i1, neg, lg)
    m2 = jnp.max(lg2, axis=-1, keepdims=True)
    i2 = jnp.min(jnp.where(lg2 == m2, lane, float(LANES)), axis=-1, keepdims=True)
    e2 = jnp.exp(m2 - m1)
    w1 = 1.0 / (1.0 + e2)
    w2 = e2 / (1.0 + e2)
    return jnp.where(lane == 0, i1, jnp.where(lane == 1, i2, jnp.where(lane == 2, w1, jnp.where(lane == 3, w2, 0.0))))


def _mix_out_kernel(*refs, routed):
    if routed:
        (ya_ref, yb_ref, ga_ref, gb_ref, x_ref, wa_ref, wb_ref, wo_ref, gpost_ref, gpre_ref, rw_ref,
         xo_ref, ho_ref, route_ref) = refs
    else:
        (ya_ref, yb_ref, ga_ref, gb_ref, x_ref, wa_ref, wb_ref, wo_ref, gpost_ref, gpre_ref,
         xo_ref, ho_ref) = refs
    a = jnp.dot(ya_ref[...], wa_ref[...], preferred_element_type=f32)
    b = jnp.dot(yb_ref[...], wb_ref[...], preferred_element_type=f32)
    merged = _sigmoid(ga_ref[...].astype(f32)) * a + _sigmoid(gb_ref[...].astype(f32)) * b
    y = jnp.dot(merged.astype(bf16), wo_ref[...], preferred_element_type=f32)
    x1 = x_ref[...] + _rms(y, gpost_ref[...])
    h = _rms(x1, gpre_ref[...])
    xo_ref[...] = x1
    if routed:
        _store_row_tiles(ho_ref, h)
        route_ref[...] = _route_top2(h, rw_ref)
    else:
        ho_ref[...] = h.astype(ho_ref.dtype)


def _mix_out(ya, yb, proj, x, wa, wb, wo, g_post, g_pre, router_w=None, tm=512):
    t = x.shape[0]
    routed = router_w is not None
    row = lambda i: (i, 0)
    in_specs = [pl.BlockSpec((tm, A_WIDTH), row),
                pl.BlockSpec((tm, B_V_WIDTH), row),
                pl.BlockSpec((tm, D_MODEL), lambda i: (i, COL_GA // D_MODEL)),
                pl.BlockSpec((tm, D_MODEL), lambda i: (i, COL_GB // D_MODEL)),
                pl.BlockSpec((tm, D_MODEL), row),
                _const_spec((A_WIDTH, D_MODEL)),
                _const_spec((B_V_WIDTH, D_MODEL)),
                _const_spec((D_MODEL, D_MODEL)),
                _const_spec((1, D_MODEL)),
                _const_spec((1, D_MODEL))]
    args = [ya, yb, proj, proj, x, wa, wb, wo, g_post, g_pre]
    if routed:
        out_specs = [pl.BlockSpec((tm, D_MODEL), row), pl.BlockSpec((tm * ROW_TILE, LANES), row)]
        out_shape = [jax.ShapeDtypeStruct((t, D_MODEL), f32), jax.ShapeDtypeStruct((t * ROW_TILE, LANES), f32)]
    else:
        out_specs = [pl.BlockSpec((tm, D_MODEL), row), pl.BlockSpec((tm, D_MODEL), row)]
        out_shape = [jax.ShapeDtypeStruct((t, D_MODEL), f32), jax.ShapeDtypeStruct((t, D_MODEL), bf16)]
    if routed:
        in_specs.append(_const_spec((D_MODEL, LANES)))
        args.append(router_w)
        out_specs.append(pl.BlockSpec((tm, LANES), row))
        out_shape.append(jax.ShapeDtypeStruct((t, LANES), f32))
    return pl.pallas_call(
        functools.partial(_mix_out_kernel, routed=routed),
        grid=(t // tm,),
        in_specs=in_specs,
        out_specs=out_specs,
        out_shape=out_shape,
        compiler_params=_params("parallel"),
        name="mix_out",
    )(*args)


def _post_ffn(y, x1, p, gpost_ref, wpp_ref, wpg_ref, gple_ref):
    x2 = x1 + _rms(y, gpost_ref[...])
    e = jnp.dot(p.astype(bf16), wpp_ref[...], preferred_element_type=f32)
    e = e * _sigmoid(jnp.dot(x2.astype(bf16), wpg_ref[...], preferred_element_type=f32))
    return x2 + _rms(e, gple_ref[...])


def _swiglu_chunks(x, wg_ref, wu_ref, wd_ref, lead, width):
    acc = None
    for c0, c1 in _chunks(width, FF_CHUNK):
        g = jnp.dot(x, wg_ref[lead + (slice(None), slice(c0, c1))], preferred_element_type=f32)
        u = jnp.dot(x, wu_ref[lead + (slice(None), slice(c0, c1))], preferred_element_type=f32)
        act = (_silu(g) * u).astype(bf16)
        part = jnp.dot(act, wd_ref[lead + (slice(c0, c1), slice(None))], preferred_element_type=f32)
        acc = part if acc is None else acc + part
    return acc


def _dense_ffn_kernel(h_ref, x_ref, p_ref, wg_ref, wu_ref, wd_ref, gpost_ref, wpp_ref, wpg_ref, gple_ref, o_ref):
    y = _swiglu_chunks(h_ref[...], wg_ref, wu_ref, wd_ref, (), wg_ref.shape[1])
    o_ref[...] = _post_ffn(y, x_ref[...], p_ref[...], gpost_ref, wpp_ref, wpg_ref, gple_ref)


def _dense_ffn(h, x1, p, wg, wu, wd, g_post, wpp, wpg, g_ple, tm=512):
    t = x1.shape[0]
    ff = wg.shape[1]
    row = lambda i: (i, 0)
    return pl.pallas_call(
        _dense_ffn_kernel,
        grid=(t // tm,),
        in_specs=[pl.BlockSpec((tm, D_MODEL), row),
                  pl.BlockSpec((tm, D_MODEL), row),
                  pl.BlockSpec((tm, PLE_DIM), row),
                  _const_spec((D_MODEL, ff)),
                  _const_spec((D_MODEL, ff)),
                  _const_spec((ff, D_MODEL)),
                  _const_spec((1, D_MODEL)),
                  _const_spec((PLE_DIM, D_MODEL)),
                  _const_spec((D_MODEL, D_MODEL)),
                  _const_spec((1, D_MODEL))],
        out_specs=pl.BlockSpec((tm, D_MODEL), row),
        out_shape=jax.ShapeDtypeStruct((t, D_MODEL), f32),
        compiler_params=_params("parallel"),
        name="dense_ffn",
    )(h, x1, p, wg, wu, wd, g_post, wpp, wpg, g_ple)


def _scatter_kernel(fill_ref, pos_ref, h_ref, xs_out, zbuf, sem, zsem):
    rows = h_ref.shape[0] // ROW_TILE

    @pl.when(pl.program_id(0) == 0)
    def _():
        zbuf[...] = jnp.zeros_like(zbuf)

        def zero_rows(first_row):
            start = pl.multiple_of(first_row * ROW_TILE, ROW_TILE)
            fill = pltpu.make_async_copy(zbuf, xs_out.at[pl.ds(start, MOE_TM * ROW_TILE)], zsem)
            fill.start()
            fill.wait()

        for e in range(N_EXPERTS):
            zero_rows(fill_ref[e])

        def zero_tile(tile, c):
            zero_rows(tile * MOE_TM)
            return c

        lax.fori_loop(fill_ref[N_EXPERTS], xs_out.shape[0] // (MOE_TM * ROW_TILE), zero_tile, 0)

    def issue(r, c):
        for k in range(2):
            pltpu.make_async_copy(_tile_rows(h_ref, r), _tile_rows(xs_out, pos_ref[2 * r + k]),
                                  sem.at[k]).start(priority=k)
        return c

    lax.fori_loop(0, rows, issue, 0, unroll=ISSUE_UNROLL)
    for k in range(2):
        pltpu.make_async_copy(h_ref, xs_out.at[pl.ds(0, rows * ROW_TILE)], sem.at[k]).wait()


def _scatter_rows(fill_plan, pos, h, n_rows):
    t = h.shape[0] // ROW_TILE
    return pl.pallas_call(
        _scatter_kernel,
        grid_spec=pltpu.PrefetchScalarGridSpec(
            num_scalar_prefetch=1,
            grid=(t // ROW_BLK,),
            in_specs=[pl.BlockSpec((2 * ROW_BLK,), lambda i, fill: (i,), memory_space=pltpu.SMEM),
                      pl.BlockSpec((ROW_BLK * ROW_TILE, LANES), lambda i, fill: (i, 0))],
            out_specs=pl.BlockSpec(memory_space=pl.ANY),
            scratch_shapes=[pltpu.VMEM((MOE_TM * ROW_TILE, LANES), f32),
                            pltpu.SemaphoreType.DMA((2,)),
                            pltpu.SemaphoreType.DMA(())]),
        out_shape=jax.ShapeDtypeStruct((n_rows * ROW_TILE, LANES), f32),
        compiler_params=_params("arbitrary"),
        name="moe_scatter",
    )(fill_plan, pos, h)


def _expert_kernel(te_ref, nu_ref, x_ref, wg_ref, wu_ref, wd_ref, o_ref, acc_ref):
    del te_ref
    i = pl.program_id(0)
    f = pl.program_id(1)

    @pl.when(i < nu_ref[0])
    def _():
        x = _load_row_tiles(x_ref, MOE_TM).astype(bf16)
        y = _swiglu_chunks(x, wg_ref, wu_ref, wd_ref, (0,), wg_ref.shape[2])

        @pl.when(f == 0)
        def _():
            acc_ref[...] = y

        @pl.when(f > 0)
        def _():
            acc_ref[...] += y

    last = f == pl.num_programs(1) - 1

    @pl.when(jnp.logical_and(last, i < nu_ref[0]))
    def _():
        _store_row_tiles(o_ref, acc_ref[...])

    @pl.when(jnp.logical_and(last, i >= nu_ref[0]))
    def _():
        o_ref[...] = jnp.zeros_like(o_ref)


def _expert_ffn(tile_expert, n_used, xs, wg, wu, wd, tf=1792):
    n_rows = xs.shape[0] // ROW_TILE
    ff = wg.shape[2]
    nf = ff // tf

    def f_idx(i, f, nu):
        return jnp.where(i < nu[0], f, nf - 1)

    return pl.pallas_call(
        _expert_kernel,
        grid_spec=pltpu.PrefetchScalarGridSpec(
            num_scalar_prefetch=2,
            grid=(n_rows // MOE_TM, nf),
            in_specs=[pl.BlockSpec((MOE_TM * ROW_TILE, LANES), lambda i, f, te, nu: (i, 0)),
                      pl.BlockSpec((1, D_MODEL, tf), lambda i, f, te, nu: (te[i], 0, f_idx(i, f, nu))),
                      pl.BlockSpec((1, D_MODEL, tf), lambda i, f, te, nu: (te[i], 0, f_idx(i, f, nu))),
                      pl.BlockSpec((1, tf, D_MODEL), lambda i, f, te, nu: (te[i], f_idx(i, f, nu), 0))],
            out_specs=pl.BlockSpec((MOE_TM * ROW_TILE, LANES), lambda i, f, te, nu: (i, 0)),
            scratch_shapes=[pltpu.VMEM((MOE_TM, D_MODEL), f32)]),
        out_shape=jax.ShapeDtypeStruct((n_rows * ROW_TILE, LANES), f32),
        compiler_params=_params("arbitrary", "arbitrary"),
        name="moe_experts",
    )(tile_expert, n_used, xs, wg, wu, wd)


def _combine_kernel(pos0_ref, pos_next_ref, ys_hbm, route_ref, x_ref, p_ref, gpost_ref, wpp_ref, wpg_ref, gple_ref,
                    o_ref, buf, sem):
    i = pl.program_id(0)
    rows = x_ref.shape[0]
    slot = i % 2

    def fetch(pos_ref, into):
        def issue(r, c):
            for k in range(2):
                pltpu.make_async_copy(_tile_rows(ys_hbm, pos_ref[2 * r + k]), _tile_rows(buf.at[into, k], r),
                                      sem.at[into, k]).start(priority=k)
            return c

        lax.fori_loop(0, rows, issue, 0, unroll=ISSUE_UNROLL)

    @pl.when(i == 0)
    def _():
        fetch(pos0_ref, 0)

    @pl.when(i + 1 < pl.num_programs(0))
    def _():
        fetch(pos_next_ref, 1 - slot)

    for k in range(2):
        pltpu.make_async_copy(ys_hbm.at[pl.ds(0, rows * ROW_TILE)], buf.at[slot, k], sem.at[slot, k]).wait()
    route = route_ref[...]
    y = (route[:, 2:3] * _load_row_tiles(buf.at[slot, 0], rows)
         + route[:, 3:4] * _load_row_tiles(buf.at[slot, 1], rows))
    o_ref[...] = _post_ffn(y, x_ref[...], p_ref[...], gpost_ref, wpp_ref, wpg_ref, gple_ref)


def _combine(pos, ys, route, x1, p, g_post, wpp, wpg, g_ple):
    t = x1.shape[0]
    n_blk = t // ROW_BLK
    row = lambda i: (i, 0)
    smem = functools.partial(pl.BlockSpec, (2 * ROW_BLK,), memory_space=pltpu.SMEM)
    return pl.pallas_call(
        _combine_kernel,
        grid=(n_blk,),
        in_specs=[smem(lambda i: (0,)),
                  smem(lambda i: (jnp.minimum(i + 1, n_blk - 1),)),
                  pl.BlockSpec(memory_space=pl.ANY),
                  pl.BlockSpec((ROW_BLK, LANES), row),
                  pl.BlockSpec((ROW_BLK, D_MODEL), row),
                  pl.BlockSpec((ROW_BLK, PLE_DIM), row),
                  _const_spec((1, D_MODEL)),
                  _const_spec((PLE_DIM, D_MODEL)),
                  _const_spec((D_MODEL, D_MODEL)),
                  _const_spec((1, D_MODEL))],
        out_specs=pl.BlockSpec((ROW_BLK, D_MODEL), row),
        out_shape=jax.ShapeDtypeStruct((t, D_MODEL), f32),
        scratch_shapes=[pltpu.VMEM((2, 2, ROW_BLK * ROW_TILE, LANES), f32),
                        pltpu.SemaphoreType.DMA((2, 2))],
        compiler_params=_params("arbitrary"),
        name="moe_combine",
    )(pos, pos, ys, route, x1, p, g_post, wpp, wpg, g_ple)


def _route_slots(route, n_tiles):
    ids = route[:, 0:2].astype(jnp.int32).reshape(-1)
    onehot = (ids[None, :] == jnp.arange(N_EXPERTS, dtype=jnp.int32)[:, None]).astype(jnp.int32)
    csum = jnp.cumsum(onehot, axis=1)
    rank = jnp.sum(onehot * csum, axis=0) - 1
    counts = csum[:, -1]
    padded = ((counts + MOE_TM - 1) // MOE_TM) * MOE_TM
    ends = jnp.cumsum(padded)
    pos = (ends - padded)[ids] + rank
    tile_start = jnp.arange(n_tiles, dtype=jnp.int32) * MOE_TM
    tile_expert = jnp.minimum(jnp.sum((tile_start[:, None] >= ends[None, :]).astype(jnp.int32), axis=1),
                              N_EXPERTS - 1)
    n_used = (ends[-1] // MOE_TM).reshape(1)
    fill_plan = jnp.concatenate([ends - padded + counts, n_used])
    return (pos.astype(jnp.int32), tile_expert.astype(jnp.int32), n_used.astype(jnp.int32),
            fill_plan.astype(jnp.int32))


def _moe(h, route, x1, p, wg, wu, wd, g_post, wpp, wpg, g_ple):
    t = x1.shape[0]
    n_tiles = 2 * t // MOE_TM + N_EXPERTS
    pos, tile_expert, n_used, fill_plan = _route_slots(route, n_tiles)
    xs = _scatter_rows(fill_plan, pos, h, n_tiles * MOE_TM)
    ys = _expert_ffn(tile_expert, n_used, xs, wg, wu, wd)
    return _combine(pos, ys, route, x1, p, g_post, wpp, wpg, g_ple)


def _prep_w_in(w):
    main = jnp.concatenate([w[:, SRC_KA:SRC_VA], w[:, SRC_QB:SRC_GD], w[:, SRC_R:SRC_END]], axis=1)
    wq_t = w[:, SRC_QA:SRC_KA].T
    wv_t = w[:, SRC_VA:SRC_QB].T
    w_gd = jnp.pad(w[:, SRC_GD:SRC_R], ((0, 0), (0, LANES - GATE_RANK)))
    return main.astype(bf16), wq_t.astype(bf16), wv_t.astype(bf16), w_gd.astype(bf16)


def kernel(x, p, w_in, rel_bias, w_gla_gate_up, b_gla_gate, gla_norm, w_branch_a, w_branch_b, w_out, norm_mix_pre, norm_mix_post, norm_ffn_pre, norm_ffn_post, ffn_w_gate, ffn_w_up, ffn_w_down, router_w, moe_w_gate, moe_w_up, moe_w_down, ple_w_proj, ple_w_gate, ple_norm):
    b, s, d = x.shape
    t = b * s
    depth = w_in.shape[0]
    x = x.reshape(t, d)
    vec = lambda a: a.reshape(1, -1).astype(f32)
    for i in range(depth):
        w_main, wq_t, wv_t, w_gd = _prep_w_in(w_in[i])
        g_pre = vec(norm_mix_pre[i])
        proj = _in_proj(x, g_pre, w_main)
        qt, vt, gd = _qv_proj(x, g_pre, wq_t, wv_t, w_gd, b, s)
        proj3 = proj.reshape(b, s, MAIN_WIDTH)
        ya = _attention(qt, proj3, vt, _attn_bias(rel_bias[i])).reshape(t, A_WIDTH)
        w_up = jnp.pad(w_gla_gate_up[i], ((0, LANES - GATE_RANK), (0, 0))).astype(bf16)
        yb = _gla(proj3, gd.reshape(b, s, LANES), w_up, vec(b_gla_gate[i]), vec(gla_norm[i])).reshape(t, B_V_WIDTH)
        j = i // 2
        routed = i % 2 == 1
        rw = jnp.pad(router_w[j], ((0, 0), (0, LANES - N_EXPERTS))).astype(bf16) if routed else None
        outs = _mix_out(ya, yb, proj, x, w_branch_a[i].astype(bf16), w_branch_b[i].astype(bf16),
                        w_out[i].astype(bf16), vec(norm_mix_post[i]), vec(norm_ffn_pre[i]), rw)
        p_i = p[i].reshape(t, PLE_DIM)
        tail = (vec(norm_ffn_post[i]), ple_w_proj[i].astype(bf16), ple_w_gate[i].astype(bf16), vec(ple_norm[i]))
        if routed:
            x1, h, route = outs
            x = _moe(h, route, x1, p_i, moe_w_gate[j].astype(bf16), moe_w_up[j].astype(bf16),
                     moe_w_down[j].astype(bf16), *tail)
        else:
            x1, h = outs
            x = _dense_ffn(h, x1, p_i, ffn_w_gate[j].astype(bf16), ffn_w_up[j].astype(bf16),
                           ffn_w_down[j].astype(bf16), *tail)
    return x.reshape(b, s, d)
```

```python
import functools

import numpy as np
import jax
import jax.numpy as jnp
from jax import lax
from jax.experimental import pallas as pl
from jax.experimental.pallas import tpu as pltpu

f32 = jnp.float32
bf16 = jnp.bfloat16

D_MODEL = 1024
CHUNK = 64
N_PREV_CHUNKS = 8
A_HEADS = 8
A_HEAD_DIM = 64
A_WIDTH = A_HEADS * A_HEAD_DIM
REL_CLIP = 128
B_HEADS = 4
B_KEY_DIM = 64
B_VAL_DIM = 128
B_K_WIDTH = B_HEADS * B_KEY_DIM
B_V_WIDTH = B_HEADS * B_VAL_DIM
GATE_RANK = 16
GATE_TEMP = 16.0
N_EXPERTS = 8
PLE_DIM = 256
NORM_EPS = 1e-6

LANES = 128
VMEM_LIMIT = 56 * 1024 * 1024

SRC_QA, SRC_KA, SRC_VA, SRC_QB, SRC_GD, SRC_R = 0, 512, 1024, 1536, 2560, 2576
SRC_END = 5136
COL_KA, COL_QB, COL_KB, COL_VB, COL_R, COL_GA, COL_GB = 0, 512, 768, 1024, 1536, 2048, 3072
MAIN_WIDTH = 4096
IN_TILE_N = 2048

QBLK = 2 * CHUNK
KBAND = (N_PREV_CHUNKS + 2) * CHUNK
KPAD = N_PREV_CHUNKS * CHUNK

SUBLANES = 8
ROW_TILE = D_MODEL // LANES
assert ROW_TILE == SUBLANES
MOE_TM = 512
ROW_BLK = 512
FF_CHUNK = 512
ISSUE_UNROLL = 8
GLA_NB = 2
W_STAGE_ROWS_IN = 128
W_STAGE_ROWS_OUT = 256


def _params(*sem):
    return pltpu.CompilerParams(dimension_semantics=sem, vmem_limit_bytes=VMEM_LIMIT)


def _rms(x, g):
    return x * lax.rsqrt(jnp.mean(x * x, axis=-1, keepdims=True) + NORM_EPS) * g


def _sigmoid(x):
    return 1.0 / (1.0 + jnp.exp(-x))


def _silu(x):
    return x * _sigmoid(x)


def _chunks(n, c):
    return [(s, min(s + c, n)) for s in range(0, n, c)]


def _store_row_tiles(ref, x):
    rows = x.shape[0]
    for c in range(ROW_TILE):
        ref[pl.ds(c, rows, stride=ROW_TILE), :] = x[:, c * LANES:(c + 1) * LANES].astype(ref.dtype)


def _load_row_tiles(ref, rows):
    return jnp.concatenate([ref[pl.ds(c, rows, stride=ROW_TILE), :] for c in range(ROW_TILE)], axis=1)


def _tile_rows(ref, r):
    return ref.at[pl.ds(pl.multiple_of(r * ROW_TILE, ROW_TILE), ROW_TILE)]


def _const_spec(shape):
    nd = len(shape)
    return pl.BlockSpec(shape, lambda *_: (0,) * nd, pipeline_mode=pl.Buffered(1))


def _qv_proj_kernel(x_ref, g_ref, wq_ref, wv_ref, wgd_ref, h_ref, qt_ref, vt_ref, gd_ref):
    nt = (((1,), (1,)), ((), ()))
    h = _rms(x_ref[...], g_ref[...]).astype(bf16)
    h_ref[...] = h
    qt_ref[0] = lax.dot_general(wq_ref[...], h, nt, preferred_element_type=f32).astype(qt_ref.dtype)
    vt = lax.dot_general(wv_ref[...], h, nt, preferred_element_type=f32).astype(vt_ref.dtype)
    for kt in range(vt_ref.shape[1]):
        vt_ref[0, kt] = vt[:, kt * LANES:(kt + 1) * LANES]
    gd_ref[...] = jnp.dot(h, wgd_ref[...], preferred_element_type=f32).astype(gd_ref.dtype)


def _qv_proj(x, g, wq_t, wv_t, w_gd, b, s, tm=1024):
    t = x.shape[0]
    per_b = s // tm
    return pl.pallas_call(
        _qv_proj_kernel,
        grid=(t // tm,),
        in_specs=[pl.BlockSpec((tm, D_MODEL), lambda i: (i, 0)),
                  _const_spec((1, D_MODEL)),
                  _const_spec((A_WIDTH, D_MODEL)),
                  _const_spec((A_WIDTH, D_MODEL)),
                  _const_spec((D_MODEL, LANES))],
        out_specs=[pl.BlockSpec((tm, D_MODEL), lambda i: (i, 0)),
                   pl.BlockSpec((1, A_WIDTH, tm), lambda i: (i // per_b, 0, i % per_b)),
                   pl.BlockSpec((1, tm // LANES, A_WIDTH, LANES), lambda i: (i // per_b, i % per_b, 0, 0)),
                   pl.BlockSpec((tm, LANES), lambda i: (i, 0))],
        out_shape=[jax.ShapeDtypeStruct((t, D_MODEL), bf16),
                   jax.ShapeDtypeStruct((b, A_WIDTH, s), bf16),
                   jax.ShapeDtypeStruct((b, s // LANES, A_WIDTH, LANES), bf16),
                   jax.ShapeDtypeStruct((t, LANES), bf16)],
        compiler_params=_params("parallel"),
        name="qv_proj",
    )(x, g, wq_t, wv_t, w_gd)


def _in_proj_kernel(h_ref, w_ref, o_ref):
    o_ref[...] = jnp.dot(h_ref[...], w_ref[...], preferred_element_type=f32).astype(o_ref.dtype)


def _in_proj(h, w, tm=1024):
    t = h.shape[0]
    n = w.shape[1]
    return pl.pallas_call(
        _in_proj_kernel,
        grid=(t // tm, n // IN_TILE_N),
        in_specs=[pl.BlockSpec((tm, D_MODEL), lambda i, j: (i, 0)),
                  pl.BlockSpec((D_MODEL, IN_TILE_N), lambda i, j: (0, j))],
        out_specs=pl.BlockSpec((tm, IN_TILE_N), lambda i, j: (i, j)),
        out_shape=jax.ShapeDtypeStruct((t, n), bf16),
        compiler_params=_params("parallel", "arbitrary"),
        name="in_proj",
    )(h, w)


def _attn_kernel(qt_ref, k_ref, vt_ref, bias_ref, o_ref):
    i = pl.program_id(1)
    row = lax.broadcasted_iota(jnp.int32, (LANES, QBLK), 0)
    upper = row >= A_HEAD_DIM
    n_kt = KBAND // LANES
    kt_seq = [i + kt - KPAD // LANES for kt in range(n_kt)]
    kt_clamped = [jnp.maximum(j, 0) for j in kt_seq]
    for hp in range(A_WIDTH // LANES):
        pair = slice(hp * LANES, (hp + 1) * LANES)
        qt = qt_ref[0, pair, :] * (A_HEAD_DIM ** -0.5)
        kps = [k_ref[0, pl.ds(pl.multiple_of(j * LANES, LANES), LANES), pair] for j in kt_clamped]
        outs = []
        for s in range(2):
            h = 2 * hp + s
            qm = jnp.where(upper == bool(s), qt, jnp.zeros_like(qt))
            tiles = []
            for kt in range(n_kt):
                t = jnp.dot(kps[kt], qm, preferred_element_type=f32) + bias_ref[h, kt]
                tiles.append(jnp.where(kt_seq[kt] >= 0, t, -1e30))
            m = tiles[0]
            for t in tiles[1:]:
                m = jnp.maximum(m, t)
            m = jnp.max(m, axis=0, keepdims=True)
            l = None
            o = None
            for kt in range(n_kt):
                p = jnp.exp(tiles[kt] - m)
                ls = jnp.sum(p, axis=0, keepdims=True)
                vt = vt_ref[0, kt_clamped[kt], h * A_HEAD_DIM:(h + 1) * A_HEAD_DIM, :]
                part = jnp.dot(vt, p.astype(bf16), preferred_element_type=f32)
                l = ls if l is None else l + ls
                o = part if o is None else o + part
            outs.append(o / l)
        o_ref[0, :, pair] = jnp.concatenate(outs, axis=0).T.astype(o_ref.dtype)


def _attention(qt, proj3, vt, bias_t):
    b, s, _ = proj3.shape
    return pl.pallas_call(
        _attn_kernel,
        grid=(b, s // QBLK),
        in_specs=[pl.BlockSpec((1, A_WIDTH, QBLK), lambda bi, i: (bi, 0, i)),
                  pl.BlockSpec((1, s, A_WIDTH), lambda bi, i: (bi, 0, COL_KA // A_WIDTH)),
                  pl.BlockSpec((1, s // LANES, A_WIDTH, LANES), lambda bi, i: (bi, 0, 0, 0)),
                  _const_spec((A_HEADS, KBAND // LANES, LANES, QBLK))],
        out_specs=pl.BlockSpec((1, QBLK, A_WIDTH), lambda bi, i: (bi, i, 0)),
        out_shape=jax.ShapeDtypeStruct((b, s, A_WIDTH), bf16),
        compiler_params=_params("parallel", "arbitrary"),
        name="chunk_attention",
    )(qt, proj3, vt, bias_t)


def _attn_bias(rel_bias):
    h = rel_bias.shape[0]
    r = np.arange(QBLK)[:, None]
    k = np.arange(KBAND)[None, :]
    visible = np.where(r < CHUNK, k < KBAND - CHUNK, k >= CHUNK)
    span = QBLK + KBAND - 1
    n_far = span - 2 * REL_CLIP
    line = jnp.concatenate([jnp.broadcast_to(rel_bias[:, -1:], (h, n_far)), rel_bias[:, :0:-1]], axis=1)
    line = jnp.pad(line.astype(f32), ((0, 0), (0, 1)))
    skew = jnp.tile(line, (1, QBLK))[:, :QBLK * span].reshape(h, QBLK, span)
    bias = jnp.where(visible[None], skew[:, :, QBLK - 1:], -1e30)
    return jnp.swapaxes(bias, 1, 2).reshape(h, KBAND // LANES, LANES, QBLK)


def _gla_kernel(q_ref, k_ref, v_ref, gd_ref, r_ref, wup_ref, b_ref, gn_ref, o_ref, state):
    @pl.when(pl.program_id(1) == 0)
    def _():
        state[...] = jnp.zeros_like(state)

    nb, tq = q_ref.shape[0], q_ref.shape[1]
    tn = (((0,), (0,)), ((), ()))
    nt = (((1,), (1,)), ((), ()))
    row = lax.broadcasted_iota(jnp.int32, (tq, tq), 0)
    col = lax.broadcasted_iota(jnp.int32, (tq, tq), 1)
    later = jnp.logical_and(col > row, col // CHUNK == row // CHUNK)
    later = jnp.where(later, 1.0, 0.0).astype(bf16)
    log_a, kdec = [], []
    for bb in range(nb):
        z = jnp.dot(gd_ref[bb], wup_ref[...], preferred_element_type=f32) + b_ref[...]
        la = (jnp.minimum(z, 0.0) - jnp.log(1.0 + jnp.exp(-jnp.abs(z)))) / GATE_TEMP
        la_hi = la.astype(bf16)
        la_lo = (la - la_hi.astype(f32)).astype(bf16)
        rev = (jnp.dot(later, la_hi, preferred_element_type=f32)
               + jnp.dot(later, la_lo, preferred_element_type=f32))
        log_a.append(la)
        kdec.append((k_ref[bb].astype(f32) * jnp.exp(rev)).astype(bf16))
    hrow = lax.broadcasted_iota(jnp.int32, (B_HEADS * CHUNK, B_K_WIDTH), 0) // CHUNK
    hlane = lax.broadcasted_iota(jnp.int32, (B_HEADS * CHUNK, B_K_WIDTH), 1) // B_KEY_DIM
    own = hrow == hlane
    st = [state[bb] for bb in range(nb)]
    for c in range(tq // CHUNK):
        rows = slice(c * CHUNK, (c + 1) * CHUNK)
        for bb in range(nb):
            decay = jnp.exp(jnp.sum(log_a[bb][rows], axis=0, keepdims=True))
            kd = kdec[bb][rows]
            kd4 = jnp.where(own, jnp.concatenate([kd] * B_HEADS, axis=0), jnp.zeros((), bf16))
            v4 = jnp.concatenate([v_ref[bb, rows, h * B_VAL_DIM:(h + 1) * B_VAL_DIM] for h in range(B_HEADS)],
                                 axis=0)
            d_state = lax.dot_general(v4, kd4, tn, preferred_element_type=f32)
            st[bb] = decay * st[bb] + d_state
            qc = q_ref[bb, rows, :] * (B_KEY_DIM ** -0.5)
            q4 = jnp.where(own, jnp.concatenate([qc] * B_HEADS, axis=0), jnp.zeros((), bf16))
            o4 = lax.dot_general(q4, st[bb].astype(bf16), nt, preferred_element_type=f32)
            for h in range(B_HEADS):
                vcols = slice(h * B_VAL_DIM, (h + 1) * B_VAL_DIM)
                o = _rms(o4[h * CHUNK:(h + 1) * CHUNK], gn_ref[:, vcols])
                r = r_ref[bb, rows, vcols].astype(f32)
                o_ref[bb, rows, vcols] = (o * _silu(r)).astype(o_ref.dtype)
    for bb in range(nb):
        state[bb] = st[bb]


def _gla(proj3, gd3, w_up, b_gate, gla_norm, tq=256, nb=GLA_NB):
    b, s, _ = proj3.shape
    return pl.pallas_call(
        _gla_kernel,
        grid=(b // nb, s // tq),
        in_specs=[pl.BlockSpec((nb, tq, B_K_WIDTH), lambda bi, i: (bi, i, COL_QB // B_K_WIDTH)),
                  pl.BlockSpec((nb, tq, B_K_WIDTH), lambda bi, i: (bi, i, COL_KB // B_K_WIDTH)),
                  pl.BlockSpec((nb, tq, B_V_WIDTH), lambda bi, i: (bi, i, COL_VB // B_V_WIDTH)),
                  pl.BlockSpec((nb, tq, LANES), lambda bi, i: (bi, i, 0)),
                  pl.BlockSpec((nb, tq, B_V_WIDTH), lambda bi, i: (bi, i, COL_R // B_V_WIDTH)),
                  _const_spec((LANES, B_K_WIDTH)),
                  _const_spec((1, B_K_WIDTH)),
                  _const_spec((1, B_V_WIDTH))],
        out_specs=pl.BlockSpec((nb, tq, B_V_WIDTH), lambda bi, i: (bi, i, 0)),
        out_shape=jax.ShapeDtypeStruct((b, s, B_V_WIDTH), bf16),
        scratch_shapes=[pltpu.VMEM((nb, B_VAL_DIM, B_K_WIDTH), f32)],
        compiler_params=_params("parallel", "arbitrary"),
        name="gla",
    )(proj3, proj3, proj3, gd3, proj3, w_up, b_gate, gla_norm)


def _route_top2(h, rw_ref):
    logits = jnp.dot(h.astype(bf16), rw_ref[...], preferred_element_type=f32)
    lane = lax.broadcasted_iota(jnp.int32, logits.shape, 1).astype(f32)
    neg = -jnp.inf
    lg = jnp.where(lane < N_EXPERTS, logits, neg)
    m1 = jnp.max(lg, axis=-1, keepdims=True)
    i1 = jnp.min(jnp.where(lg == m1, lane, float(LANES)), axis=-1, keepdims=True)
    lg2 = jnp.where(lane == i1, neg, lg)
    m2 = jnp.max(lg2, axis=-1, keepdims=True)
    i2 = jnp.min(jnp.where(lg2 == m2, lane, float(LANES)), axis=-1, keepdims=True)
    e2 = jnp.exp(m2 - m1)
    w1 = 1.0 / (1.0 + e2)
    w2 = e2 / (1.0 + e2)
    return jnp.where(lane == 0, i1, jnp.where(lane == 1, i2, jnp.where(lane == 2, w1, jnp.where(lane == 3, w2, 0.0))))


def _mix_out_kernel(*refs, routed):
    if routed:
        (ya_ref, yb_ref, ga_ref, gb_ref, x_ref, wa_ref, wb_ref, wo_ref, gpost_ref, gpre_ref, rw_ref,
         xo_ref, ho_ref, route_ref) = refs
    else:
        (ya_ref, yb_ref, ga_ref, gb_ref, x_ref, wa_ref, wb_ref, wo_ref, gpost_ref, gpre_ref,
         xo_ref, ho_ref) = refs
    a = jnp.dot(ya_ref[...], wa_ref[...], preferred_element_type=f32)
    b = jnp.dot(yb_ref[...], wb_ref[...], preferred_element_type=f32)
    merged = _sigmoid(ga_ref[...].astype(f32)) * a + _sigmoid(gb_ref[...].astype(f32)) * b
    y = jnp.dot(merged.astype(bf16), wo_ref[...], preferred_element_type=f32)
    x1 = x_ref[...] + _rms(y, gpost_ref[...])
    h = _rms(x1, gpre_ref[...])
    xo_ref[...] = x1
    if routed:
        _store_row_tiles(ho_ref, h)
        route_ref[...] = _route_top2(h, rw_ref)
    else:
        ho_ref[...] = h.astype(ho_ref.dtype)


def _mix_out(ya, yb, proj, x, wa, wb, wo, g_post, g_pre, router_w=None, tm=512):
    t = x.shape[0]
    routed = router_w is not None
    row = lambda i: (i, 0)
    in_specs = [pl.BlockSpec((tm, A_WIDTH), row),
                pl.BlockSpec((tm, B_V_WIDTH), row),
                pl.BlockSpec((tm, D_MODEL), lambda i: (i, COL_GA // D_MODEL)),
                pl.BlockSpec((tm, D_MODEL), lambda i: (i, COL_GB // D_MODEL)),
                pl.BlockSpec((tm, D_MODEL), row),
                _const_spec((A_WIDTH, D_MODEL)),
                _const_spec((B_V_WIDTH, D_MODEL)),
                _const_spec((D_MODEL, D_MODEL)),
                _const_spec((1, D_MODEL)),
                _const_spec((1, D_MODEL))]
    args = [ya, yb, proj, proj, x, wa, wb, wo, g_post, g_pre]
    if routed:
        out_specs = [pl.BlockSpec((tm, D_MODEL), row), pl.BlockSpec((tm * ROW_TILE, LANES), row)]
        out_shape = [jax.ShapeDtypeStruct((t, D_MODEL), f32), jax.ShapeDtypeStruct((t * ROW_TILE, LANES), f32)]
    else:
        out_specs = [pl.BlockSpec((tm, D_MODEL), row), pl.BlockSpec((tm, D_MODEL), row)]
        out_shape = [jax.ShapeDtypeStruct((t, D_MODEL), f32), jax.ShapeDtypeStruct((t, D_MODEL), bf16)]
    if routed:
        in_specs.append(_const_spec((D_MODEL, LANES)))
        args.append(router_w)
        out_specs.append(pl.BlockSpec((tm, LANES), row))
        out_shape.append(jax.ShapeDtypeStruct((t, LANES), f32))
    return pl.pallas_call(
        functools.partial(_mix_out_kernel, routed=routed),
        grid=(t // tm,),
        in_specs=in_specs,
        out_specs=out_specs,
        out_shape=out_shape,
        compiler_params=_params("parallel"),
        name="mix_out",
    )(*args)


def _post_ffn(y, x1, p, gpost_ref, wpp_ref, wpg_ref, gple_ref):
    x2 = x1 + _rms(y, gpost_ref[...])
    e = jnp.dot(p.astype(bf16), wpp_ref[...], preferred_element_type=f32)
    e = e * _sigmoid(jnp.dot(x2.astype(bf16), wpg_ref[...], preferred_element_type=f32))
    return x2 + _rms(e, gple_ref[...])


def _swiglu_chunks(x, wg_ref, wu_ref, wd_ref, lead, width):
    acc = None
    for c0, c1 in _chunks(width, FF_CHUNK):
        g = jnp.dot(x, wg_ref[lead + (slice(None), slice(c0, c1))], preferred_element_type=f32)
        u = jnp.dot(x, wu_ref[lead + (slice(None), slice(c0, c1))], preferred_element_type=f32)
        act = (_silu(g) * u).astype(bf16)
        part = jnp.dot(act, wd_ref[lead + (slice(c0, c1), slice(None))], preferred_element_type=f32)
        acc = part if acc is None else acc + part
    return acc


def _fetch_cast(src_hbm, dst_ref, stage, sem, chunk):
    n = src_hbm.shape[0] // chunk
    assert n * chunk == src_hbm.shape[0]
    copies = [pltpu.make_async_copy(src_hbm.at[pl.ds(c * chunk, chunk)], stage.at[c % 2], sem.at[c % 2])
              for c in range(n)]
    copies[0].start()
    for c in range(n):
        if c + 1 < n:
            copies[c + 1].start()
        copies[c].wait()
        dst_ref[c * chunk:(c + 1) * chunk, :] = stage[c % 2].astype(dst_ref.dtype)


def _dense_ffn_kernel(h_ref, x_ref, p_ref, wg_hbm, wu_hbm, wd_hbm, gpost_ref, wpp_ref, wpg_ref, gple_ref, o_ref,
                      wg_ref, wu_ref, wd_ref, stage_in, stage_out, sem):
    @pl.when(pl.program_id(0) == 0)
    def _():
        _fetch_cast(wg_hbm, wg_ref, stage_in, sem, stage_in.shape[1])
        _fetch_cast(wu_hbm, wu_ref, stage_in, sem, stage_in.shape[1])
        _fetch_cast(wd_hbm, wd_ref, stage_out, sem, stage_out.shape[1])

    y = _swiglu_chunks(h_ref[...], wg_ref, wu_ref, wd_ref, (), wg_ref.shape[1])
    o_ref[...] = _post_ffn(y, x_ref[...], p_ref[...], gpost_ref, wpp_ref, wpg_ref, gple_ref)


def _dense_ffn(h, x1, p, wg, wu, wd, g_post, wpp, wpg, g_ple, tm=512):
    t = x1.shape[0]
    ff = wg.shape[1]
    row = lambda i: (i, 0)
    hbm = pl.BlockSpec(memory_space=pl.ANY)
    return pl.pallas_call(
        _dense_ffn_kernel,
        grid=(t // tm,),
        in_specs=[pl.BlockSpec((tm, D_MODEL), row),
                  pl.BlockSpec((tm, D_MODEL), row),
                  pl.BlockSpec((tm, PLE_DIM), row),
                  hbm, hbm, hbm,
                  _const_spec((1, D_MODEL)),
                  _const_spec((PLE_DIM, D_MODEL)),
                  _const_spec((D_MODEL, D_MODEL)),
                  _const_spec((1, D_MODEL))],
        out_specs=pl.BlockSpec((tm, D_MODEL), row),
        out_shape=jax.ShapeDtypeStruct((t, D_MODEL), f32),
        scratch_shapes=[pltpu.VMEM((D_MODEL, ff), bf16),
                        pltpu.VMEM((D_MODEL, ff), bf16),
                        pltpu.VMEM((ff, D_MODEL), bf16),
                        pltpu.VMEM((2, W_STAGE_ROWS_IN, ff), f32),
                        pltpu.VMEM((2, W_STAGE_ROWS_OUT, D_MODEL), f32),
                        pltpu.SemaphoreType.DMA((2,))],
        compiler_params=_params("arbitrary"),
        name="dense_ffn",
    )(h, x1, p, wg, wu, wd, g_post, wpp, wpg, g_ple)


def _scatter_kernel(fill_ref, pos_ref, h_ref, xs_out, zbuf, sem, zsem):
    rows = h_ref.shape[0] // ROW_TILE

    @pl.when(pl.program_id(0) == 0)
    def _():
        zbuf[...] = jnp.zeros_like(zbuf)

        def zero_rows(first_row):
            start = pl.multiple_of(first_row * ROW_TILE, ROW_TILE)
            fill = pltpu.make_async_copy(zbuf, xs_out.at[pl.ds(start, MOE_TM * ROW_TILE)], zsem)
            fill.start()
            fill.wait()

        for e in range(N_EXPERTS):
            zero_rows(fill_ref[e])

        def zero_tile(tile, c):
            zero_rows(tile * MOE_TM)
            return c

        lax.fori_loop(fill_ref[N_EXPERTS], xs_out.shape[0] // (MOE_TM * ROW_TILE), zero_tile, 0)

    def issue(r, c):
        for k in range(2):
            pltpu.make_async_copy(_tile_rows(h_ref, r), _tile_rows(xs_out, pos_ref[2 * r + k]),
                                  sem.at[k]).start(priority=k)
        return c

    lax.fori_loop(0, rows, issue, 0, unroll=ISSUE_UNROLL)
    for k in range(2):
        pltpu.make_async_copy(h_ref, xs_out.at[pl.ds(0, rows * ROW_TILE)], sem.at[k]).wait()


def _scatter_rows(fill_plan, pos, h, n_rows):
    t = h.shape[0] // ROW_TILE
    return pl.pallas_call(
        _scatter_kernel,
        grid_spec=pltpu.PrefetchScalarGridSpec(
            num_scalar_prefetch=1,
            grid=(t // ROW_BLK,),
            in_specs=[pl.BlockSpec((2 * ROW_BLK,), lambda i, fill: (i,), memory_space=pltpu.SMEM),
                      pl.BlockSpec((ROW_BLK * ROW_TILE, LANES), lambda i, fill: (i, 0))],
            out_specs=pl.BlockSpec(memory_space=pl.ANY),
            scratch_shapes=[pltpu.VMEM((MOE_TM * ROW_TILE, LANES), f32),
                            pltpu.SemaphoreType.DMA((2,)),
                            pltpu.SemaphoreType.DMA(())]),
        out_shape=jax.ShapeDtypeStruct((n_rows * ROW_TILE, LANES), f32),
        compiler_params=_params("arbitrary"),
        name="moe_scatter",
    )(fill_plan, pos, h)


def _expert_kernel(te_ref, nu_ref, x_ref, wg_ref, wu_ref, wd_ref, o_ref, acc_ref):
    del te_ref
    i = pl.program_id(0)
    f = pl.program_id(1)

    @pl.when(i < nu_ref[0])
    def _():
        x = _load_row_tiles(x_ref, MOE_TM).astype(bf16)
        y = _swiglu_chunks(x, wg_ref, wu_ref, wd_ref, (0,), wg_ref.shape[2])

        @pl.when(f == 0)
        def _():
            acc_ref[...] = y

        @pl.when(f > 0)
        def _():
            acc_ref[...] += y

    last = f == pl.num_programs(1) - 1

    @pl.when(jnp.logical_and(last, i < nu_ref[0]))
    def _():
        _store_row_tiles(o_ref, acc_ref[...])

    @pl.when(jnp.logical_and(last, i >= nu_ref[0]))
    def _():
        o_ref[...] = jnp.zeros_like(o_ref)


def _expert_ffn(tile_expert, n_used, xs, wg, wu, wd, tf=1792):
    n_rows = xs.shape[0] // ROW_TILE
    ff = wg.shape[2]
    nf = ff // tf

    def f_idx(i, f, nu):
        return jnp.where(i < nu[0], f, nf - 1)

    return pl.pallas_call(
        _expert_kernel,
        grid_spec=pltpu.PrefetchScalarGridSpec(
            num_scalar_prefetch=2,
            grid=(n_rows // MOE_TM, nf),
            in_specs=[pl.BlockSpec((MOE_TM * ROW_TILE, LANES), lambda i, f, te, nu: (i, 0)),
                      pl.BlockSpec((1, D_MODEL, tf), lambda i, f, te, nu: (te[i], 0, f_idx(i, f, nu))),
                      pl.BlockSpec((1, D_MODEL, tf), lambda i, f, te, nu: (te[i], 0, f_idx(i, f, nu))),
                      pl.BlockSpec((1, tf, D_MODEL), lambda i, f, te, nu: (te[i], f_idx(i, f, nu), 0))],
            out_specs=pl.BlockSpec((MOE_TM * ROW_TILE, LANES), lambda i, f, te, nu: (i, 0)),
            scratch_shapes=[pltpu.VMEM((MOE_TM, D_MODEL), f32)]),
        out_shape=jax.ShapeDtypeStruct((n_rows * ROW_TILE, LANES), f32),
        compiler_params=_params("arbitrary", "arbitrary"),
        name="moe_experts",
    )(tile_expert, n_used, xs, wg, wu, wd)


def _combine_kernel(pos0_ref, pos_next_ref, ys_hbm, route_ref, x_ref, p_ref, gpost_ref, wpp_ref, wpg_ref, gple_ref,
                    o_ref, buf, sem):
    i = pl.program_id(0)
    rows = x_ref.shape[0]
    slot = i % 2

    def fetch(pos_ref, into):
        def issue(r, c):
            for k in range(2):
                pltpu.make_async_copy(_tile_rows(ys_hbm, pos_ref[2 * r + k]), _tile_rows(buf.at[into, k], r),
                                      sem.at[into, k]).start(priority=k)
            return c

        lax.fori_loop(0, rows, issue, 0, unroll=ISSUE_UNROLL)

    @pl.when(i == 0)
    def _():
        fetch(pos0_ref, 0)

    @pl.when(i + 1 < pl.num_programs(0))
    def _():
        fetch(pos_next_ref, 1 - slot)

    for k in range(2):
        pltpu.make_async_copy(ys_hbm.at[pl.ds(0, rows * ROW_TILE)], buf.at[slot, k], sem.at[slot, k]).wait()
    route = route_ref[...]
    y = (route[:, 2:3] * _load_row_tiles(buf.at[slot, 0], rows)
         + route[:, 3:4] * _load_row_tiles(buf.at[slot, 1], rows))
    o_ref[...] = _post_ffn(y, x_ref[...], p_ref[...], gpost_ref, wpp_ref, wpg_ref, gple_ref)


def _combine(pos, ys, route, x1, p, g_post, wpp, wpg, g_ple):
    t = x1.shape[0]
    n_blk = t // ROW_BLK
    row = lambda i: (i, 0)
    smem = functools.partial(pl.BlockSpec, (2 * ROW_BLK,), memory_space=pltpu.SMEM)
    return pl.pallas_call(
        _combine_kernel,
        grid=(n_blk,),
        in_specs=[smem(lambda i: (0,)),
                  smem(lambda i: (jnp.minimum(i + 1, n_blk - 1),)),
                  pl.BlockSpec(memory_space=pl.ANY),
                  pl.BlockSpec((ROW_BLK, LANES), row),
                  pl.BlockSpec((ROW_BLK, D_MODEL), row),
                  pl.BlockSpec((ROW_BLK, PLE_DIM), row),
                  _const_spec((1, D_MODEL)),
                  _const_spec((PLE_DIM, D_MODEL)),
                  _const_spec((D_MODEL, D_MODEL)),
                  _const_spec((1, D_MODEL))],
        out_specs=pl.BlockSpec((ROW_BLK, D_MODEL), row),
        out_shape=jax.ShapeDtypeStruct((t, D_MODEL), f32),
        scratch_shapes=[pltpu.VMEM((2, 2, ROW_BLK * ROW_TILE, LANES), f32),
                        pltpu.SemaphoreType.DMA((2, 2))],
        compiler_params=_params("arbitrary"),
        name="moe_combine",
    )(pos, pos, ys, route, x1, p, g_post, wpp, wpg, g_ple)


def _route_slots(route, n_tiles):
    ids = route[:, 0:2].astype(jnp.int32).reshape(-1)
    onehot = (ids[None, :] == jnp.arange(N_EXPERTS, dtype=jnp.int32)[:, None]).astype(jnp.int32)
    csum = jnp.cumsum(onehot, axis=1)
    rank = jnp.sum(onehot * csum, axis=0) - 1
    counts = csum[:, -1]
    padded = ((counts + MOE_TM - 1) // MOE_TM) * MOE_TM
    ends = jnp.cumsum(padded)
    pos = (ends - padded)[ids] + rank
    tile_start = jnp.arange(n_tiles, dtype=jnp.int32) * MOE_TM
    tile_expert = jnp.minimum(jnp.sum((tile_start[:, None] >= ends[None, :]).astype(jnp.int32), axis=1),
                              N_EXPERTS - 1)
    n_used = (ends[-1] // MOE_TM).reshape(1)
    fill_plan = jnp.concatenate([ends - padded + counts, n_used])
    return (pos.astype(jnp.int32), tile_expert.astype(jnp.int32), n_used.astype(jnp.int32),
            fill_plan.astype(jnp.int32))


def _moe(h, route, x1, p, wg, wu, wd, g_post, wpp, wpg, g_ple):
    t = x1.shape[0]
    n_tiles = 2 * t // MOE_TM + N_EXPERTS
    pos, tile_expert, n_used, fill_plan = _route_slots(route, n_tiles)
    xs = _scatter_rows(fill_plan, pos, h, n_tiles * MOE_TM)
    ys = _expert_ffn(tile_expert, n_used, xs, wg, wu, wd)
    return _combine(pos, ys, route, x1, p, g_post, wpp, wpg, g_ple)


def _prep_w_in(w):
    main = jnp.concatenate([w[:, SRC_KA:SRC_VA], w[:, SRC_QB:SRC_GD], w[:, SRC_R:SRC_END]], axis=1)
    wq_t = w[:, SRC_QA:SRC_KA].T
    wv_t = w[:, SRC_VA:SRC_QB].T
    w_gd = jnp.pad(w[:, SRC_GD:SRC_R], ((0, 0), (0, LANES - GATE_RANK)))
    return main.astype(bf16), wq_t.astype(bf16), wv_t.astype(bf16), w_gd.astype(bf16)


def kernel(x, p, w_in, rel_bias, w_gla_gate_up, b_gla_gate, gla_norm, w_branch_a, w_branch_b, w_out, norm_mix_pre, norm_mix_post, norm_ffn_pre, norm_ffn_post, ffn_w_gate, ffn_w_up, ffn_w_down, router_w, moe_w_gate, moe_w_up, moe_w_down, ple_w_proj, ple_w_gate, ple_norm):
    b, s, d = x.shape
    t = b * s
    depth = w_in.shape[0]
    x = x.reshape(t, d)
    vec = lambda a: a.reshape(1, -1).astype(f32)
    for i in range(depth):
        w_main, wq_t, wv_t, w_gd = _prep_w_in(w_in[i])
        h_in, qt, vt, gd = _qv_proj(x, vec(norm_mix_pre[i]), wq_t, wv_t, w_gd, b, s)
        proj = _in_proj(h_in, w_main)
        proj3 = proj.reshape(b, s, MAIN_WIDTH)
        ya = _attention(qt, proj3, vt, _attn_bias(rel_bias[i])).reshape(t, A_WIDTH)
        w_up = jnp.pad(w_gla_gate_up[i], ((0, LANES - GATE_RANK), (0, 0))).astype(bf16)
        yb = _gla(proj3, gd.reshape(b, s, LANES), w_up, vec(b_gla_gate[i]), vec(gla_norm[i])).reshape(t, B_V_WIDTH)
        j = i // 2
        routed = i % 2 == 1
        rw = jnp.pad(router_w[j], ((0, 0), (0, LANES - N_EXPERTS))).astype(bf16) if routed else None
        outs = _mix_out(ya, yb, proj, x, w_branch_a[i].astype(bf16), w_branch_b[i].astype(bf16),
                        w_out[i].astype(bf16), vec(norm_mix_post[i]), vec(norm_ffn_pre[i]), rw)
        p_i = p[i].reshape(t, PLE_DIM)
        tail = (vec(norm_ffn_post[i]), ple_w_proj[i].astype(bf16), ple_w_gate[i].astype(bf16), vec(ple_norm[i]))
        if routed:
            x1, h, route = outs
            x = _moe(h, route, x1, p_i, moe_w_gate[j].astype(bf16), moe_w_up[j].astype(bf16),
                     moe_w_down[j].astype(bf16), *tail)
        else:
            x1, h = outs
            x = _dense_ffn(h, x1, p_i, ffn_w_gate[j], ffn_w_up[j], ffn_w_down[j], *tail)
    return x.reshape(b, s, d)
```

```python
import functools

import numpy as np
import jax
import jax.numpy as jnp
from jax import lax
from jax.experimental import pallas as pl
from jax.experimental.pallas import tpu as pltpu

f32 = jnp.float32
bf16 = jnp.bfloat16

D_MODEL = 1024
CHUNK = 64
N_PREV_CHUNKS = 8
A_HEADS = 8
A_HEAD_DIM = 64
A_WIDTH = A_HEADS * A_HEAD_DIM
REL_CLIP = 128
B_HEADS = 4
B_KEY_DIM = 64
B_VAL_DIM = 128
B_K_WIDTH = B_HEADS * B_KEY_DIM
B_V_WIDTH = B_HEADS * B_VAL_DIM
GATE_RANK = 16
GATE_TEMP = 16.0
N_EXPERTS = 8
PLE_DIM = 256
NORM_EPS = 1e-6

LANES = 128
VMEM_LIMIT = 56 * 1024 * 1024
VMEM_LIMIT_BIG = 62 * 1024 * 1024

SRC_QA, SRC_KA, SRC_VA, SRC_QB, SRC_GD, SRC_R = 0, 512, 1024, 1536, 2560, 2576
SRC_END = 5136
COL_KA, COL_QB, COL_KB, COL_VB, COL_R, COL_GA, COL_GB = 0, 512, 768, 1024, 1536, 2048, 3072
MAIN_WIDTH = 4096
IN_TILE_N = 2048

QBLK = 2 * CHUNK
KBAND = (N_PREV_CHUNKS + 2) * CHUNK
KPAD = N_PREV_CHUNKS * CHUNK

SUBLANES = 8
ROW_TILE = D_MODEL // LANES
assert ROW_TILE == SUBLANES
MOE_TM = 512
ROW_BLK = 512
FF_CHUNK = 512
ISSUE_UNROLL = 8
GLA_NB = 2


def _params(*sem):
    return pltpu.CompilerParams(dimension_semantics=sem, vmem_limit_bytes=VMEM_LIMIT)


def _rms(x, g):
    return x * lax.rsqrt(jnp.mean(x * x, axis=-1, keepdims=True) + NORM_EPS) * g


def _sigmoid(x):
    return 1.0 / (1.0 + jnp.exp(-x))


def _silu(x):
    return x * _sigmoid(x)


def _chunks(n, c):
    return [(s, min(s + c, n)) for s in range(0, n, c)]


def _store_row_tiles(ref, x):
    rows = x.shape[0]
    for c in range(ROW_TILE):
        ref[pl.ds(c, rows, stride=ROW_TILE), :] = x[:, c * LANES:(c + 1) * LANES].astype(ref.dtype)


def _load_row_tiles(ref, rows):
    return jnp.concatenate([ref[pl.ds(c, rows, stride=ROW_TILE), :] for c in range(ROW_TILE)], axis=1)


def _tile_rows(ref, r):
    return ref.at[pl.ds(pl.multiple_of(r * ROW_TILE, ROW_TILE), ROW_TILE)]


def _const_spec(shape):
    nd = len(shape)
    return pl.BlockSpec(shape, lambda *_: (0,) * nd, pipeline_mode=pl.Buffered(1))


def _qv_proj_kernel(x_ref, g_ref, wq_ref, wv_ref, wgd_ref, h_ref, qt_ref, vt_ref, gd_ref):
    nt = (((1,), (1,)), ((), ()))
    h = _rms(x_ref[...], g_ref[...]).astype(bf16)
    h_ref[...] = h
    qt_ref[0] = lax.dot_general(wq_ref[...], h, nt, preferred_element_type=f32).astype(qt_ref.dtype)
    vt = lax.dot_general(wv_ref[...], h, nt, preferred_element_type=f32).astype(vt_ref.dtype)
    for kt in range(vt_ref.shape[1]):
        vt_ref[0, kt] = vt[:, kt * LANES:(kt + 1) * LANES]
    gd_ref[...] = jnp.dot(h, wgd_ref[...], preferred_element_type=f32).astype(gd_ref.dtype)


def _qv_proj(x, g, wq_t, wv_t, w_gd, b, s, tm=1024):
    t = x.shape[0]
    per_b = s // tm
    return pl.pallas_call(
        _qv_proj_kernel,
        grid=(t // tm,),
        in_specs=[pl.BlockSpec((tm, D_MODEL), lambda i: (i, 0)),
                  _const_spec((1, D_MODEL)),
                  _const_spec((A_WIDTH, D_MODEL)),
                  _const_spec((A_WIDTH, D_MODEL)),
                  _const_spec((D_MODEL, LANES))],
        out_specs=[pl.BlockSpec((tm, D_MODEL), lambda i: (i, 0)),
                   pl.BlockSpec((1, A_WIDTH, tm), lambda i: (i // per_b, 0, i % per_b)),
                   pl.BlockSpec((1, tm // LANES, A_WIDTH, LANES), lambda i: (i // per_b, i % per_b, 0, 0)),
                   pl.BlockSpec((tm, LANES), lambda i: (i, 0))],
        out_shape=[jax.ShapeDtypeStruct((t, D_MODEL), bf16),
                   jax.ShapeDtypeStruct((b, A_WIDTH, s), bf16),
                   jax.ShapeDtypeStruct((b, s // LANES, A_WIDTH, LANES), bf16),
                   jax.ShapeDtypeStruct((t, LANES), bf16)],
        compiler_params=_params("parallel"),
        name="qv_proj",
    )(x, g, wq_t, wv_t, w_gd)


def _in_proj_kernel(h_ref, w_ref, o_ref):
    o_ref[...] = jnp.dot(h_ref[...], w_ref[...], preferred_element_type=f32).astype(o_ref.dtype)


def _in_proj(h, w, tm=1024):
    t = h.shape[0]
    n = w.shape[1]
    return pl.pallas_call(
        _in_proj_kernel,
        grid=(t // tm, n // IN_TILE_N),
        in_specs=[pl.BlockSpec((tm, D_MODEL), lambda i, j: (i, 0)),
                  pl.BlockSpec((D_MODEL, IN_TILE_N), lambda i, j: (0, j))],
        out_specs=pl.BlockSpec((tm, IN_TILE_N), lambda i, j: (i, j)),
        out_shape=jax.ShapeDtypeStruct((t, n), bf16),
        compiler_params=_params("parallel", "arbitrary"),
        name="in_proj",
    )(h, w)


def _attn_kernel(qt_ref, k_ref, vt_ref, bias_ref, o_ref):
    i = pl.program_id(1)
    row = lax.broadcasted_iota(jnp.int32, (LANES, QBLK), 0)
    upper = row >= A_HEAD_DIM
    n_kt = KBAND // LANES
    kt_seq = [i + kt - KPAD // LANES for kt in range(n_kt)]
    kt_clamped = [jnp.maximum(j, 0) for j in kt_seq]
    for hp in range(A_WIDTH // LANES):
        pair = slice(hp * LANES, (hp + 1) * LANES)
        qt = qt_ref[0, pair, :] * (A_HEAD_DIM ** -0.5)
        kps = [k_ref[0, pl.ds(pl.multiple_of(j * LANES, LANES), LANES), pair] for j in kt_clamped]
        outs = []
        for s in range(2):
            h = 2 * hp + s
            qm = jnp.where(upper == bool(s), qt, jnp.zeros_like(qt))
            tiles = []
            for kt in range(n_kt):
                t = jnp.dot(kps[kt], qm, preferred_element_type=f32) + bias_ref[h, kt]
                tiles.append(jnp.where(kt_seq[kt] >= 0, t, -1e30))
            m = tiles[0]
            for t in tiles[1:]:
                m = jnp.maximum(m, t)
            m = jnp.max(m, axis=0, keepdims=True)
            l = None
            o = None
            for kt in range(n_kt):
                p = jnp.exp(tiles[kt] - m)
                ls = jnp.sum(p, axis=0, keepdims=True)
                vt = vt_ref[0, kt_clamped[kt], h * A_HEAD_DIM:(h + 1) * A_HEAD_DIM, :]
                part = jnp.dot(vt, p.astype(bf16), preferred_element_type=f32)
                l = ls if l is None else l + ls
                o = part if o is None else o + part
            outs.append(o / l)
        o_ref[0, :, pair] = jnp.concatenate(outs, axis=0).T.astype(o_ref.dtype)


def _attention(qt, proj3, vt, bias_t):
    b, s, _ = proj3.shape
    return pl.pallas_call(
        _attn_kernel,
        grid=(b, s // QBLK),
        in_specs=[pl.BlockSpec((1, A_WIDTH, QBLK), lambda bi, i: (bi, 0, i)),
                  pl.BlockSpec((1, s, A_WIDTH), lambda bi, i: (bi, 0, COL_KA // A_WIDTH)),
                  pl.BlockSpec((1, s // LANES, A_WIDTH, LANES), lambda bi, i: (bi, 0, 0, 0)),
                  _const_spec((A_HEADS, KBAND // LANES, LANES, QBLK))],
        out_specs=pl.BlockSpec((1, QBLK, A_WIDTH), lambda bi, i: (bi, i, 0)),
        out_shape=jax.ShapeDtypeStruct((b, s, A_WIDTH), bf16),
        compiler_params=_params("parallel", "arbitrary"),
        name="chunk_attention",
    )(qt, proj3, vt, bias_t)


def _attn_bias(rel_bias):
    h = rel_bias.shape[0]
    r = np.arange(QBLK)[:, None]
    k = np.arange(KBAND)[None, :]
    visible = np.where(r < CHUNK, k < KBAND - CHUNK, k >= CHUNK)
    span = QBLK + KBAND - 1
    n_far = span - 2 * REL_CLIP
    line = jnp.concatenate([jnp.broadcast_to(rel_bias[:, -1:], (h, n_far)), rel_bias[:, :0:-1]], axis=1)
    line = jnp.pad(line.astype(f32), ((0, 0), (0, 1)))
    skew = jnp.tile(line, (1, QBLK))[:, :QBLK * span].reshape(h, QBLK, span)
    bias = jnp.where(visible[None], skew[:, :, QBLK - 1:], -1e30)
    return jnp.swapaxes(bias, 1, 2).reshape(h, KBAND // LANES, LANES, QBLK)


def _gla_kernel(q_ref, k_ref, v_ref, gd_ref, r_ref, wup_ref, b_ref, gn_ref, o_ref, state):
    @pl.when(pl.program_id(1) == 0)
    def _():
        state[...] = jnp.zeros_like(state)

    nb, tq = q_ref.shape[0], q_ref.shape[1]
    tn = (((0,), (0,)), ((), ()))
    nt = (((1,), (1,)), ((), ()))
    row = lax.broadcasted_iota(jnp.int32, (tq, tq), 0)
    col = lax.broadcasted_iota(jnp.int32, (tq, tq), 1)
    later = jnp.logical_and(col > row, col // CHUNK == row // CHUNK)
    later = jnp.where(later, 1.0, 0.0).astype(bf16)
    log_a, kdec = [], []
    for bb in range(nb):
        z = jnp.dot(gd_ref[bb], wup_ref[...], preferred_element_type=f32) + b_ref[...]
        la = (jnp.minimum(z, 0.0) - jnp.log(1.0 + jnp.exp(-jnp.abs(z)))) / GATE_TEMP
        la_hi = la.astype(bf16)
        la_lo = (la - la_hi.astype(f32)).astype(bf16)
        rev = (jnp.dot(later, la_hi, preferred_element_type=f32)
               + jnp.dot(later, la_lo, preferred_element_type=f32))
        log_a.append(la)
        kdec.append((k_ref[bb].astype(f32) * jnp.exp(rev)).astype(bf16))
    hrow = lax.broadcasted_iota(jnp.int32, (B_HEADS * CHUNK, B_K_WIDTH), 0) // CHUNK
    hlane = lax.broadcasted_iota(jnp.int32, (B_HEADS * CHUNK, B_K_WIDTH), 1) // B_KEY_DIM
    own = hrow == hlane
    st = [state[bb] for bb in range(nb)]
    for c in range(tq // CHUNK):
        rows = slice(c * CHUNK, (c + 1) * CHUNK)
        for bb in range(nb):
            decay = jnp.exp(jnp.sum(log_a[bb][rows], axis=0, keepdims=True))
            kd = kdec[bb][rows]
            kd4 = jnp.where(own, jnp.concatenate([kd] * B_HEADS, axis=0), jnp.zeros((), bf16))
            v4 = jnp.concatenate([v_ref[bb, rows, h * B_VAL_DIM:(h + 1) * B_VAL_DIM] for h in range(B_HEADS)],
                                 axis=0)
            d_state = lax.dot_general(v4, kd4, tn, preferred_element_type=f32)
            st[bb] = decay * st[bb] + d_state
            qc = q_ref[bb, rows, :] * (B_KEY_DIM ** -0.5)
            q4 = jnp.where(own, jnp.concatenate([qc] * B_HEADS, axis=0), jnp.zeros((), bf16))
            o4 = lax.dot_general(q4, st[bb].astype(bf16), nt, preferred_element_type=f32)
            for h in range(B_HEADS):
                vcols = slice(h * B_VAL_DIM, (h + 1) * B_VAL_DIM)
                o = _rms(o4[h * CHUNK:(h + 1) * CHUNK], gn_ref[:, vcols])
                r = r_ref[bb, rows, vcols].astype(f32)
                o_ref[bb, rows, vcols] = (o * _silu(r)).astype(o_ref.dtype)
    for bb in range(nb):
        state[bb] = st[bb]


def _gla(proj3, gd3, w_up, b_gate, gla_norm, tq=256, nb=GLA_NB):
    b, s, _ = proj3.shape
    return pl.pallas_call(
        _gla_kernel,
        grid=(b // nb, s // tq),
        in_specs=[pl.BlockSpec((nb, tq, B_K_WIDTH), lambda bi, i: (bi, i, COL_QB // B_K_WIDTH)),
                  pl.BlockSpec((nb, tq, B_K_WIDTH), lambda bi, i: (bi, i, COL_KB // B_K_WIDTH)),
                  pl.BlockSpec((nb, tq, B_V_WIDTH), lambda bi, i: (bi, i, COL_VB // B_V_WIDTH)),
                  pl.BlockSpec((nb, tq, LANES), lambda bi, i: (bi, i, 0)),
                  pl.BlockSpec((nb, tq, B_V_WIDTH), lambda bi, i: (bi, i, COL_R // B_V_WIDTH)),
                  _const_spec((LANES, B_K_WIDTH)),
                  _const_spec((1, B_K_WIDTH)),
                  _const_spec((1, B_V_WIDTH))],
        out_specs=pl.BlockSpec((nb, tq, B_V_WIDTH), lambda bi, i: (bi, i, 0)),
        out_shape=jax.ShapeDtypeStruct((b, s, B_V_WIDTH), bf16),
        scratch_shapes=[pltpu.VMEM((nb, B_VAL_DIM, B_K_WIDTH), f32)],
        compiler_params=_params("parallel", "arbitrary"),
        name="gla",
    )(proj3, proj3, proj3, gd3, proj3, w_up, b_gate, gla_norm)


def _route_top2(h, rw_ref):
    logits = jnp.dot(h.astype(bf16), rw_ref[...], preferred_element_type=f32)
    lane = lax.broadcasted_iota(jnp.int32, logits.shape, 1).astype(f32)
    neg = -jnp.inf
    lg = jnp.where(lane < N_EXPERTS, logits, neg)
    m1 = jnp.max(lg, axis=-1, keepdims=True)
    i1 = jnp.min(jnp.where(lg == m1, lane, float(LANES)), axis=-1, keepdims=True)
    lg2 = jnp.where(lane == i1, neg, lg)
    m2 = jnp.max(lg2, axis=-1, keepdims=True)
    i2 = jnp.min(jnp.where(lg2 == m2, lane, float(LANES)), axis=-1, keepdims=True)
    e2 = jnp.exp(m2 - m1)
    w1 = 1.0 / (1.0 + e2)
    w2 = e2 / (1.0 + e2)
    return jnp.where(lane == 0, i1, jnp.where(lane == 1, i2, jnp.where(lane == 2, w1, jnp.where(lane == 3, w2, 0.0))))


def _mix_out_kernel(*refs, routed):
    if routed:
        (ya_ref, yb_ref, ga_ref, gb_ref, x_ref, wa_ref, wb_ref, wo_ref, gpost_ref, gpre_ref, rw_ref,
         xo_ref, ho_ref, route_ref) = refs
    else:
        (ya_ref, yb_ref, ga_ref, gb_ref, x_ref, wa_ref, wb_ref, wo_ref, gpost_ref, gpre_ref,
         xo_ref, ho_ref) = refs
    a = jnp.dot(ya_ref[...], wa_ref[...], preferred_element_type=f32)
    b = jnp.dot(yb_ref[...], wb_ref[...], preferred_element_type=f32)
    merged = _sigmoid(ga_ref[...].astype(f32)) * a + _sigmoid(gb_ref[...].astype(f32)) * b
    y = jnp.dot(merged.astype(bf16), wo_ref[...], preferred_element_type=f32)
    x1 = x_ref[...] + _rms(y, gpost_ref[...])
    h = _rms(x1, gpre_ref[...])
    xo_ref[...] = x1
    if routed:
        _store_row_tiles(ho_ref, h)
        route_ref[...] = _route_top2(h, rw_ref)
    else:
        ho_ref[...] = h.astype(ho_ref.dtype)


def _mix_out(ya, yb, proj, x, wa, wb, wo, g_post, g_pre, router_w=None, tm=512):
    t = x.shape[0]
    routed = router_w is not None
    row = lambda i: (i, 0)
    in_specs = [pl.BlockSpec((tm, A_WIDTH), row),
                pl.BlockSpec((tm, B_V_WIDTH), row),
                pl.BlockSpec((tm, D_MODEL), lambda i: (i, COL_GA // D_MODEL)),
                pl.BlockSpec((tm, D_MODEL), lambda i: (i, COL_GB // D_MODEL)),
                pl.BlockSpec((tm, D_MODEL), row),
                _const_spec((A_WIDTH, D_MODEL)),
                _const_spec((B_V_WIDTH, D_MODEL)),
                _const_spec((D_MODEL, D_MODEL)),
                _const_spec((1, D_MODEL)),
                _const_spec((1, D_MODEL))]
    args = [ya, yb, proj, proj, x, wa, wb, wo, g_post, g_pre]
    if routed:
        out_specs = [pl.BlockSpec((tm, D_MODEL), row), pl.BlockSpec((tm * ROW_TILE, LANES), row)]
        out_shape = [jax.ShapeDtypeStruct((t, D_MODEL), f32), jax.ShapeDtypeStruct((t * ROW_TILE, LANES), f32)]
    else:
        out_specs = [pl.BlockSpec((tm, D_MODEL), row), pl.BlockSpec((tm, D_MODEL), row)]
        out_shape = [jax.ShapeDtypeStruct((t, D_MODEL), f32), jax.ShapeDtypeStruct((t, D_MODEL), bf16)]
    if routed:
        in_specs.append(_const_spec((D_MODEL, LANES)))
        args.append(router_w)
        out_specs.append(pl.BlockSpec((tm, LANES), row))
        out_shape.append(jax.ShapeDtypeStruct((t, LANES), f32))
    return pl.pallas_call(
        functools.partial(_mix_out_kernel, routed=routed),
        grid=(t // tm,),
        in_specs=in_specs,
        out_specs=out_specs,
        out_shape=out_shape,
        compiler_params=_params("parallel"),
        name="mix_out",
    )(*args)


def _post_ffn(y, x1, p, gpost_ref, wpp_ref, wpg_ref, gple_ref):
    x2 = x1 + _rms(y, gpost_ref[...])
    e = jnp.dot(p.astype(bf16), wpp_ref[...], preferred_element_type=f32)
    e = e * _sigmoid(jnp.dot(x2.astype(bf16), wpg_ref[...], preferred_element_type=f32))
    return x2 + _rms(e, gple_ref[...])


def _swiglu_chunks(x, wg_ref, wu_ref, wd_ref, lead, width):
    acc = None
    for c0, c1 in _chunks(width, FF_CHUNK):
        g = jnp.dot(x, wg_ref[lead + (slice(None), slice(c0, c1))], preferred_element_type=f32)
        u = jnp.dot(x, wu_ref[lead + (slice(None), slice(c0, c1))], preferred_element_type=f32)
        act = (_silu(g) * u).astype(bf16)
        part = jnp.dot(act, wd_ref[lead + (slice(c0, c1), slice(None))], preferred_element_type=f32)
        acc = part if acc is None else acc + part
    return acc


def _dense_ffn_kernel(*refs, n_side):
    (h_ref, x_ref, p_ref, wg_ref, wu_ref, wd_ref, gpost_ref, wpp_ref, wpg_ref, gple_ref) = refs[:10]
    side_in = refs[10:10 + n_side]
    o_ref = refs[10 + n_side]
    side_out = refs[11 + n_side:11 + 2 * n_side]
    scratch = refs[11 + 2 * n_side:]
    stage_in, stage_out = scratch[:n_side], scratch[n_side:2 * n_side]
    i = pl.program_id(0)
    n = pl.num_programs(0)
    if n_side:
        sem_in, sem_out = scratch[2 * n_side], scratch[2 * n_side + 1]

        def slab(ref, k, step):
            rows = stage_in[k].shape[0]
            return ref.at[pl.ds(pl.multiple_of(step * rows, rows), rows)]

        def fetch(step):
            for k in range(n_side):
                pltpu.make_async_copy(slab(side_in[k], k, step), stage_in[k], sem_in.at[k]).start()

        @pl.when(i == 0)
        def _():
            fetch(0)

        for k in range(n_side):
            pltpu.make_async_copy(slab(side_in[k], k, i), stage_in[k], sem_in.at[k]).wait()

        @pl.when(i > 0)
        def _():
            for k in range(n_side):
                pltpu.make_async_copy(stage_out[k], slab(side_out[k], k, i - 1), sem_out.at[k]).wait()

        for k in range(n_side):
            stage_out[k][...] = stage_in[k][...].astype(stage_out[k].dtype)
        for k in range(n_side):
            pltpu.make_async_copy(stage_out[k], slab(side_out[k], k, i), sem_out.at[k]).start()

        @pl.when(i + 1 < n)
        def _():
            fetch(i + 1)

    y = _swiglu_chunks(h_ref[...], wg_ref, wu_ref, wd_ref, (), wg_ref.shape[1])
    o_ref[...] = _post_ffn(y, x_ref[...], p_ref[...], gpost_ref, wpp_ref, wpg_ref, gple_ref)

    if n_side:
        @pl.when(i == n - 1)
        def _():
            for k in range(n_side):
                pltpu.make_async_copy(stage_out[k], slab(side_out[k], k, i), sem_out.at[k]).wait()


def _dense_ffn(h, x1, p, wg, wu, wd, g_post, wpp, wpg, g_ple, side=(), tm=512):
    t = x1.shape[0]
    ff = wg.shape[1]
    steps = t // tm
    row = lambda i: (i, 0)
    hbm = pl.BlockSpec(memory_space=pl.ANY)
    n_side = len(side)
    slabs = [(m.shape[0] // steps, m.shape[1]) for m in side]
    assert all(r * steps == m.shape[0] for (r, _), m in zip(slabs, side))
    scratch = ([pltpu.VMEM(sl, f32) for sl in slabs] + [pltpu.VMEM(sl, bf16) for sl in slabs])
    if n_side:
        scratch += [pltpu.SemaphoreType.DMA((n_side,)), pltpu.SemaphoreType.DMA((n_side,))]
    outs = pl.pallas_call(
        functools.partial(_dense_ffn_kernel, n_side=n_side),
        grid=(steps,),
        in_specs=[pl.BlockSpec((tm, D_MODEL), row),
                  pl.BlockSpec((tm, D_MODEL), row),
                  pl.BlockSpec((tm, PLE_DIM), row),
                  _const_spec((D_MODEL, ff)),
                  _const_spec((D_MODEL, ff)),
                  _const_spec((ff, D_MODEL)),
                  _const_spec((1, D_MODEL)),
                  _const_spec((PLE_DIM, D_MODEL)),
                  _const_spec((D_MODEL, D_MODEL)),
                  _const_spec((1, D_MODEL))] + [hbm] * n_side,
        out_specs=[pl.BlockSpec((tm, D_MODEL), row)] + [hbm] * n_side,
        out_shape=[jax.ShapeDtypeStruct((t, D_MODEL), f32)] + [jax.ShapeDtypeStruct(m.shape, bf16) for m in side],
        scratch_shapes=scratch,
        compiler_params=pltpu.CompilerParams(dimension_semantics=("arbitrary",), vmem_limit_bytes=VMEM_LIMIT_BIG),
        name="dense_ffn",
    )(h, x1, p, wg, wu, wd, g_post, wpp, wpg, g_ple, *side)
    return outs[0], tuple(outs[1:])


def _scatter_kernel(fill_ref, pos_ref, h_ref, xs_out, zbuf, sem, zsem):
    rows = h_ref.shape[0] // ROW_TILE

    @pl.when(pl.program_id(0) == 0)
    def _():
        zbuf[...] = jnp.zeros_like(zbuf)

        def zero_rows(first_row):
            start = pl.multiple_of(first_row * ROW_TILE, ROW_TILE)
            fill = pltpu.make_async_copy(zbuf, xs_out.at[pl.ds(start, MOE_TM * ROW_TILE)], zsem)
            fill.start()
            fill.wait()

        for e in range(N_EXPERTS):
            zero_rows(fill_ref[e])

        def zero_tile(tile, c):
            zero_rows(tile * MOE_TM)
            return c

        lax.fori_loop(fill_ref[N_EXPERTS], xs_out.shape[0] // (MOE_TM * ROW_TILE), zero_tile, 0)

    def issue(r, c):
        for k in range(2):
            pltpu.make_async_copy(_tile_rows(h_ref, r), _tile_rows(xs_out, pos_ref[2 * r + k]),
                                  sem.at[k]).start(priority=k)
        return c

    lax.fori_loop(0, rows, issue, 0, unroll=ISSUE_UNROLL)
    for k in range(2):
        pltpu.make_async_copy(h_ref, xs_out.at[pl.ds(0, rows * ROW_TILE)], sem.at[k]).wait()


def _scatter_rows(fill_plan, pos, h, n_rows):
    t = h.shape[0] // ROW_TILE
    return pl.pallas_call(
        _scatter_kernel,
        grid_spec=pltpu.PrefetchScalarGridSpec(
            num_scalar_prefetch=1,
            grid=(t // ROW_BLK,),
            in_specs=[pl.BlockSpec((2 * ROW_BLK,), lambda i, fill: (i,), memory_space=pltpu.SMEM),
                      pl.BlockSpec((ROW_BLK * ROW_TILE, LANES), lambda i, fill: (i, 0))],
            out_specs=pl.BlockSpec(memory_space=pl.ANY),
            scratch_shapes=[pltpu.VMEM((MOE_TM * ROW_TILE, LANES), f32),
                            pltpu.SemaphoreType.DMA((2,)),
                            pltpu.SemaphoreType.DMA(())]),
        out_shape=jax.ShapeDtypeStruct((n_rows * ROW_TILE, LANES), f32),
        compiler_params=_params("arbitrary"),
        name="moe_scatter",
    )(fill_plan, pos, h)


def _expert_kernel(te_ref, nu_ref, x_ref, wg_ref, wu_ref, wd_ref, o_ref, acc_ref):
    del te_ref
    i = pl.program_id(0)
    f = pl.program_id(1)

    @pl.when(i < nu_ref[0])
    def _():
        x = _load_row_tiles(x_ref, MOE_TM).astype(bf16)
        y = _swiglu_chunks(x, wg_ref, wu_ref, wd_ref, (0,), wg_ref.shape[2])

        @pl.when(f == 0)
        def _():
            acc_ref[...] = y

        @pl.when(f > 0)
        def _():
            acc_ref[...] += y

    last = f == pl.num_programs(1) - 1

    @pl.when(jnp.logical_and(last, i < nu_ref[0]))
    def _():
        _store_row_tiles(o_ref, acc_ref[...])

    @pl.when(jnp.logical_and(last, i >= nu_ref[0]))
    def _():
        o_ref[...] = jnp.zeros_like(o_ref)


def _expert_ffn(tile_expert, n_used, xs, wg, wu, wd, tf=1792):
    n_rows = xs.shape[0] // ROW_TILE
    ff = wg.shape[2]
    nf = ff // tf

    def f_idx(i, f, nu):
        return jnp.where(i < nu[0], f, nf - 1)

    return pl.pallas_call(
        _expert_kernel,
        grid_spec=pltpu.PrefetchScalarGridSpec(
            num_scalar_prefetch=2,
            grid=(n_rows // MOE_TM, nf),
            in_specs=[pl.BlockSpec((MOE_TM * ROW_TILE, LANES), lambda i, f, te, nu: (i, 0)),
                      pl.BlockSpec((1, D_MODEL, tf), lambda i, f, te, nu: (te[i], 0, f_idx(i, f, nu))),
                      pl.BlockSpec((1, D_MODEL, tf), lambda i, f, te, nu: (te[i], 0, f_idx(i, f, nu))),
                      pl.BlockSpec((1, tf, D_MODEL), lambda i, f, te, nu: (te[i], f_idx(i, f, nu), 0))],
            out_specs=pl.BlockSpec((MOE_TM * ROW_TILE, LANES), lambda i, f, te, nu: (i, 0)),
            scratch_shapes=[pltpu.VMEM((MOE_TM, D_MODEL), f32)]),
        out_shape=jax.ShapeDtypeStruct((n_rows * ROW_TILE, LANES), f32),
        compiler_params=_params("arbitrary", "arbitrary"),
        name="moe_experts",
    )(tile_expert, n_used, xs, wg, wu, wd)


def _combine_kernel(pos0_ref, pos_next_ref, ys_hbm, route_ref, x_ref, p_ref, gpost_ref, wpp_ref, wpg_ref, gple_ref,
                    o_ref, buf, sem):
    i = pl.program_id(0)
    rows = x_ref.shape[0]
    slot = i % 2

    def fetch(pos_ref, into):
        def issue(r, c):
            for k in range(2):
                pltpu.make_async_copy(_tile_rows(ys_hbm, pos_ref[2 * r + k]), _tile_rows(buf.at[into, k], r),
                                      sem.at[into, k]).start(priority=k)
            return c

        lax.fori_loop(0, rows, issue, 0, unroll=ISSUE_UNROLL)

    @pl.when(i == 0)
    def _():
        fetch(pos0_ref, 0)

    @pl.when(i + 1 < pl.num_programs(0))
    def _():
        fetch(pos_next_ref, 1 - slot)

    for k in range(2):
        pltpu.make_async_copy(ys_hbm.at[pl.ds(0, rows * ROW_TILE)], buf.at[slot, k], sem.at[slot, k]).wait()
    route = route_ref[...]
    y = (route[:, 2:3] * _load_row_tiles(buf.at[slot, 0], rows)
         + route[:, 3:4] * _load_row_tiles(buf.at[slot, 1], rows))
    o_ref[...] = _post_ffn(y, x_ref[...], p_ref[...], gpost_ref, wpp_ref, wpg_ref, gple_ref)


def _combine(pos, ys, route, x1, p, g_post, wpp, wpg, g_ple):
    t = x1.shape[0]
    n_blk = t // ROW_BLK
    row = lambda i: (i, 0)
    smem = functools.partial(pl.BlockSpec, (2 * ROW_BLK,), memory_space=pltpu.SMEM)
    return pl.pallas_call(
        _combine_kernel,
        grid=(n_blk,),
        in_specs=[smem(lambda i: (0,)),
                  smem(lambda i: (jnp.minimum(i + 1, n_blk - 1),)),
                  pl.BlockSpec(memory_space=pl.ANY),
                  pl.BlockSpec((ROW_BLK, LANES), row),
                  pl.BlockSpec((ROW_BLK, D_MODEL), row),
                  pl.BlockSpec((ROW_BLK, PLE_DIM), row),
                  _const_spec((1, D_MODEL)),
                  _const_spec((PLE_DIM, D_MODEL)),
                  _const_spec((D_MODEL, D_MODEL)),
                  _const_spec((1, D_MODEL))],
        out_specs=pl.BlockSpec((ROW_BLK, D_MODEL), row),
        out_shape=jax.ShapeDtypeStruct((t, D_MODEL), f32),
        scratch_shapes=[pltpu.VMEM((2, 2, ROW_BLK * ROW_TILE, LANES), f32),
                        pltpu.SemaphoreType.DMA((2, 2))],
        compiler_params=_params("arbitrary"),
        name="moe_combine",
    )(pos, pos, ys, route, x1, p, g_post, wpp, wpg, g_ple)


def _route_slots(route, n_tiles):
    ids = route[:, 0:2].astype(jnp.int32).reshape(-1)
    onehot = (ids[None, :] == jnp.arange(N_EXPERTS, dtype=jnp.int32)[:, None]).astype(jnp.int32)
    csum = jnp.cumsum(onehot, axis=1)
    rank = jnp.sum(onehot * csum, axis=0) - 1
    counts = csum[:, -1]
    padded = ((counts + MOE_TM - 1) // MOE_TM) * MOE_TM
    ends = jnp.cumsum(padded)
    pos = (ends - padded)[ids] + rank
    tile_start = jnp.arange(n_tiles, dtype=jnp.int32) * MOE_TM
    tile_expert = jnp.minimum(jnp.sum((tile_start[:, None] >= ends[None, :]).astype(jnp.int32), axis=1),
                              N_EXPERTS - 1)
    n_used = (ends[-1] // MOE_TM).reshape(1)
    fill_plan = jnp.concatenate([ends - padded + counts, n_used])
    return (pos.astype(jnp.int32), tile_expert.astype(jnp.int32), n_used.astype(jnp.int32),
            fill_plan.astype(jnp.int32))


def _moe(h, route, x1, p, wg, wu, wd, g_post, wpp, wpg, g_ple):
    t = x1.shape[0]
    n_tiles = 2 * t // MOE_TM + N_EXPERTS
    pos, tile_expert, n_used, fill_plan = _route_slots(route, n_tiles)
    xs = _scatter_rows(fill_plan, pos, h, n_tiles * MOE_TM)
    ys = _expert_ffn(tile_expert, n_used, xs, wg, wu, wd)
    return _combine(pos, ys, route, x1, p, g_post, wpp, wpg, g_ple)


def _prep_w_in(w):
    main = jnp.concatenate([w[:, SRC_KA:SRC_VA], w[:, SRC_QB:SRC_GD], w[:, SRC_R:SRC_END]], axis=1)
    wq_t = w[:, SRC_QA:SRC_KA].T
    wv_t = w[:, SRC_VA:SRC_QB].T
    w_gd = jnp.pad(w[:, SRC_GD:SRC_R], ((0, 0), (0, LANES - GATE_RANK)))
    return main.astype(bf16), wq_t.astype(bf16), wv_t.astype(bf16), w_gd.astype(bf16)


def kernel(x, p, w_in, rel_bias, w_gla_gate_up, b_gla_gate, gla_norm, w_branch_a, w_branch_b, w_out, norm_mix_pre, norm_mix_post, norm_ffn_pre, norm_ffn_post, ffn_w_gate, ffn_w_up, ffn_w_down, router_w, moe_w_gate, moe_w_up, moe_w_down, ple_w_proj, ple_w_gate, ple_norm):
    b, s, d = x.shape
    t = b * s
    depth = w_in.shape[0]
    x = x.reshape(t, d)
    vec = lambda a: a.reshape(1, -1).astype(f32)
    moe_bf16 = None
    for i in range(depth):
        w_main, wq_t, wv_t, w_gd = _prep_w_in(w_in[i])
        h_in, qt, vt, gd = _qv_proj(x, vec(norm_mix_pre[i]), wq_t, wv_t, w_gd, b, s)
        proj = _in_proj(h_in, w_main)
        proj3 = proj.reshape(b, s, MAIN_WIDTH)
        ya = _attention(qt, proj3, vt, _attn_bias(rel_bias[i])).reshape(t, A_WIDTH)
        w_up = jnp.pad(w_gla_gate_up[i], ((0, LANES - GATE_RANK), (0, 0))).astype(bf16)
        yb = _gla(proj3, gd.reshape(b, s, LANES), w_up, vec(b_gla_gate[i]), vec(gla_norm[i])).reshape(t, B_V_WIDTH)
        j = i // 2
        routed = i % 2 == 1
        rw = jnp.pad(router_w[j], ((0, 0), (0, LANES - N_EXPERTS))).astype(bf16) if routed else None
        outs = _mix_out(ya, yb, proj, x, w_branch_a[i].astype(bf16), w_branch_b[i].astype(bf16),
                        w_out[i].astype(bf16), vec(norm_mix_post[i]), vec(norm_ffn_pre[i]), rw)
        p_i = p[i].reshape(t, PLE_DIM)
        tail = (vec(norm_ffn_post[i]), ple_w_proj[i].astype(bf16), ple_w_gate[i].astype(bf16), vec(ple_norm[i]))
        if routed:
            x1, h, route = outs
            if moe_bf16 is None:
                moe_bf16 = tuple(w[j].astype(bf16) for w in (moe_w_gate, moe_w_up, moe_w_down))
            x = _moe(h, route, x1, p_i, *moe_bf16, *tail)
            moe_bf16 = None
        else:
            x1, h = outs
            side = ()
            if i + 1 < depth:
                side = tuple(w[(i + 1) // 2].reshape(-1, w.shape[-1]) for w in (moe_w_gate, moe_w_up, moe_w_down))
            x, cast = _dense_ffn(h, x1, p_i, ffn_w_gate[j].astype(bf16), ffn_w_up[j].astype(bf16),
                                 ffn_w_down[j].astype(bf16), *tail, side=side)
            if side:
                moe_bf16 = tuple(c.reshape(w.shape[1:]) for c, w in zip(cast, (moe_w_gate, moe_w_up, moe_w_down)))
    return x.reshape(b, s, d)
```

```python
import functools

import numpy as np
import jax
import jax.numpy as jnp
from jax import lax
from jax.experimental import pallas as pl
from jax.experimental.pallas import tpu as pltpu

f32 = jnp.float32
bf16 = jnp.bfloat16

D_MODEL = 1024
CHUNK = 64
N_PREV_CHUNKS = 8
A_HEADS = 8
A_HEAD_DIM = 64
A_WIDTH = A_HEADS * A_HEAD_DIM
REL_CLIP = 128
B_HEADS = 4
B_KEY_DIM = 64
B_VAL_DIM = 128
B_K_WIDTH = B_HEADS * B_KEY_DIM
B_V_WIDTH = B_HEADS * B_VAL_DIM
GATE_RANK = 16
GATE_TEMP = 16.0
N_EXPERTS = 8
PLE_DIM = 256
NORM_EPS = 1e-6

LANES = 128
VMEM_LIMIT = 56 * 1024 * 1024
VMEM_LIMIT_BIG = 62 * 1024 * 1024

SRC_QA, SRC_KA, SRC_VA, SRC_QB, SRC_GD, SRC_R = 0, 512, 1024, 1536, 2560, 2576
SRC_END = 5136
COL_KA, COL_QB, COL_KB, COL_VB, COL_R, COL_GA, COL_GB = 0, 512, 768, 1024, 1536, 2048, 3072
MAIN_WIDTH = 4096
IN_TILE_N = 2048

QBLK = 2 * CHUNK
KBAND = (N_PREV_CHUNKS + 2) * CHUNK
KPAD = N_PREV_CHUNKS * CHUNK

SUBLANES = 8
ROW_TILE = D_MODEL // LANES
assert ROW_TILE == SUBLANES
MOE_TM = 512
ROW_BLK = 512
FF_CHUNK = 512
ISSUE_UNROLL = 8
GLA_NB = 2


def _params(*sem):
    return pltpu.CompilerParams(dimension_semantics=sem, vmem_limit_bytes=VMEM_LIMIT)


def _rms(x, g):
    return x * lax.rsqrt(jnp.mean(x * x, axis=-1, keepdims=True) + NORM_EPS) * g


def _sigmoid(x):
    return 1.0 / (1.0 + jnp.exp(-x))


def _silu(x):
    return x * _sigmoid(x)


def _chunks(n, c):
    return [(s, min(s + c, n)) for s in range(0, n, c)]


def _store_row_tiles(ref, x):
    rows = x.shape[0]
    for c in range(ROW_TILE):
        ref[pl.ds(c, rows, stride=ROW_TILE), :] = x[:, c * LANES:(c + 1) * LANES].astype(ref.dtype)


def _load_row_tiles(ref, rows):
    return jnp.concatenate([ref[pl.ds(c, rows, stride=ROW_TILE), :] for c in range(ROW_TILE)], axis=1)


def _tile_rows(ref, r):
    return ref.at[pl.ds(pl.multiple_of(r * ROW_TILE, ROW_TILE), ROW_TILE)]


def _const_spec(shape):
    nd = len(shape)
    return pl.BlockSpec(shape, lambda *_: (0,) * nd, pipeline_mode=pl.Buffered(1))


def _qv_proj_kernel(x_ref, g_ref, wq_ref, wv_ref, wgd_ref, h_ref, qt_ref, vt_ref, gd_ref):
    nt = (((1,), (1,)), ((), ()))
    h = _rms(x_ref[...], g_ref[...]).astype(bf16)
    h_ref[...] = h
    qt_ref[0] = lax.dot_general(wq_ref[...], h, nt, preferred_element_type=f32).astype(qt_ref.dtype)
    vt = lax.dot_general(wv_ref[...], h, nt, preferred_element_type=f32).astype(vt_ref.dtype)
    for kt in range(vt_ref.shape[1]):
        vt_ref[0, kt] = vt[:, kt * LANES:(kt + 1) * LANES]
    gd_ref[...] = jnp.dot(h, wgd_ref[...], preferred_element_type=f32).astype(gd_ref.dtype)


def _qv_proj(x, g, wq_t, wv_t, w_gd, b, s, tm=1024):
    t = x.shape[0]
    per_b = s // tm
    return pl.pallas_call(
        _qv_proj_kernel,
        grid=(t // tm,),
        in_specs=[pl.BlockSpec((tm, D_MODEL), lambda i: (i, 0)),
                  _const_spec((1, D_MODEL)),
                  _const_spec((A_WIDTH, D_MODEL)),
                  _const_spec((A_WIDTH, D_MODEL)),
                  _const_spec((D_MODEL, LANES))],
        out_specs=[pl.BlockSpec((tm, D_MODEL), lambda i: (i, 0)),
                   pl.BlockSpec((1, A_WIDTH, tm), lambda i: (i // per_b, 0, i % per_b)),
                   pl.BlockSpec((1, tm // LANES, A_WIDTH, LANES), lambda i: (i // per_b, i % per_b, 0, 0)),
                   pl.BlockSpec((tm, LANES), lambda i: (i, 0))],
        out_shape=[jax.ShapeDtypeStruct((t, D_MODEL), bf16),
                   jax.ShapeDtypeStruct((b, A_WIDTH, s), bf16),
                   jax.ShapeDtypeStruct((b, s // LANES, A_WIDTH, LANES), bf16),
                   jax.ShapeDtypeStruct((t, LANES), bf16)],
        compiler_params=_params("parallel"),
        name="qv_proj",
    )(x, g, wq_t, wv_t, w_gd)


def _in_proj_kernel(h_ref, w_ref, o_ref):
    o_ref[...] = jnp.dot(h_ref[...], w_ref[...], preferred_element_type=f32).astype(o_ref.dtype)


def _in_proj(h, w, tm=1024):
    t = h.shape[0]
    n = w.shape[1]
    return pl.pallas_call(
        _in_proj_kernel,
        grid=(t // tm, n // IN_TILE_N),
        in_specs=[pl.BlockSpec((tm, D_MODEL), lambda i, j: (i, 0)),
                  pl.BlockSpec((D_MODEL, IN_TILE_N), lambda i, j: (0, j))],
        out_specs=pl.BlockSpec((tm, IN_TILE_N), lambda i, j: (i, j)),
        out_shape=jax.ShapeDtypeStruct((t, n), bf16),
        compiler_params=_params("parallel", "arbitrary"),
        name="in_proj",
    )(h, w)


def _attn_kernel(qt_ref, k_ref, vt_ref, bias_ref, o_ref):
    i = pl.program_id(1)
    row = lax.broadcasted_iota(jnp.int32, (LANES, QBLK), 0)
    upper = row >= A_HEAD_DIM
    n_kt = KBAND // LANES
    kt_seq = [i + kt - KPAD // LANES for kt in range(n_kt)]
    kt_clamped = [jnp.maximum(j, 0) for j in kt_seq]
    for hp in range(A_WIDTH // LANES):
        pair = slice(hp * LANES, (hp + 1) * LANES)
        qt = qt_ref[0, pair, :] * (A_HEAD_DIM ** -0.5)
        kps = [k_ref[0, pl.ds(pl.multiple_of(j * LANES, LANES), LANES), pair] for j in kt_clamped]
        outs = []
        for s in range(2):
            h = 2 * hp + s
            qm = jnp.where(upper == bool(s), qt, jnp.zeros_like(qt))
            tiles = []
            for kt in range(n_kt):
                t = jnp.dot(kps[kt], qm, preferred_element_type=f32) + bias_ref[h, kt]
                tiles.append(jnp.where(kt_seq[kt] >= 0, t, -1e30))
            m = tiles[0]
            for t in tiles[1:]:
                m = jnp.maximum(m, t)
            m = jnp.max(m, axis=0, keepdims=True)
            l = None
            o = None
            for kt in range(n_kt):
                p = jnp.exp(tiles[kt] - m)
                ls = jnp.sum(p, axis=0, keepdims=True)
                vt = vt_ref[0, kt_clamped[kt], h * A_HEAD_DIM:(h + 1) * A_HEAD_DIM, :]
                part = jnp.dot(vt, p.astype(bf16), preferred_element_type=f32)
                l = ls if l is None else l + ls
                o = part if o is None else o + part
            outs.append(o / l)
        o_ref[0, :, pair] = jnp.concatenate(outs, axis=0).T.astype(o_ref.dtype)


def _attention(qt, proj3, vt, bias_t):
    b, s, _ = proj3.shape
    return pl.pallas_call(
        _attn_kernel,
        grid=(b, s // QBLK),
        in_specs=[pl.BlockSpec((1, A_WIDTH, QBLK), lambda bi, i: (bi, 0, i)),
                  pl.BlockSpec((1, s, A_WIDTH), lambda bi, i: (bi, 0, COL_KA // A_WIDTH)),
                  pl.BlockSpec((1, s // LANES, A_WIDTH, LANES), lambda bi, i: (bi, 0, 0, 0)),
                  _const_spec((A_HEADS, KBAND // LANES, LANES, QBLK))],
        out_specs=pl.BlockSpec((1, QBLK, A_WIDTH), lambda bi, i: (bi, i, 0)),
        out_shape=jax.ShapeDtypeStruct((b, s, A_WIDTH), bf16),
        compiler_params=_params("parallel", "arbitrary"),
        name="chunk_attention",
    )(qt, proj3, vt, bias_t)


def _attn_bias(rel_bias):
    h = rel_bias.shape[0]
    r = np.arange(QBLK)[:, None]
    k = np.arange(KBAND)[None, :]
    visible = np.where(r < CHUNK, k < KBAND - CHUNK, k >= CHUNK)
    span = QBLK + KBAND - 1
    n_far = span - 2 * REL_CLIP
    line = jnp.concatenate([jnp.broadcast_to(rel_bias[:, -1:], (h, n_far)), rel_bias[:, :0:-1]], axis=1)
    line = jnp.pad(line.astype(f32), ((0, 0), (0, 1)))
    skew = jnp.tile(line, (1, QBLK))[:, :QBLK * span].reshape(h, QBLK, span)
    bias = jnp.where(visible[None], skew[:, :, QBLK - 1:], -1e30)
    return jnp.swapaxes(bias, 1, 2).reshape(h, KBAND // LANES, LANES, QBLK)


def _gla_kernel(q_ref, k_ref, v_ref, gd_ref, r_ref, wup_ref, b_ref, gn_ref, o_ref, state):
    @pl.when(pl.program_id(1) == 0)
    def _():
        state[...] = jnp.zeros_like(state)

    nb, tq = q_ref.shape[0], q_ref.shape[1]
    tn = (((0,), (0,)), ((), ()))
    nt = (((1,), (1,)), ((), ()))
    row = lax.broadcasted_iota(jnp.int32, (tq, tq), 0)
    col = lax.broadcasted_iota(jnp.int32, (tq, tq), 1)
    later = jnp.logical_and(col > row, col // CHUNK == row // CHUNK)
    later = jnp.where(later, 1.0, 0.0).astype(bf16)
    log_a, kdec = [], []
    for bb in range(nb):
        z = jnp.dot(gd_ref[bb], wup_ref[...], preferred_element_type=f32) + b_ref[...]
        la = (jnp.minimum(z, 0.0) - jnp.log(1.0 + jnp.exp(-jnp.abs(z)))) / GATE_TEMP
        la_hi = la.astype(bf16)
        la_lo = (la - la_hi.astype(f32)).astype(bf16)
        rev = (jnp.dot(later, la_hi, preferred_element_type=f32)
               + jnp.dot(later, la_lo, preferred_element_type=f32))
        log_a.append(la)
        kdec.append((k_ref[bb].astype(f32) * jnp.exp(rev)).astype(bf16))
    hrow = lax.broadcasted_iota(jnp.int32, (B_HEADS * CHUNK, B_K_WIDTH), 0) // CHUNK
    hlane = lax.broadcasted_iota(jnp.int32, (B_HEADS * CHUNK, B_K_WIDTH), 1) // B_KEY_DIM
    own = hrow == hlane
    st = [state[bb] for bb in range(nb)]
    for c in range(tq // CHUNK):
        rows = slice(c * CHUNK, (c + 1) * CHUNK)
        for bb in range(nb):
            decay = jnp.exp(jnp.sum(log_a[bb][rows], axis=0, keepdims=True))
            kd = kdec[bb][rows]
            kd4 = jnp.where(own, jnp.concatenate([kd] * B_HEADS, axis=0), jnp.zeros((), bf16))
            v4 = jnp.concatenate([v_ref[bb, rows, h * B_VAL_DIM:(h + 1) * B_VAL_DIM] for h in range(B_HEADS)],
                                 axis=0)
            d_state = lax.dot_general(v4, kd4, tn, preferred_element_type=f32)
            st[bb] = decay * st[bb] + d_state
            qc = q_ref[bb, rows, :] * (B_KEY_DIM ** -0.5)
            q4 = jnp.where(own, jnp.concatenate([qc] * B_HEADS, axis=0), jnp.zeros((), bf16))
            o4 = lax.dot_general(q4, st[bb].astype(bf16), nt, preferred_element_type=f32)
            for h in range(B_HEADS):
                vcols = slice(h * B_VAL_DIM, (h + 1) * B_VAL_DIM)
                o = _rms(o4[h * CHUNK:(h + 1) * CHUNK], gn_ref[:, vcols])
                r = r_ref[bb, rows, vcols].astype(f32)
                o_ref[bb, rows, vcols] = (o * _silu(r)).astype(o_ref.dtype)
    for bb in range(nb):
        state[bb] = st[bb]


def _gla(proj3, gd3, w_up, b_gate, gla_norm, tq=256, nb=GLA_NB):
    b, s, _ = proj3.shape
    return pl.pallas_call(
        _gla_kernel,
        grid=(b // nb, s // tq),
        in_specs=[pl.BlockSpec((nb, tq, B_K_WIDTH), lambda bi, i: (bi, i, COL_QB // B_K_WIDTH)),
                  pl.BlockSpec((nb, tq, B_K_WIDTH), lambda bi, i: (bi, i, COL_KB // B_K_WIDTH)),
                  pl.BlockSpec((nb, tq, B_V_WIDTH), lambda bi, i: (bi, i, COL_VB // B_V_WIDTH)),
                  pl.BlockSpec((nb, tq, LANES), lambda bi, i: (bi, i, 0)),
                  pl.BlockSpec((nb, tq, B_V_WIDTH), lambda bi, i: (bi, i, COL_R // B_V_WIDTH)),
                  _const_spec((LANES, B_K_WIDTH)),
                  _const_spec((1, B_K_WIDTH)),
                  _const_spec((1, B_V_WIDTH))],
        out_specs=pl.BlockSpec((nb, tq, B_V_WIDTH), lambda bi, i: (bi, i, 0)),
        out_shape=jax.ShapeDtypeStruct((b, s, B_V_WIDTH), bf16),
        scratch_shapes=[pltpu.VMEM((nb, B_VAL_DIM, B_K_WIDTH), f32)],
        compiler_params=_params("parallel", "arbitrary"),
        name="gla",
    )(proj3, proj3, proj3, gd3, proj3, w_up, b_gate, gla_norm)


def _route_top2(h, rw_ref):
    logits = jnp.dot(h.astype(bf16), rw_ref[...], preferred_element_type=f32)
    lane = lax.broadcasted_iota(jnp.int32, logits.shape, 1).astype(f32)
    neg = -jnp.inf
    lg = jnp.where(lane < N_EXPERTS, logits, neg)
    m1 = jnp.max(lg, axis=-1, keepdims=True)
    i1 = jnp.min(jnp.where(lg == m1, lane, float(LANES)), axis=-1, keepdims=True)
    lg2 = jnp.where(lane == i1, neg, lg)
    m2 = jnp.max(lg2, axis=-1, keepdims=True)
    i2 = jnp.min(jnp.where(lg2 == m2, lane, float(LANES)), axis=-1, keepdims=True)
    e2 = jnp.exp(m2 - m1)
    w1 = 1.0 / (1.0 + e2)
    w2 = e2 / (1.0 + e2)
    return jnp.where(lane == 0, i1, jnp.where(lane == 1, i2, jnp.where(lane == 2, w1, jnp.where(lane == 3, w2, 0.0))))


def _mix_out_kernel(*refs, routed):
    if routed:
        (ya_ref, yb_ref, ga_ref, gb_ref, x_ref, wa_ref, wb_ref, wo_ref, gpost_ref, gpre_ref, rw_ref,
         xo_ref, ho_ref, route_ref) = refs
    else:
        (ya_ref, yb_ref, ga_ref, gb_ref, x_ref, wa_ref, wb_ref, wo_ref, gpost_ref, gpre_ref,
         xo_ref, ho_ref) = refs
    a = jnp.dot(ya_ref[...], wa_ref[...], preferred_element_type=f32)
    b = jnp.dot(yb_ref[...], wb_ref[...], preferred_element_type=f32)
    merged = _sigmoid(ga_ref[...].astype(f32)) * a + _sigmoid(gb_ref[...].astype(f32)) * b
    y = jnp.dot(merged.astype(bf16), wo_ref[...], preferred_element_type=f32)
    x1 = x_ref[...] + _rms(y, gpost_ref[...])
    h = _rms(x1, gpre_ref[...])
    xo_ref[...] = x1
    if routed:
        _store_row_tiles(ho_ref, h)
        route_ref[...] = _route_top2(h, rw_ref)
    else:
        ho_ref[...] = h.astype(ho_ref.dtype)


def _mix_out(ya, yb, proj, x, wa, wb, wo, g_post, g_pre, router_w=None, tm=512):
    t = x.shape[0]
    routed = router_w is not None
    row = lambda i: (i, 0)
    in_specs = [pl.BlockSpec((tm, A_WIDTH), row),
                pl.BlockSpec((tm, B_V_WIDTH), row),
                pl.BlockSpec((tm, D_MODEL), lambda i: (i, COL_GA // D_MODEL)),
                pl.BlockSpec((tm, D_MODEL), lambda i: (i, COL_GB // D_MODEL)),
                pl.BlockSpec((tm, D_MODEL), row),
                _const_spec((A_WIDTH, D_MODEL)),
                _const_spec((B_V_WIDTH, D_MODEL)),
                _const_spec((D_MODEL, D_MODEL)),
                _const_spec((1, D_MODEL)),
                _const_spec((1, D_MODEL))]
    args = [ya, yb, proj, proj, x, wa, wb, wo, g_post, g_pre]
    if routed:
        out_specs = [pl.BlockSpec((tm, D_MODEL), row), pl.BlockSpec((tm * ROW_TILE, LANES), row)]
        out_shape = [jax.ShapeDtypeStruct((t, D_MODEL), f32), jax.ShapeDtypeStruct((t * ROW_TILE, LANES), f32)]
    else:
        out_specs = [pl.BlockSpec((tm, D_MODEL), row), pl.BlockSpec((tm, D_MODEL), row)]
        out_shape = [jax.ShapeDtypeStruct((t, D_MODEL), f32), jax.ShapeDtypeStruct((t, D_MODEL), bf16)]
    if routed:
        in_specs.append(_const_spec((D_MODEL, LANES)))
        args.append(router_w)
        out_specs.append(pl.BlockSpec((tm, LANES), row))
        out_shape.append(jax.ShapeDtypeStruct((t, LANES), f32))
    return pl.pallas_call(
        functools.partial(_mix_out_kernel, routed=routed),
        grid=(t // tm,),
        in_specs=in_specs,
        out_specs=out_specs,
        out_shape=out_shape,
        compiler_params=_params("parallel"),
        name="mix_out",
    )(*args)


def _post_ffn(y, x1, p, gpost_ref, wpp_ref, wpg_ref, gple_ref):
    x2 = x1 + _rms(y, gpost_ref[...])
    e = jnp.dot(p.astype(bf16), wpp_ref[...], preferred_element_type=f32)
    e = e * _sigmoid(jnp.dot(x2.astype(bf16), wpg_ref[...], preferred_element_type=f32))
    return x2 + _rms(e, gple_ref[...])


def _swiglu_chunks(x, wg_ref, wu_ref, wd_ref, lead, width):
    acc = None
    for c0, c1 in _chunks(width, FF_CHUNK):
        g = jnp.dot(x, wg_ref[lead + (slice(None), slice(c0, c1))], preferred_element_type=f32)
        u = jnp.dot(x, wu_ref[lead + (slice(None), slice(c0, c1))], preferred_element_type=f32)
        act = (_silu(g) * u).astype(bf16)
        part = jnp.dot(act, wd_ref[lead + (slice(c0, c1), slice(None))], preferred_element_type=f32)
        acc = part if acc is None else acc + part
    return acc


def _dense_ffn_kernel(*refs, n_side):
    (h_ref, x_ref, p_ref, wg_ref, wu_ref, wd_ref, gpost_ref, wpp_ref, wpg_ref, gple_ref) = refs[:10]
    side_in = refs[10:10 + n_side]
    o_ref = refs[10 + n_side]
    side_out = refs[11 + n_side:11 + 2 * n_side]
    scratch = refs[11 + 2 * n_side:]
    stage_in, stage_out = scratch[:n_side], scratch[n_side:2 * n_side]
    i = pl.program_id(0)
    n = pl.num_programs(0)
    if n_side:
        sem_in, sem_out = scratch[2 * n_side], scratch[2 * n_side + 1]
        slot = i % 2

        def slab(ref, k, step):
            rows = stage_out[k].shape[0]
            return ref.at[pl.ds(pl.multiple_of(step * rows, rows), rows)]

        def fetch(step, into):
            for k in range(n_side):
                pltpu.make_async_copy(slab(side_in[k], k, step), stage_in[k].at[into], sem_in.at[k, into]).start()

        @pl.when(i == 0)
        def _():
            fetch(0, 0)

        @pl.when(i + 1 < n)
        def _():
            fetch(i + 1, 1 - slot)

        for k in range(n_side):
            pltpu.make_async_copy(slab(side_in[k], k, i), stage_in[k].at[slot], sem_in.at[k, slot]).wait()

        @pl.when(i > 0)
        def _():
            for k in range(n_side):
                pltpu.make_async_copy(stage_out[k], slab(side_out[k], k, i - 1), sem_out.at[k]).wait()

        for k in range(n_side):
            stage_out[k][...] = stage_in[k][slot].astype(stage_out[k].dtype)
        for k in range(n_side):
            pltpu.make_async_copy(stage_out[k], slab(side_out[k], k, i), sem_out.at[k]).start()

    y = _swiglu_chunks(h_ref[...], wg_ref, wu_ref, wd_ref, (), wg_ref.shape[1])
    o_ref[...] = _post_ffn(y, x_ref[...], p_ref[...], gpost_ref, wpp_ref, wpg_ref, gple_ref)

    if n_side:
        @pl.when(i == n - 1)
        def _():
            for k in range(n_side):
                pltpu.make_async_copy(stage_out[k], slab(side_out[k], k, i), sem_out.at[k]).wait()


def _dense_ffn(h, x1, p, wg, wu, wd, g_post, wpp, wpg, g_ple, side=(), tm=512):
    t = x1.shape[0]
    ff = wg.shape[1]
    steps = t // tm
    row = lambda i: (i, 0)
    hbm = pl.BlockSpec(memory_space=pl.ANY)
    n_side = len(side)
    slabs = [(m.shape[0] // steps, m.shape[1]) for m in side]
    assert all(r * steps == m.shape[0] for (r, _), m in zip(slabs, side))
    scratch = ([pltpu.VMEM((2,) + sl, f32) for sl in slabs] + [pltpu.VMEM(sl, bf16) for sl in slabs])
    if n_side:
        scratch += [pltpu.SemaphoreType.DMA((n_side, 2)), pltpu.SemaphoreType.DMA((n_side,))]
    outs = pl.pallas_call(
        functools.partial(_dense_ffn_kernel, n_side=n_side),
        grid=(steps,),
        in_specs=[pl.BlockSpec((tm, D_MODEL), row),
                  pl.BlockSpec((tm, D_MODEL), row),
                  pl.BlockSpec((tm, PLE_DIM), row),
                  _const_spec((D_MODEL, ff)),
                  _const_spec((D_MODEL, ff)),
                  _const_spec((ff, D_MODEL)),
                  _const_spec((1, D_MODEL)),
                  _const_spec((PLE_DIM, D_MODEL)),
                  _const_spec((D_MODEL, D_MODEL)),
                  _const_spec((1, D_MODEL))] + [hbm] * n_side,
        out_specs=[pl.BlockSpec((tm, D_MODEL), row)] + [hbm] * n_side,
        out_shape=[jax.ShapeDtypeStruct((t, D_MODEL), f32)] + [jax.ShapeDtypeStruct(m.shape, bf16) for m in side],
        scratch_shapes=scratch,
        compiler_params=pltpu.CompilerParams(dimension_semantics=("arbitrary",), vmem_limit_bytes=VMEM_LIMIT_BIG),
        name="dense_ffn",
    )(h, x1, p, wg, wu, wd, g_post, wpp, wpg, g_ple, *side)
    return outs[0], tuple(outs[1:])


def _scatter_kernel(fill_ref, pos_ref, h_ref, xs_out, zbuf, sem, zsem):
    rows = h_ref.shape[0] // ROW_TILE

    @pl.when(pl.program_id(0) == 0)
    def _():
        zbuf[...] = jnp.zeros_like(zbuf)

        def zero_rows(first_row):
            start = pl.multiple_of(first_row * ROW_TILE, ROW_TILE)
            fill = pltpu.make_async_copy(zbuf, xs_out.at[pl.ds(start, MOE_TM * ROW_TILE)], zsem)
            fill.start()
            fill.wait()

        for e in range(N_EXPERTS):
            zero_rows(fill_ref[e])

        def zero_tile(tile, c):
            zero_rows(tile * MOE_TM)
            return c

        lax.fori_loop(fill_ref[N_EXPERTS], xs_out.shape[0] // (MOE_TM * ROW_TILE), zero_tile, 0)

    def issue(r, c):
        for k in range(2):
            pltpu.make_async_copy(_tile_rows(h_ref, r), _tile_rows(xs_out, pos_ref[2 * r + k]),
                                  sem.at[k]).start(priority=k)
        return c

    lax.fori_loop(0, rows, issue, 0, unroll=ISSUE_UNROLL)
    for k in range(2):
        pltpu.make_async_copy(h_ref, xs_out.at[pl.ds(0, rows * ROW_TILE)], sem.at[k]).wait()


def _scatter_rows(fill_plan, pos, h, n_rows):
    t = h.shape[0] // ROW_TILE
    return pl.pallas_call(
        _scatter_kernel,
        grid_spec=pltpu.PrefetchScalarGridSpec(
            num_scalar_prefetch=1,
            grid=(t // ROW_BLK,),
            in_specs=[pl.BlockSpec((2 * ROW_BLK,), lambda i, fill: (i,), memory_space=pltpu.SMEM),
                      pl.BlockSpec((ROW_BLK * ROW_TILE, LANES), lambda i, fill: (i, 0))],
            out_specs=pl.BlockSpec(memory_space=pl.ANY),
            scratch_shapes=[pltpu.VMEM((MOE_TM * ROW_TILE, LANES), f32),
                            pltpu.SemaphoreType.DMA((2,)),
                            pltpu.SemaphoreType.DMA(())]),
        out_shape=jax.ShapeDtypeStruct((n_rows * ROW_TILE, LANES), f32),
        compiler_params=_params("arbitrary"),
        name="moe_scatter",
    )(fill_plan, pos, h)


def _expert_kernel(te_ref, nu_ref, x_ref, wg_ref, wu_ref, wd_ref, o_ref, acc_ref):
    del te_ref
    i = pl.program_id(0)
    f = pl.program_id(1)

    @pl.when(i < nu_ref[0])
    def _():
        x = _load_row_tiles(x_ref, MOE_TM).astype(bf16)
        y = _swiglu_chunks(x, wg_ref, wu_ref, wd_ref, (0,), wg_ref.shape[2])

        @pl.when(f == 0)
        def _():
            acc_ref[...] = y

        @pl.when(f > 0)
        def _():
            acc_ref[...] += y

    last = f == pl.num_programs(1) - 1

    @pl.when(jnp.logical_and(last, i < nu_ref[0]))
    def _():
        _store_row_tiles(o_ref, acc_ref[...])

    @pl.when(jnp.logical_and(last, i >= nu_ref[0]))
    def _():
        o_ref[...] = jnp.zeros_like(o_ref)


def _expert_ffn(tile_expert, n_used, xs, wg, wu, wd, tf=1792):
    n_rows = xs.shape[0] // ROW_TILE
    ff = wg.shape[2]
    nf = ff // tf

    def f_idx(i, f, nu):
        return jnp.where(i < nu[0], f, nf - 1)

    return pl.pallas_call(
        _expert_kernel,
        grid_spec=pltpu.PrefetchScalarGridSpec(
            num_scalar_prefetch=2,
            grid=(n_rows // MOE_TM, nf),
            in_specs=[pl.BlockSpec((MOE_TM * ROW_TILE, LANES), lambda i, f, te, nu: (i, 0)),
                      pl.BlockSpec((1, D_MODEL, tf), lambda i, f, te, nu: (te[i], 0, f_idx(i, f, nu))),
                      pl.BlockSpec((1, D_MODEL, tf), lambda i, f, te, nu: (te[i], 0, f_idx(i, f, nu))),
                      pl.BlockSpec((1, tf, D_MODEL), lambda i, f, te, nu: (te[i], f_idx(i, f, nu), 0))],
            out_specs=pl.BlockSpec((MOE_TM * ROW_TILE, LANES), lambda i, f, te, nu: (i, 0)),
            scratch_shapes=[pltpu.VMEM((MOE_TM, D_MODEL), f32)]),
        out_shape=jax.ShapeDtypeStruct((n_rows * ROW_TILE, LANES), f32),
        compiler_params=_params("arbitrary", "arbitrary"),
        name="moe_experts",
    )(tile_expert, n_used, xs, wg, wu, wd)


def _combine_kernel(pos0_ref, pos_next_ref, ys_hbm, route_ref, x_ref, p_ref, gpost_ref, wpp_ref, wpg_ref, gple_ref,
                    o_ref, buf, sem):
    i = pl.program_id(0)
    rows = x_ref.shape[0]
    slot = i % 2

    def fetch(pos_ref, into):
        def issue(r, c):
            for k in range(2):
                pltpu.make_async_copy(_tile_rows(ys_hbm, pos_ref[2 * r + k]), _tile_rows(buf.at[into, k], r),
                                      sem.at[into, k]).start(priority=k)
            return c

        lax.fori_loop(0, rows, issue, 0, unroll=ISSUE_UNROLL)

    @pl.when(i == 0)
    def _():
        fetch(pos0_ref, 0)

    @pl.when(i + 1 < pl.num_programs(0))
    def _():
        fetch(pos_next_ref, 1 - slot)

    for k in range(2):
        pltpu.make_async_copy(ys_hbm.at[pl.ds(0, rows * ROW_TILE)], buf.at[slot, k], sem.at[slot, k]).wait()
    route = route_ref[...]
    y = (route[:, 2:3] * _load_row_tiles(buf.at[slot, 0], rows)
         + route[:, 3:4] * _load_row_tiles(buf.at[slot, 1], rows))
    o_ref[...] = _post_ffn(y, x_ref[...], p_ref[...], gpost_ref, wpp_ref, wpg_ref, gple_ref)


def _combine(pos, ys, route, x1, p, g_post, wpp, wpg, g_ple):
    t = x1.shape[0]
    n_blk = t // ROW_BLK
    row = lambda i: (i, 0)
    smem = functools.partial(pl.BlockSpec, (2 * ROW_BLK,), memory_space=pltpu.SMEM)
    return pl.pallas_call(
        _combine_kernel,
        grid=(n_blk,),
        in_specs=[smem(lambda i: (0,)),
                  smem(lambda i: (jnp.minimum(i + 1, n_blk - 1),)),
                  pl.BlockSpec(memory_space=pl.ANY),
                  pl.BlockSpec((ROW_BLK, LANES), row),
                  pl.BlockSpec((ROW_BLK, D_MODEL), row),
                  pl.BlockSpec((ROW_BLK, PLE_DIM), row),
                  _const_spec((1, D_MODEL)),
                  _const_spec((PLE_DIM, D_MODEL)),
                  _const_spec((D_MODEL, D_MODEL)),
                  _const_spec((1, D_MODEL))],
        out_specs=pl.BlockSpec((ROW_BLK, D_MODEL), row),
        out_shape=jax.ShapeDtypeStruct((t, D_MODEL), f32),
        scratch_shapes=[pltpu.VMEM((2, 2, ROW_BLK * ROW_TILE, LANES), f32),
                        pltpu.SemaphoreType.DMA((2, 2))],
        compiler_params=_params("arbitrary"),
        name="moe_combine",
    )(pos, pos, ys, route, x1, p, g_post, wpp, wpg, g_ple)


def _route_slots(route, n_tiles):
    ids = route[:, 0:2].astype(jnp.int32).reshape(-1)
    onehot = (ids[None, :] == jnp.arange(N_EXPERTS, dtype=jnp.int32)[:, None]).astype(jnp.int32)
    csum = jnp.cumsum(onehot, axis=1)
    rank = jnp.sum(onehot * csum, axis=0) - 1
    counts = csum[:, -1]
    padded = ((counts + MOE_TM - 1) // MOE_TM) * MOE_TM
    ends = jnp.cumsum(padded)
    pos = (ends - padded)[ids] + rank
    tile_start = jnp.arange(n_tiles, dtype=jnp.int32) * MOE_TM
    tile_expert = jnp.minimum(jnp.sum((tile_start[:, None] >= ends[None, :]).astype(jnp.int32), axis=1),
                              N_EXPERTS - 1)
    n_used = (ends[-1] // MOE_TM).reshape(1)
    fill_plan = jnp.concatenate([ends - padded + counts, n_used])
    return (pos.astype(jnp.int32), tile_expert.astype(jnp.int32), n_used.astype(jnp.int32),
            fill_plan.astype(jnp.int32))


def _moe(h, route, x1, p, wg, wu, wd, g_post, wpp, wpg, g_ple):
    t = x1.shape[0]
    n_tiles = 2 * t // MOE_TM + N_EXPERTS
    pos, tile_expert, n_used, fill_plan = _route_slots(route, n_tiles)
    xs = _scatter_rows(fill_plan, pos, h, n_tiles * MOE_TM)
    ys = _expert_ffn(tile_expert, n_used, xs, wg, wu, wd)
    return _combine(pos, ys, route, x1, p, g_post, wpp, wpg, g_ple)


def _prep_w_in(w):
    main = jnp.concatenate([w[:, SRC_KA:SRC_VA], w[:, SRC_QB:SRC_GD], w[:, SRC_R:SRC_END]], axis=1)
    wq_t = w[:, SRC_QA:SRC_KA].T
    wv_t = w[:, SRC_VA:SRC_QB].T
    w_gd = jnp.pad(w[:, SRC_GD:SRC_R], ((0, 0), (0, LANES - GATE_RANK)))
    return main.astype(bf16), wq_t.astype(bf16), wv_t.astype(bf16), w_gd.astype(bf16)


def kernel(x, p, w_in, rel_bias, w_gla_gate_up, b_gla_gate, gla_norm, w_branch_a, w_branch_b, w_out, norm_mix_pre, norm_mix_post, norm_ffn_pre, norm_ffn_post, ffn_w_gate, ffn_w_up, ffn_w_down, router_w, moe_w_gate, moe_w_up, moe_w_down, ple_w_proj, ple_w_gate, ple_norm):
    b, s, d = x.shape
    t = b * s
    depth = w_in.shape[0]
    x = x.reshape(t, d)
    vec = lambda a: a.reshape(1, -1).astype(f32)
    moe_bf16 = None
    for i in range(depth):
        w_main, wq_t, wv_t, w_gd = _prep_w_in(w_in[i])
        h_in, qt, vt, gd = _qv_proj(x, vec(norm_mix_pre[i]), wq_t, wv_t, w_gd, b, s)
        proj = _in_proj(h_in, w_main)
        proj3 = proj.reshape(b, s, MAIN_WIDTH)
        ya = _attention(qt, proj3, vt, _attn_bias(rel_bias[i])).reshape(t, A_WIDTH)
        w_up = jnp.pad(w_gla_gate_up[i], ((0, LANES - GATE_RANK), (0, 0))).astype(bf16)
        yb = _gla(proj3, gd.reshape(b, s, LANES), w_up, vec(b_gla_gate[i]), vec(gla_norm[i])).reshape(t, B_V_WIDTH)
        j = i // 2
        routed = i % 2 == 1
        rw = jnp.pad(router_w[j], ((0, 0), (0, LANES - N_EXPERTS))).astype(bf16) if routed else None
        outs = _mix_out(ya, yb, proj, x, w_branch_a[i].astype(bf16), w_branch_b[i].astype(bf16),
                        w_out[i].astype(bf16), vec(norm_mix_post[i]), vec(norm_ffn_pre[i]), rw)
        p_i = p[i].reshape(t, PLE_DIM)
        tail = (vec(norm_ffn_post[i]), ple_w_proj[i].astype(bf16), ple_w_gate[i].astype(bf16), vec(ple_norm[i]))
        if routed:
            x1, h, route = outs
            if moe_bf16 is None:
                moe_bf16 = tuple(w[j].astype(bf16) for w in (moe_w_gate, moe_w_up, moe_w_down))
            x = _moe(h, route, x1, p_i, *moe_bf16, *tail)
            moe_bf16 = None
        else:
            x1, h = outs
            side = ()
            if i + 1 < depth:
                side = tuple(w[(i + 1) // 2].reshape(-1, w.shape[-1]) for w in (moe_w_gate, moe_w_up))
            x, cast = _dense_ffn(h, x1, p_i, ffn_w_gate[j].astype(bf16), ffn_w_up[j].astype(bf16),
                                 ffn_w_down[j].astype(bf16), *tail, side=side)
            if side:
                moe_bf16 = tuple(c.reshape(w.shape[1:]) for c, w in zip(cast, (moe_w_gate, moe_w_up)))
                moe_bf16 += (moe_w_down[(i + 1) // 2].astype(bf16),)
    return x.reshape(b, s, d)
```

```python
import functools

import numpy as np
import jax
import jax.numpy as jnp
from jax import lax
from jax.experimental import pallas as pl
from jax.experimental.pallas import tpu as pltpu

f32 = jnp.float32
bf16 = jnp.bfloat16

D_MODEL = 1024
CHUNK = 64
N_PREV_CHUNKS = 8
A_HEADS = 8
A_HEAD_DIM = 64
A_WIDTH = A_HEADS * A_HEAD_DIM
REL_CLIP = 128
B_HEADS = 4
B_KEY_DIM = 64
B_VAL_DIM = 128
B_K_WIDTH = B_HEADS * B_KEY_DIM
B_V_WIDTH = B_HEADS * B_VAL_DIM
GATE_RANK = 16
GATE_TEMP = 16.0
N_EXPERTS = 8
PLE_DIM = 256
NORM_EPS = 1e-6

LANES = 128
VMEM_LIMIT = 56 * 1024 * 1024
VMEM_LIMIT_BIG = 62 * 1024 * 1024

SRC_QA, SRC_KA, SRC_VA, SRC_QB, SRC_GD, SRC_R = 0, 512, 1024, 1536, 2560, 2576
SRC_END = 5136
COL_KA, COL_QB, COL_KB, COL_VB, COL_R, COL_GA, COL_GB = 0, 512, 768, 1024, 1536, 2048, 3072
MAIN_WIDTH = 4096
IN_TILE_N = 2048

QBLK = 2 * CHUNK
KBAND = (N_PREV_CHUNKS + 2) * CHUNK
KPAD = N_PREV_CHUNKS * CHUNK

SUBLANES = 8
ROW_TILE = D_MODEL // LANES
assert ROW_TILE == SUBLANES
MOE_TM = 512
ROW_BLK = 512
FF_CHUNK = 512
ISSUE_UNROLL = 8
GLA_NB = 2


def _params(*sem):
    return pltpu.CompilerParams(dimension_semantics=sem, vmem_limit_bytes=VMEM_LIMIT)


def _rms(x, g):
    return x * lax.rsqrt(jnp.mean(x * x, axis=-1, keepdims=True) + NORM_EPS) * g


def _sigmoid(x):
    return 1.0 / (1.0 + jnp.exp(-x))


def _silu(x):
    return x * _sigmoid(x)


def _chunks(n, c):
    return [(s, min(s + c, n)) for s in range(0, n, c)]


def _store_row_tiles(ref, x):
    rows = x.shape[0]
    for c in range(ROW_TILE):
        ref[pl.ds(c, rows, stride=ROW_TILE), :] = x[:, c * LANES:(c + 1) * LANES].astype(ref.dtype)


def _load_row_tiles(ref, rows):
    return jnp.concatenate([ref[pl.ds(c, rows, stride=ROW_TILE), :] for c in range(ROW_TILE)], axis=1)


def _tile_rows(ref, r):
    return ref.at[pl.ds(pl.multiple_of(r * ROW_TILE, ROW_TILE), ROW_TILE)]


def _const_spec(shape):
    nd = len(shape)
    return pl.BlockSpec(shape, lambda *_: (0,) * nd, pipeline_mode=pl.Buffered(1))


def _side_cast_scratch(side, steps):
    slabs = [(m.shape[0] // steps, m.shape[1]) for m in side]
    assert all(r * steps == m.shape[0] for (r, _), m in zip(slabs, side))
    scratch = [pltpu.VMEM((2,) + sl, f32) for sl in slabs] + [pltpu.VMEM(sl, bf16) for sl in slabs]
    if side:
        scratch += [pltpu.SemaphoreType.DMA((len(side), 2)), pltpu.SemaphoreType.DMA((len(side),))]
    return scratch


def _side_cast(step, n_steps, side_in, side_out, scratch):
    n_side = len(side_in)
    if not n_side:
        return lambda: None
    stage_in, stage_out = scratch[:n_side], scratch[n_side:2 * n_side]
    sem_in, sem_out = scratch[2 * n_side], scratch[2 * n_side + 1]
    slot = step % 2

    def slab(ref, k, at):
        rows = stage_out[k].shape[0]
        return ref.at[pl.ds(pl.multiple_of(at * rows, rows), rows)]

    def fetch(at, into):
        for k in range(n_side):
            pltpu.make_async_copy(slab(side_in[k], k, at), stage_in[k].at[into], sem_in.at[k, into]).start()

    @pl.when(step == 0)
    def _():
        fetch(0, 0)

    @pl.when(step + 1 < n_steps)
    def _():
        fetch(step + 1, 1 - slot)

    for k in range(n_side):
        pltpu.make_async_copy(slab(side_in[k], k, step), stage_in[k].at[slot], sem_in.at[k, slot]).wait()

    @pl.when(step > 0)
    def _():
        for k in range(n_side):
            pltpu.make_async_copy(stage_out[k], slab(side_out[k], k, step - 1), sem_out.at[k]).wait()

    for k in range(n_side):
        stage_out[k][...] = stage_in[k][slot].astype(stage_out[k].dtype)
    for k in range(n_side):
        pltpu.make_async_copy(stage_out[k], slab(side_out[k], k, step), sem_out.at[k]).start()

    def finish():
        @pl.when(step == n_steps - 1)
        def _():
            for k in range(n_side):
                pltpu.make_async_copy(stage_out[k], slab(side_out[k], k, step), sem_out.at[k]).wait()

    return finish


def _qv_proj_kernel(x_ref, g_ref, wq_ref, wv_ref, wgd_ref, h_ref, qt_ref, vt_ref, gd_ref):
    nt = (((1,), (1,)), ((), ()))
    h = _rms(x_ref[...], g_ref[...]).astype(bf16)
    h_ref[...] = h
    qt_ref[0] = lax.dot_general(wq_ref[...], h, nt, preferred_element_type=f32).astype(qt_ref.dtype)
    vt = lax.dot_general(wv_ref[...], h, nt, preferred_element_type=f32).astype(vt_ref.dtype)
    for kt in range(vt_ref.shape[1]):
        vt_ref[0, kt] = vt[:, kt * LANES:(kt + 1) * LANES]
    gd_ref[...] = jnp.dot(h, wgd_ref[...], preferred_element_type=f32).astype(gd_ref.dtype)


def _qv_proj(x, g, wq_t, wv_t, w_gd, b, s, tm=1024):
    t = x.shape[0]
    per_b = s // tm
    return pl.pallas_call(
        _qv_proj_kernel,
        grid=(t // tm,),
        in_specs=[pl.BlockSpec((tm, D_MODEL), lambda i: (i, 0)),
                  _const_spec((1, D_MODEL)),
                  _const_spec((A_WIDTH, D_MODEL)),
                  _const_spec((A_WIDTH, D_MODEL)),
                  _const_spec((D_MODEL, LANES))],
        out_specs=[pl.BlockSpec((tm, D_MODEL), lambda i: (i, 0)),
                   pl.BlockSpec((1, A_WIDTH, tm), lambda i: (i // per_b, 0, i % per_b)),
                   pl.BlockSpec((1, tm // LANES, A_WIDTH, LANES), lambda i: (i // per_b, i % per_b, 0, 0)),
                   pl.BlockSpec((tm, LANES), lambda i: (i, 0))],
        out_shape=[jax.ShapeDtypeStruct((t, D_MODEL), bf16),
                   jax.ShapeDtypeStruct((b, A_WIDTH, s), bf16),
                   jax.ShapeDtypeStruct((b, s // LANES, A_WIDTH, LANES), bf16),
                   jax.ShapeDtypeStruct((t, LANES), bf16)],
        compiler_params=_params("parallel"),
        name="qv_proj",
    )(x, g, wq_t, wv_t, w_gd)


def _in_proj_kernel(*refs, n_side):
    h_ref, w_ref = refs[:2]
    o_ref = refs[2 + n_side]
    step = pl.program_id(0) * pl.num_programs(1) + pl.program_id(1)
    finish = _side_cast(step, pl.num_programs(0) * pl.num_programs(1), refs[2:2 + n_side],
                        refs[3 + n_side:3 + 2 * n_side], refs[3 + 2 * n_side:])
    o_ref[...] = jnp.dot(h_ref[...], w_ref[...], preferred_element_type=f32).astype(o_ref.dtype)
    finish()


def _in_proj(h, w, side=(), tm=1024):
    t = h.shape[0]
    n = w.shape[1]
    grid = (t // tm, n // IN_TILE_N)
    hbm = pl.BlockSpec(memory_space=pl.ANY)
    outs = pl.pallas_call(
        functools.partial(_in_proj_kernel, n_side=len(side)),
        grid=grid,
        in_specs=[pl.BlockSpec((tm, D_MODEL), lambda i, j: (i, 0)),
                  pl.BlockSpec((D_MODEL, IN_TILE_N), lambda i, j: (0, j))] + [hbm] * len(side),
        out_specs=[pl.BlockSpec((tm, IN_TILE_N), lambda i, j: (i, j))] + [hbm] * len(side),
        out_shape=[jax.ShapeDtypeStruct((t, n), bf16)] + [jax.ShapeDtypeStruct(m.shape, bf16) for m in side],
        scratch_shapes=_side_cast_scratch(side, grid[0] * grid[1]),
        compiler_params=_params("arbitrary", "arbitrary"),
        name="in_proj",
    )(h, w, *side)
    return outs[0], tuple(outs[1:])


def _attn_kernel(qt_ref, k_ref, vt_ref, bias_ref, o_ref):
    i = pl.program_id(1)
    row = lax.broadcasted_iota(jnp.int32, (LANES, QBLK), 0)
    upper = row >= A_HEAD_DIM
    n_kt = KBAND // LANES
    kt_seq = [i + kt - KPAD // LANES for kt in range(n_kt)]
    kt_clamped = [jnp.maximum(j, 0) for j in kt_seq]
    for hp in range(A_WIDTH // LANES):
        pair = slice(hp * LANES, (hp + 1) * LANES)
        qt = qt_ref[0, pair, :] * (A_HEAD_DIM ** -0.5)
        kps = [k_ref[0, pl.ds(pl.multiple_of(j * LANES, LANES), LANES), pair] for j in kt_clamped]
        outs = []
        for s in range(2):
            h = 2 * hp + s
            qm = jnp.where(upper == bool(s), qt, jnp.zeros_like(qt))
            tiles = []
            for kt in range(n_kt):
                t = jnp.dot(kps[kt], qm, preferred_element_type=f32) + bias_ref[h, kt]
                tiles.append(jnp.where(kt_seq[kt] >= 0, t, -1e30))
            m = tiles[0]
            for t in tiles[1:]:
                m = jnp.maximum(m, t)
            m = jnp.max(m, axis=0, keepdims=True)
            l = None
            o = None
            for kt in range(n_kt):
                p = jnp.exp(tiles[kt] - m)
                ls = jnp.sum(p, axis=0, keepdims=True)
                vt = vt_ref[0, kt_clamped[kt], h * A_HEAD_DIM:(h + 1) * A_HEAD_DIM, :]
                part = jnp.dot(vt, p.astype(bf16), preferred_element_type=f32)
                l = ls if l is None else l + ls
                o = part if o is None else o + part
            outs.append(o / l)
        o_ref[0, :, pair] = jnp.concatenate(outs, axis=0).T.astype(o_ref.dtype)


def _attention(qt, proj3, vt, bias_t):
    b, s, _ = proj3.shape
    return pl.pallas_call(
        _attn_kernel,
        grid=(b, s // QBLK),
        in_specs=[pl.BlockSpec((1, A_WIDTH, QBLK), lambda bi, i: (bi, 0, i)),
                  pl.BlockSpec((1, s, A_WIDTH), lambda bi, i: (bi, 0, COL_KA // A_WIDTH)),
                  pl.BlockSpec((1, s // LANES, A_WIDTH, LANES), lambda bi, i: (bi, 0, 0, 0)),
                  _const_spec((A_HEADS, KBAND // LANES, LANES, QBLK))],
        out_specs=pl.BlockSpec((1, QBLK, A_WIDTH), lambda bi, i: (bi, i, 0)),
        out_shape=jax.ShapeDtypeStruct((b, s, A_WIDTH), bf16),
        compiler_params=_params("parallel", "arbitrary"),
        name="chunk_attention",
    )(qt, proj3, vt, bias_t)


def _attn_bias(rel_bias):
    h = rel_bias.shape[0]
    r = np.arange(QBLK)[:, None]
    k = np.arange(KBAND)[None, :]
    visible = np.where(r < CHUNK, k < KBAND - CHUNK, k >= CHUNK)
    span = QBLK + KBAND - 1
    n_far = span - 2 * REL_CLIP
    line = jnp.concatenate([jnp.broadcast_to(rel_bias[:, -1:], (h, n_far)), rel_bias[:, :0:-1]], axis=1)
    line = jnp.pad(line.astype(f32), ((0, 0), (0, 1)))
    skew = jnp.tile(line, (1, QBLK))[:, :QBLK * span].reshape(h, QBLK, span)
    bias = jnp.where(visible[None], skew[:, :, QBLK - 1:], -1e30)
    return jnp.swapaxes(bias, 1, 2).reshape(h, KBAND // LANES, LANES, QBLK)


def _gla_kernel(q_ref, k_ref, v_ref, gd_ref, r_ref, wup_ref, b_ref, gn_ref, o_ref, state):
    @pl.when(pl.program_id(1) == 0)
    def _():
        state[...] = jnp.zeros_like(state)

    nb, tq = q_ref.shape[0], q_ref.shape[1]
    tn = (((0,), (0,)), ((), ()))
    nt = (((1,), (1,)), ((), ()))
    row = lax.broadcasted_iota(jnp.int32, (tq, tq), 0)
    col = lax.broadcasted_iota(jnp.int32, (tq, tq), 1)
    later = jnp.logical_and(col > row, col // CHUNK == row // CHUNK)
    later = jnp.where(later, 1.0, 0.0).astype(bf16)
    log_a, kdec = [], []
    for bb in range(nb):
        z = jnp.dot(gd_ref[bb], wup_ref[...], preferred_element_type=f32) + b_ref[...]
        la = (jnp.minimum(z, 0.0) - jnp.log(1.0 + jnp.exp(-jnp.abs(z)))) / GATE_TEMP
        la_hi = la.astype(bf16)
        la_lo = (la - la_hi.astype(f32)).astype(bf16)
        rev = (jnp.dot(later, la_hi, preferred_element_type=f32)
               + jnp.dot(later, la_lo, preferred_element_type=f32))
        log_a.append(la)
        kdec.append((k_ref[bb].astype(f32) * jnp.exp(rev)).astype(bf16))
    hrow = lax.broadcasted_iota(jnp.int32, (B_HEADS * CHUNK, B_K_WIDTH), 0) // CHUNK
    hlane = lax.broadcasted_iota(jnp.int32, (B_HEADS * CHUNK, B_K_WIDTH), 1) // B_KEY_DIM
    own = hrow == hlane
    st = [state[bb] for bb in range(nb)]
    for c in range(tq // CHUNK):
        rows = slice(c * CHUNK, (c + 1) * CHUNK)
        for bb in range(nb):
            decay = jnp.exp(jnp.sum(log_a[bb][rows], axis=0, keepdims=True))
            kd = kdec[bb][rows]
            kd4 = jnp.where(own, jnp.concatenate([kd] * B_HEADS, axis=0), jnp.zeros((), bf16))
            v4 = jnp.concatenate([v_ref[bb, rows, h * B_VAL_DIM:(h + 1) * B_VAL_DIM] for h in range(B_HEADS)],
                                 axis=0)
            d_state = lax.dot_general(v4, kd4, tn, preferred_element_type=f32)
            st[bb] = decay * st[bb] + d_state
            qc = q_ref[bb, rows, :] * (B_KEY_DIM ** -0.5)
            q4 = jnp.where(own, jnp.concatenate([qc] * B_HEADS, axis=0), jnp.zeros((), bf16))
            o4 = lax.dot_general(q4, st[bb].astype(bf16), nt, preferred_element_type=f32)
            for h in range(B_HEADS):
                vcols = slice(h * B_VAL_DIM, (h + 1) * B_VAL_DIM)
                o = _rms(o4[h * CHUNK:(h + 1) * CHUNK], gn_ref[:, vcols])
                r = r_ref[bb, rows, vcols].astype(f32)
                o_ref[bb, rows, vcols] = (o * _silu(r)).astype(o_ref.dtype)
    for bb in range(nb):
        state[bb] = st[bb]


def _gla(proj3, gd3, w_up, b_gate, gla_norm, tq=256, nb=GLA_NB):
    b, s, _ = proj3.shape
    return pl.pallas_call(
        _gla_kernel,
        grid=(b // nb, s // tq),
        in_specs=[pl.BlockSpec((nb, tq, B_K_WIDTH), lambda bi, i: (bi, i, COL_QB // B_K_WIDTH)),
                  pl.BlockSpec((nb, tq, B_K_WIDTH), lambda bi, i: (bi, i, COL_KB // B_K_WIDTH)),
                  pl.BlockSpec((nb, tq, B_V_WIDTH), lambda bi, i: (bi, i, COL_VB // B_V_WIDTH)),
                  pl.BlockSpec((nb, tq, LANES), lambda bi, i: (bi, i, 0)),
                  pl.BlockSpec((nb, tq, B_V_WIDTH), lambda bi, i: (bi, i, COL_R // B_V_WIDTH)),
                  _const_spec((LANES, B_K_WIDTH)),
                  _const_spec((1, B_K_WIDTH)),
                  _const_spec((1, B_V_WIDTH))],
        out_specs=pl.BlockSpec((nb, tq, B_V_WIDTH), lambda bi, i: (bi, i, 0)),
        out_shape=jax.ShapeDtypeStruct((b, s, B_V_WIDTH), bf16),
        scratch_shapes=[pltpu.VMEM((nb, B_VAL_DIM, B_K_WIDTH), f32)],
        compiler_params=_params("parallel", "arbitrary"),
        name="gla",
    )(proj3, proj3, proj3, gd3, proj3, w_up, b_gate, gla_norm)


def _route_top2(h, rw_ref):
    logits = jnp.dot(h.astype(bf16), rw_ref[...], preferred_element_type=f32)
    lane = lax.broadcasted_iota(jnp.int32, logits.shape, 1).astype(f32)
    neg = -jnp.inf
    lg = jnp.where(lane < N_EXPERTS, logits, neg)
    m1 = jnp.max(lg, axis=-1, keepdims=True)
    i1 = jnp.min(jnp.where(lg == m1, lane, float(LANES)), axis=-1, keepdims=True)
    lg2 = jnp.where(lane == i1, neg, lg)
    m2 = jnp.max(lg2, axis=-1, keepdims=True)
    i2 = jnp.min(jnp.where(lg2 == m2, lane, float(LANES)), axis=-1, keepdims=True)
    e2 = jnp.exp(m2 - m1)
    w1 = 1.0 / (1.0 + e2)
    w2 = e2 / (1.0 + e2)
    return jnp.where(lane == 0, i1, jnp.where(lane == 1, i2, jnp.where(lane == 2, w1, jnp.where(lane == 3, w2, 0.0))))


def _mix_out_kernel(*refs, routed):
    if routed:
        (ya_ref, yb_ref, ga_ref, gb_ref, x_ref, wa_ref, wb_ref, wo_ref, gpost_ref, gpre_ref, rw_ref,
         xo_ref, ho_ref, route_ref) = refs
    else:
        (ya_ref, yb_ref, ga_ref, gb_ref, x_ref, wa_ref, wb_ref, wo_ref, gpost_ref, gpre_ref,
         xo_ref, ho_ref) = refs
    a = jnp.dot(ya_ref[...], wa_ref[...], preferred_element_type=f32)
    b = jnp.dot(yb_ref[...], wb_ref[...], preferred_element_type=f32)
    merged = _sigmoid(ga_ref[...].astype(f32)) * a + _sigmoid(gb_ref[...].astype(f32)) * b
    y = jnp.dot(merged.astype(bf16), wo_ref[...], preferred_element_type=f32)
    x1 = x_ref[...] + _rms(y, gpost_ref[...])
    h = _rms(x1, gpre_ref[...])
    xo_ref[...] = x1
    if routed:
        _store_row_tiles(ho_ref, h)
        route_ref[...] = _route_top2(h, rw_ref)
    else:
        ho_ref[...] = h.astype(ho_ref.dtype)


def _mix_out(ya, yb, proj, x, wa, wb, wo, g_post, g_pre, router_w=None, tm=512):
    t = x.shape[0]
    routed = router_w is not None
    row = lambda i: (i, 0)
    in_specs = [pl.BlockSpec((tm, A_WIDTH), row),
                pl.BlockSpec((tm, B_V_WIDTH), row),
                pl.BlockSpec((tm, D_MODEL), lambda i: (i, COL_GA // D_MODEL)),
                pl.BlockSpec((tm, D_MODEL), lambda i: (i, COL_GB // D_MODEL)),
                pl.BlockSpec((tm, D_MODEL), row),
                _const_spec((A_WIDTH, D_MODEL)),
                _const_spec((B_V_WIDTH, D_MODEL)),
                _const_spec((D_MODEL, D_MODEL)),
                _const_spec((1, D_MODEL)),
                _const_spec((1, D_MODEL))]
    args = [ya, yb, proj, proj, x, wa, wb, wo, g_post, g_pre]
    if routed:
        out_specs = [pl.BlockSpec((tm, D_MODEL), row), pl.BlockSpec((tm * ROW_TILE, LANES), row)]
        out_shape = [jax.ShapeDtypeStruct((t, D_MODEL), f32), jax.ShapeDtypeStruct((t * ROW_TILE, LANES), f32)]
    else:
        out_specs = [pl.BlockSpec((tm, D_MODEL), row), pl.BlockSpec((tm, D_MODEL), row)]
        out_shape = [jax.ShapeDtypeStruct((t, D_MODEL), f32), jax.ShapeDtypeStruct((t, D_MODEL), bf16)]
    if routed:
        in_specs.append(_const_spec((D_MODEL, LANES)))
        args.append(router_w)
        out_specs.append(pl.BlockSpec((tm, LANES), row))
        out_shape.append(jax.ShapeDtypeStruct((t, LANES), f32))
    return pl.pallas_call(
        functools.partial(_mix_out_kernel, routed=routed),
        grid=(t // tm,),
        in_specs=in_specs,
        out_specs=out_specs,
        out_shape=out_shape,
        compiler_params=_params("parallel"),
        name="mix_out",
    )(*args)


def _post_ffn(y, x1, p, gpost_ref, wpp_ref, wpg_ref, gple_ref):
    x2 = x1 + _rms(y, gpost_ref[...])
    e = jnp.dot(p.astype(bf16), wpp_ref[...], preferred_element_type=f32)
    e = e * _sigmoid(jnp.dot(x2.astype(bf16), wpg_ref[...], preferred_element_type=f32))
    return x2 + _rms(e, gple_ref[...])


def _swiglu_chunks(x, wg_ref, wu_ref, wd_ref, lead, width):
    acc = None
    for c0, c1 in _chunks(width, FF_CHUNK):
        g = jnp.dot(x, wg_ref[lead + (slice(None), slice(c0, c1))], preferred_element_type=f32)
        u = jnp.dot(x, wu_ref[lead + (slice(None), slice(c0, c1))], preferred_element_type=f32)
        act = (_silu(g) * u).astype(bf16)
        part = jnp.dot(act, wd_ref[lead + (slice(c0, c1), slice(None))], preferred_element_type=f32)
        acc = part if acc is None else acc + part
    return acc


def _dense_ffn_kernel(*refs, n_side):
    (h_ref, x_ref, p_ref, wg_ref, wu_ref, wd_ref, gpost_ref, wpp_ref, wpg_ref, gple_ref) = refs[:10]
    o_ref = refs[10 + n_side]
    finish = _side_cast(pl.program_id(0), pl.num_programs(0), refs[10:10 + n_side],
                        refs[11 + n_side:11 + 2 * n_side], refs[11 + 2 * n_side:])
    y = _swiglu_chunks(h_ref[...], wg_ref, wu_ref, wd_ref, (), wg_ref.shape[1])
    o_ref[...] = _post_ffn(y, x_ref[...], p_ref[...], gpost_ref, wpp_ref, wpg_ref, gple_ref)
    finish()


def _dense_ffn(h, x1, p, wg, wu, wd, g_post, wpp, wpg, g_ple, side=(), tm=512):
    t = x1.shape[0]
    ff = wg.shape[1]
    steps = t // tm
    row = lambda i: (i, 0)
    hbm = pl.BlockSpec(memory_space=pl.ANY)
    n_side = len(side)
    outs = pl.pallas_call(
        functools.partial(_dense_ffn_kernel, n_side=n_side),
        grid=(steps,),
        in_specs=[pl.BlockSpec((tm, D_MODEL), row),
                  pl.BlockSpec((tm, D_MODEL), row),
                  pl.BlockSpec((tm, PLE_DIM), row),
                  _const_spec((D_MODEL, ff)),
                  _const_spec((D_MODEL, ff)),
                  _const_spec((ff, D_MODEL)),
                  _const_spec((1, D_MODEL)),
                  _const_spec((PLE_DIM, D_MODEL)),
                  _const_spec((D_MODEL, D_MODEL)),
                  _const_spec((1, D_MODEL))] + [hbm] * n_side,
        out_specs=[pl.BlockSpec((tm, D_MODEL), row)] + [hbm] * n_side,
        out_shape=[jax.ShapeDtypeStruct((t, D_MODEL), f32)] + [jax.ShapeDtypeStruct(m.shape, bf16) for m in side],
        scratch_shapes=_side_cast_scratch(side, steps),
        compiler_params=pltpu.CompilerParams(dimension_semantics=("arbitrary",), vmem_limit_bytes=VMEM_LIMIT_BIG),
        name="dense_ffn",
    )(h, x1, p, wg, wu, wd, g_post, wpp, wpg, g_ple, *side)
    return outs[0], tuple(outs[1:])


def _scatter_kernel(fill_ref, pos_ref, h_ref, xs_out, zbuf, sem, zsem):
    rows = h_ref.shape[0] // ROW_TILE

    @pl.when(pl.program_id(0) == 0)
    def _():
        zbuf[...] = jnp.zeros_like(zbuf)

        def zero_rows(first_row):
            start = pl.multiple_of(first_row * ROW_TILE, ROW_TILE)
            fill = pltpu.make_async_copy(zbuf, xs_out.at[pl.ds(start, MOE_TM * ROW_TILE)], zsem)
            fill.start()
            fill.wait()

        for e in range(N_EXPERTS):
            zero_rows(fill_ref[e])

        def zero_tile(tile, c):
            zero_rows(tile * MOE_TM)
            return c

        lax.fori_loop(fill_ref[N_EXPERTS], xs_out.shape[0] // (MOE_TM * ROW_TILE), zero_tile, 0)

    def issue(r, c):
        for k in range(2):
            pltpu.make_async_copy(_tile_rows(h_ref, r), _tile_rows(xs_out, pos_ref[2 * r + k]),
                                  sem.at[k]).start(priority=k)
        return c

    lax.fori_loop(0, rows, issue, 0, unroll=ISSUE_UNROLL)
    for k in range(2):
        pltpu.make_async_copy(h_ref, xs_out.at[pl.ds(0, rows * ROW_TILE)], sem.at[k]).wait()


def _scatter_rows(fill_plan, pos, h, n_rows):
    t = h.shape[0] // ROW_TILE
    return pl.pallas_call(
        _scatter_kernel,
        grid_spec=pltpu.PrefetchScalarGridSpec(
            num_scalar_prefetch=1,
            grid=(t // ROW_BLK,),
            in_specs=[pl.BlockSpec((2 * ROW_BLK,), lambda i, fill: (i,), memory_space=pltpu.SMEM),
                      pl.BlockSpec((ROW_BLK * ROW_TILE, LANES), lambda i, fill: (i, 0))],
            out_specs=pl.BlockSpec(memory_space=pl.ANY),
            scratch_shapes=[pltpu.VMEM((MOE_TM * ROW_TILE, LANES), f32),
                            pltpu.SemaphoreType.DMA((2,)),
                            pltpu.SemaphoreType.DMA(())]),
        out_shape=jax.ShapeDtypeStruct((n_rows * ROW_TILE, LANES), f32),
        compiler_params=_params("arbitrary"),
        name="moe_scatter",
    )(fill_plan, pos, h)


def _expert_kernel(te_ref, nu_ref, x_ref, wg_ref, wu_ref, wd_ref, o_ref, acc_ref, xb_ref):
    del te_ref
    i = pl.program_id(0)
    f = pl.program_id(1)

    @pl.when(jnp.logical_and(i < nu_ref[0], f == 0))
    def _():
        xb_ref[...] = _load_row_tiles(x_ref, MOE_TM).astype(bf16)

    @pl.when(i < nu_ref[0])
    def _():
        y = _swiglu_chunks(xb_ref[...], wg_ref, wu_ref, wd_ref, (0,), wg_ref.shape[2])

        @pl.when(f == 0)
        def _():
            acc_ref[...] = y

        @pl.when(f > 0)
        def _():
            acc_ref[...] += y

    last = f == pl.num_programs(1) - 1

    @pl.when(jnp.logical_and(last, i < nu_ref[0]))
    def _():
        _store_row_tiles(o_ref, acc_ref[...])

    @pl.when(jnp.logical_and(last, i >= nu_ref[0]))
    def _():
        o_ref[...] = jnp.zeros_like(o_ref)


def _expert_ffn(tile_expert, n_used, xs, wg, wu, wd, tf=1792):
    n_rows = xs.shape[0] // ROW_TILE
    ff = wg.shape[2]
    nf = ff // tf

    def f_idx(i, f, nu):
        return jnp.where(i < nu[0], f, nf - 1)

    return pl.pallas_call(
        _expert_kernel,
        grid_spec=pltpu.PrefetchScalarGridSpec(
            num_scalar_prefetch=2,
            grid=(n_rows // MOE_TM, nf),
            in_specs=[pl.BlockSpec((MOE_TM * ROW_TILE, LANES), lambda i, f, te, nu: (i, 0)),
                      pl.BlockSpec((1, D_MODEL, tf), lambda i, f, te, nu: (te[i], 0, f_idx(i, f, nu))),
                      pl.BlockSpec((1, D_MODEL, tf), lambda i, f, te, nu: (te[i], 0, f_idx(i, f, nu))),
                      pl.BlockSpec((1, tf, D_MODEL), lambda i, f, te, nu: (te[i], f_idx(i, f, nu), 0))],
            out_specs=pl.BlockSpec((MOE_TM * ROW_TILE, LANES), lambda i, f, te, nu: (i, 0)),
            scratch_shapes=[pltpu.VMEM((MOE_TM, D_MODEL), f32), pltpu.VMEM((MOE_TM, D_MODEL), bf16)]),
        out_shape=jax.ShapeDtypeStruct((n_rows * ROW_TILE, LANES), f32),
        compiler_params=_params("arbitrary", "arbitrary"),
        name="moe_experts",
    )(tile_expert, n_used, xs, wg, wu, wd)


def _combine_kernel(pos0_ref, pos_next_ref, ys_hbm, route_ref, x_ref, p_ref, gpost_ref, wpp_ref, wpg_ref, gple_ref,
                    o_ref, buf, sem):
    i = pl.program_id(0)
    rows = x_ref.shape[0]
    slot = i % 2

    def fetch(pos_ref, into):
        def issue(r, c):
            for k in range(2):
                pltpu.make_async_copy(_tile_rows(ys_hbm, pos_ref[2 * r + k]), _tile_rows(buf.at[into, k], r),
                                      sem.at[into, k]).start(priority=k)
            return c

        lax.fori_loop(0, rows, issue, 0, unroll=ISSUE_UNROLL)

    @pl.when(i == 0)
    def _():
        fetch(pos0_ref, 0)

    @pl.when(i + 1 < pl.num_programs(0))
    def _():
        fetch(pos_next_ref, 1 - slot)

    for k in range(2):
        pltpu.make_async_copy(ys_hbm.at[pl.ds(0, rows * ROW_TILE)], buf.at[slot, k], sem.at[slot, k]).wait()
    route = route_ref[...]
    y = (route[:, 2:3] * _load_row_tiles(buf.at[slot, 0], rows)
         + route[:, 3:4] * _load_row_tiles(buf.at[slot, 1], rows))
    o_ref[...] = _post_ffn(y, x_ref[...], p_ref[...], gpost_ref, wpp_ref, wpg_ref, gple_ref)


def _combine(pos, ys, route, x1, p, g_post, wpp, wpg, g_ple):
    t = x1.shape[0]
    n_blk = t // ROW_BLK
    row = lambda i: (i, 0)
    smem = functools.partial(pl.BlockSpec, (2 * ROW_BLK,), memory_space=pltpu.SMEM)
    return pl.pallas_call(
        _combine_kernel,
        grid=(n_blk,),
        in_specs=[smem(lambda i: (0,)),
                  smem(lambda i: (jnp.minimum(i + 1, n_blk - 1),)),
                  pl.BlockSpec(memory_space=pl.ANY),
                  pl.BlockSpec((ROW_BLK, LANES), row),
                  pl.BlockSpec((ROW_BLK, D_MODEL), row),
                  pl.BlockSpec((ROW_BLK, PLE_DIM), row),
                  _const_spec((1, D_MODEL)),
                  _const_spec((PLE_DIM, D_MODEL)),
                  _const_spec((D_MODEL, D_MODEL)),
                  _const_spec((1, D_MODEL))],
        out_specs=pl.BlockSpec((ROW_BLK, D_MODEL), row),
        out_shape=jax.ShapeDtypeStruct((t, D_MODEL), f32),
        scratch_shapes=[pltpu.VMEM((2, 2, ROW_BLK * ROW_TILE, LANES), f32),
                        pltpu.SemaphoreType.DMA((2, 2))],
        compiler_params=_params("arbitrary"),
        name="moe_combine",
    )(pos, pos, ys, route, x1, p, g_post, wpp, wpg, g_ple)


def _route_slots(route, n_tiles):
    ids = route[:, 0:2].astype(jnp.int32).reshape(-1)
    onehot = (ids[None, :] == jnp.arange(N_EXPERTS, dtype=jnp.int32)[:, None]).astype(jnp.int32)
    csum = jnp.cumsum(onehot, axis=1)
    rank = jnp.sum(onehot * csum, axis=0) - 1
    counts = csum[:, -1]
    padded = ((counts + MOE_TM - 1) // MOE_TM) * MOE_TM
    ends = jnp.cumsum(padded)
    pos = (ends - padded)[ids] + rank
    tile_start = jnp.arange(n_tiles, dtype=jnp.int32) * MOE_TM
    tile_expert = jnp.minimum(jnp.sum((tile_start[:, None] >= ends[None, :]).astype(jnp.int32), axis=1),
                              N_EXPERTS - 1)
    n_used = (ends[-1] // MOE_TM).reshape(1)
    fill_plan = jnp.concatenate([ends - padded + counts, n_used])
    return (pos.astype(jnp.int32), tile_expert.astype(jnp.int32), n_used.astype(jnp.int32),
            fill_plan.astype(jnp.int32))


def _moe(h, route, x1, p, wg, wu, wd, g_post, wpp, wpg, g_ple):
    t = x1.shape[0]
    n_tiles = 2 * t // MOE_TM + N_EXPERTS
    pos, tile_expert, n_used, fill_plan = _route_slots(route, n_tiles)
    xs = _scatter_rows(fill_plan, pos, h, n_tiles * MOE_TM)
    ys = _expert_ffn(tile_expert, n_used, xs, wg, wu, wd)
    return _combine(pos, ys, route, x1, p, g_post, wpp, wpg, g_ple)


def _prep_w_in(w):
    main = jnp.concatenate([w[:, SRC_KA:SRC_VA], w[:, SRC_QB:SRC_GD], w[:, SRC_R:SRC_END]], axis=1)
    wq_t = w[:, SRC_QA:SRC_KA].T
    wv_t = w[:, SRC_VA:SRC_QB].T
    w_gd = jnp.pad(w[:, SRC_GD:SRC_R], ((0, 0), (0, LANES - GATE_RANK)))
    return main.astype(bf16), wq_t.astype(bf16), wv_t.astype(bf16), w_gd.astype(bf16)


def kernel(x, p, w_in, rel_bias, w_gla_gate_up, b_gla_gate, gla_norm, w_branch_a, w_branch_b, w_out, norm_mix_pre, norm_mix_post, norm_ffn_pre, norm_ffn_post, ffn_w_gate, ffn_w_up, ffn_w_down, router_w, moe_w_gate, moe_w_up, moe_w_down, ple_w_proj, ple_w_gate, ple_norm):
    b, s, d = x.shape
    t = b * s
    depth = w_in.shape[0]
    x = x.reshape(t, d)
    vec = lambda a: a.reshape(1, -1).astype(f32)
    moe_bf16 = None
    for i in range(depth):
        w_main, wq_t, wv_t, w_gd = _prep_w_in(w_in[i])
        h_in, qt, vt, gd = _qv_proj(x, vec(norm_mix_pre[i]), wq_t, wv_t, w_gd, b, s)
        routed = i % 2 == 1
        nxt = (i + 1) // 2
        flat = lambda w: w[nxt].reshape(-1, w.shape[-1])
        ahead = (not routed) and i + 1 < depth
        proj, down_bf16 = _in_proj(h_in, w_main, side=(flat(moe_w_down),) if ahead else ())
        proj3 = proj.reshape(b, s, MAIN_WIDTH)
        ya = _attention(qt, proj3, vt, _attn_bias(rel_bias[i])).reshape(t, A_WIDTH)
        w_up = jnp.pad(w_gla_gate_up[i], ((0, LANES - GATE_RANK), (0, 0))).astype(bf16)
        yb = _gla(proj3, gd.reshape(b, s, LANES), w_up, vec(b_gla_gate[i]), vec(gla_norm[i])).reshape(t, B_V_WIDTH)
        j = i // 2
        rw = jnp.pad(router_w[j], ((0, 0), (0, LANES - N_EXPERTS))).astype(bf16) if routed else None
        outs = _mix_out(ya, yb, proj, x, w_branch_a[i].astype(bf16), w_branch_b[i].astype(bf16),
                        w_out[i].astype(bf16), vec(norm_mix_post[i]), vec(norm_ffn_pre[i]), rw)
        p_i = p[i].reshape(t, PLE_DIM)
        tail = (vec(norm_ffn_post[i]), ple_w_proj[i].astype(bf16), ple_w_gate[i].astype(bf16), vec(ple_norm[i]))
        if routed:
            x1, h, route = outs
            if moe_bf16 is None:
                moe_bf16 = tuple(w[j].astype(bf16) for w in (moe_w_gate, moe_w_up, moe_w_down))
            x = _moe(h, route, x1, p_i, *moe_bf16, *tail)
            moe_bf16 = None
        else:
            x1, h = outs
            x, cast = _dense_ffn(h, x1, p_i, ffn_w_gate[j].astype(bf16), ffn_w_up[j].astype(bf16),
                                 ffn_w_down[j].astype(bf16), *tail,
                                 side=(flat(moe_w_gate), flat(moe_w_up)) if ahead else ())
            if ahead:
                moe_bf16 = tuple(c.reshape(w.shape[1:])
                                 for c, w in zip(cast + down_bf16, (moe_w_gate, moe_w_up, moe_w_down)))
    return x.reshape(b, s, d)
```

```python
import functools

import numpy as np
import jax
import jax.numpy as jnp
from jax import lax
from jax.experimental import pallas as pl
from jax.experimental.pallas import tpu as pltpu

f32 = jnp.float32
bf16 = jnp.bfloat16

D_MODEL = 1024
CHUNK = 64
N_PREV_CHUNKS = 8
A_HEADS = 8
A_HEAD_DIM = 64
A_WIDTH = A_HEADS * A_HEAD_DIM
REL_CLIP = 128
B_HEADS = 4
B_KEY_DIM = 64
B_VAL_DIM = 128
B_K_WIDTH = B_HEADS * B_KEY_DIM
B_V_WIDTH = B_HEADS * B_VAL_DIM
GATE_RANK = 16
GATE_TEMP = 16.0
N_EXPERTS = 8
PLE_DIM = 256
NORM_EPS = 1e-6

LANES = 128
VMEM_LIMIT = 56 * 1024 * 1024
VMEM_LIMIT_BIG = 62 * 1024 * 1024

SRC_QA, SRC_KA, SRC_VA, SRC_QB, SRC_GD, SRC_R = 0, 512, 1024, 1536, 2560, 2576
SRC_END = 5136
COL_KA, COL_QB, COL_KB, COL_VB, COL_R, COL_GA, COL_GB = 0, 512, 768, 1024, 1536, 2048, 3072
MAIN_WIDTH = 4096
IN_TILE_N = 2048

QBLK = 2 * CHUNK
KBAND = (N_PREV_CHUNKS + 2) * CHUNK
KPAD = N_PREV_CHUNKS * CHUNK

SUBLANES = 8
ROW_TILE = D_MODEL // LANES
assert ROW_TILE == SUBLANES
MOE_TM = 512
ROW_BLK = 512
FFN_TM = 512
FF_CHUNK = 512
ISSUE_UNROLL = 8
GLA_NB = 2


def _params(*sem):
    return pltpu.CompilerParams(dimension_semantics=sem, vmem_limit_bytes=VMEM_LIMIT)


def _rms(x, g):
    return x * lax.rsqrt(jnp.mean(x * x, axis=-1, keepdims=True) + NORM_EPS) * g


def _sigmoid(x):
    return 1.0 / (1.0 + jnp.exp(-x))


def _silu(x):
    return x * _sigmoid(x)


def _chunks(n, c):
    return [(s, min(s + c, n)) for s in range(0, n, c)]


def _store_row_tiles(ref, x):
    rows = x.shape[0]
    for c in range(ROW_TILE):
        ref[pl.ds(c, rows, stride=ROW_TILE), :] = x[:, c * LANES:(c + 1) * LANES].astype(ref.dtype)


def _load_row_tiles(ref, rows):
    return jnp.concatenate([ref[pl.ds(c, rows, stride=ROW_TILE), :] for c in range(ROW_TILE)], axis=1)


def _tile_rows(ref, r):
    return ref.at[pl.ds(pl.multiple_of(r * ROW_TILE, ROW_TILE), ROW_TILE)]


def _const_spec(shape):
    nd = len(shape)
    return pl.BlockSpec(shape, lambda *_: (0,) * nd, pipeline_mode=pl.Buffered(1))


def _side_cast_scratch(side, steps):
    slabs = [(m.shape[0] // steps, m.shape[1]) for m in side]
    assert all(r * steps == m.shape[0] for (r, _), m in zip(slabs, side))
    scratch = [pltpu.VMEM((2,) + sl, f32) for sl in slabs] + [pltpu.VMEM(sl, bf16) for sl in slabs]
    if side:
        scratch += [pltpu.SemaphoreType.DMA((len(side), 2)), pltpu.SemaphoreType.DMA((len(side),))]
    return scratch


def _side_cast(step, n_steps, side_in, side_out, scratch):
    n_side = len(side_in)
    if not n_side:
        return lambda: None
    stage_in, stage_out = scratch[:n_side], scratch[n_side:2 * n_side]
    sem_in, sem_out = scratch[2 * n_side], scratch[2 * n_side + 1]
    slot = step % 2

    def slab(ref, k, at):
        rows = stage_out[k].shape[0]
        return ref.at[pl.ds(pl.multiple_of(at * rows, rows), rows)]

    def fetch(at, into):
        for k in range(n_side):
            pltpu.make_async_copy(slab(side_in[k], k, at), stage_in[k].at[into], sem_in.at[k, into]).start()

    @pl.when(step == 0)
    def _():
        fetch(0, 0)

    @pl.when(step + 1 < n_steps)
    def _():
        fetch(step + 1, 1 - slot)

    for k in range(n_side):
        pltpu.make_async_copy(slab(side_in[k], k, step), stage_in[k].at[slot], sem_in.at[k, slot]).wait()

    @pl.when(step > 0)
    def _():
        for k in range(n_side):
            pltpu.make_async_copy(stage_out[k], slab(side_out[k], k, step - 1), sem_out.at[k]).wait()

    for k in range(n_side):
        stage_out[k][...] = stage_in[k][slot].astype(stage_out[k].dtype)
    for k in range(n_side):
        pltpu.make_async_copy(stage_out[k], slab(side_out[k], k, step), sem_out.at[k]).start()

    def finish():
        @pl.when(step == n_steps - 1)
        def _():
            for k in range(n_side):
                pltpu.make_async_copy(stage_out[k], slab(side_out[k], k, step), sem_out.at[k]).wait()

    return finish


def _qv_proj_kernel(x_ref, g_ref, wq_ref, wv_ref, wgd_ref, h_ref, qt_ref, vt_ref, gd_ref):
    nt = (((1,), (1,)), ((), ()))
    h = _rms(x_ref[...], g_ref[...]).astype(bf16)
    h_ref[...] = h
    qt_ref[0] = lax.dot_general(wq_ref[...], h, nt, preferred_element_type=f32).astype(qt_ref.dtype)
    vt = lax.dot_general(wv_ref[...], h, nt, preferred_element_type=f32).astype(vt_ref.dtype)
    for kt in range(vt_ref.shape[1]):
        vt_ref[0, kt] = vt[:, kt * LANES:(kt + 1) * LANES]
    gd_ref[...] = jnp.dot(h, wgd_ref[...], preferred_element_type=f32).astype(gd_ref.dtype)


def _qv_proj(x, g, wq_t, wv_t, w_gd, b, s, tm=1024):
    t = x.shape[0]
    per_b = s // tm
    return pl.pallas_call(
        _qv_proj_kernel,
        grid=(t // tm,),
        in_specs=[pl.BlockSpec((tm, D_MODEL), lambda i: (i, 0)),
                  _const_spec((1, D_MODEL)),
                  _const_spec((A_WIDTH, D_MODEL)),
                  _const_spec((A_WIDTH, D_MODEL)),
                  _const_spec((D_MODEL, LANES))],
        out_specs=[pl.BlockSpec((tm, D_MODEL), lambda i: (i, 0)),
                   pl.BlockSpec((1, A_WIDTH, tm), lambda i: (i // per_b, 0, i % per_b)),
                   pl.BlockSpec((1, tm // LANES, A_WIDTH, LANES), lambda i: (i // per_b, i % per_b, 0, 0)),
                   pl.BlockSpec((tm, LANES), lambda i: (i, 0))],
        out_shape=[jax.ShapeDtypeStruct((t, D_MODEL), bf16),
                   jax.ShapeDtypeStruct((b, A_WIDTH, s), bf16),
                   jax.ShapeDtypeStruct((b, s // LANES, A_WIDTH, LANES), bf16),
                   jax.ShapeDtypeStruct((t, LANES), bf16)],
        compiler_params=_params("parallel"),
        name="qv_proj",
    )(x, g, wq_t, wv_t, w_gd)


def _in_proj_kernel(*refs, n_side):
    h_ref, w_ref = refs[:2]
    o_ref = refs[2 + n_side]
    step = pl.program_id(0) * pl.num_programs(1) + pl.program_id(1)
    finish = _side_cast(step, pl.num_programs(0) * pl.num_programs(1), refs[2:2 + n_side],
                        refs[3 + n_side:3 + 2 * n_side], refs[3 + 2 * n_side:])
    o_ref[...] = jnp.dot(h_ref[...], w_ref[...], preferred_element_type=f32).astype(o_ref.dtype)
    finish()


def _in_proj(h, w, side=(), tm=1024):
    t = h.shape[0]
    n = w.shape[1]
    grid = (t // tm, n // IN_TILE_N)
    hbm = pl.BlockSpec(memory_space=pl.ANY)
    outs = pl.pallas_call(
        functools.partial(_in_proj_kernel, n_side=len(side)),
        grid=grid,
        in_specs=[pl.BlockSpec((tm, D_MODEL), lambda i, j: (i, 0)),
                  pl.BlockSpec((D_MODEL, IN_TILE_N), lambda i, j: (0, j))] + [hbm] * len(side),
        out_specs=[pl.BlockSpec((tm, IN_TILE_N), lambda i, j: (i, j))] + [hbm] * len(side),
        out_shape=[jax.ShapeDtypeStruct((t, n), bf16)] + [jax.ShapeDtypeStruct(m.shape, bf16) for m in side],
        scratch_shapes=_side_cast_scratch(side, grid[0] * grid[1]),
        compiler_params=_params("arbitrary", "arbitrary"),
        name="in_proj",
    )(h, w, *side)
    return outs[0], tuple(outs[1:])


def _attn_kernel(qt_ref, k_ref, vt_ref, bias_ref, o_ref):
    i = pl.program_id(1)
    row = lax.broadcasted_iota(jnp.int32, (LANES, QBLK), 0)
    upper = row >= A_HEAD_DIM
    n_kt = KBAND // LANES
    kt_seq = [i + kt - KPAD // LANES for kt in range(n_kt)]
    kt_clamped = [jnp.maximum(j, 0) for j in kt_seq]
    for hp in range(A_WIDTH // LANES):
        pair = slice(hp * LANES, (hp + 1) * LANES)
        qt = qt_ref[0, pair, :] * (A_HEAD_DIM ** -0.5)
        kps = [k_ref[0, pl.ds(pl.multiple_of(j * LANES, LANES), LANES), pair] for j in kt_clamped]
        outs = []
        for s in range(2):
            h = 2 * hp + s
            qm = jnp.where(upper == bool(s), qt, jnp.zeros_like(qt))
            tiles = []
            for kt in range(n_kt):
                t = jnp.dot(kps[kt], qm, preferred_element_type=f32) + bias_ref[h, kt]
                tiles.append(jnp.where(kt_seq[kt] >= 0, t, -1e30))
            m = tiles[0]
            for t in tiles[1:]:
                m = jnp.maximum(m, t)
            m = jnp.max(m, axis=0, keepdims=True)
            l = None
            o = None
            for kt in range(n_kt):
                p = jnp.exp(tiles[kt] - m)
                ls = jnp.sum(p, axis=0, keepdims=True)
                vt = vt_ref[0, kt_clamped[kt], h * A_HEAD_DIM:(h + 1) * A_HEAD_DIM, :]
                part = jnp.dot(vt, p.astype(bf16), preferred_element_type=f32)
                l = ls if l is None else l + ls
                o = part if o is None else o + part
            outs.append(o / l)
        o_ref[0, :, pair] = jnp.concatenate(outs, axis=0).T.astype(o_ref.dtype)


def _attention(qt, proj3, vt, bias_t):
    b, s, _ = proj3.shape
    return pl.pallas_call(
        _attn_kernel,
        grid=(b, s // QBLK),
        in_specs=[pl.BlockSpec((1, A_WIDTH, QBLK), lambda bi, i: (bi, 0, i)),
                  pl.BlockSpec((1, s, A_WIDTH), lambda bi, i: (bi, 0, COL_KA // A_WIDTH)),
                  pl.BlockSpec((1, s // LANES, A_WIDTH, LANES), lambda bi, i: (bi, 0, 0, 0)),
                  _const_spec((A_HEADS, KBAND // LANES, LANES, QBLK))],
        out_specs=pl.BlockSpec((1, QBLK, A_WIDTH), lambda bi, i: (bi, i, 0)),
        out_shape=jax.ShapeDtypeStruct((b, s, A_WIDTH), bf16),
        compiler_params=_params("parallel", "arbitrary"),
        name="chunk_attention",
    )(qt, proj3, vt, bias_t)


def _attn_bias(rel_bias):
    h = rel_bias.shape[0]
    r = np.arange(QBLK)[:, None]
    k = np.arange(KBAND)[None, :]
    visible = np.where(r < CHUNK, k < KBAND - CHUNK, k >= CHUNK)
    span = QBLK + KBAND - 1
    n_far = span - 2 * REL_CLIP
    line = jnp.concatenate([jnp.broadcast_to(rel_bias[:, -1:], (h, n_far)), rel_bias[:, :0:-1]], axis=1)
    line = jnp.pad(line.astype(f32), ((0, 0), (0, 1)))
    skew = jnp.tile(line, (1, QBLK))[:, :QBLK * span].reshape(h, QBLK, span)
    bias = jnp.where(visible[None], skew[:, :, QBLK - 1:], -1e30)
    return jnp.swapaxes(bias, 1, 2).reshape(h, KBAND // LANES, LANES, QBLK)


def _gla_kernel(q_ref, k_ref, v_ref, gd_ref, r_ref, wup_ref, b_ref, gn_ref, o_ref, state):
    @pl.when(pl.program_id(1) == 0)
    def _():
        state[...] = jnp.zeros_like(state)

    nb, tq = q_ref.shape[0], q_ref.shape[1]
    tn = (((0,), (0,)), ((), ()))
    nt = (((1,), (1,)), ((), ()))
    row = lax.broadcasted_iota(jnp.int32, (tq, tq), 0)
    col = lax.broadcasted_iota(jnp.int32, (tq, tq), 1)
    later = jnp.logical_and(col > row, col // CHUNK == row // CHUNK)
    later = jnp.where(later, 1.0, 0.0).astype(bf16)
    log_a, kdec = [], []
    for bb in range(nb):
        z = jnp.dot(gd_ref[bb], wup_ref[...], preferred_element_type=f32) + b_ref[...]
        la = (jnp.minimum(z, 0.0) - jnp.log(1.0 + jnp.exp(-jnp.abs(z)))) / GATE_TEMP
        la_hi = la.astype(bf16)
        la_lo = (la - la_hi.astype(f32)).astype(bf16)
        rev = (jnp.dot(later, la_hi, preferred_element_type=f32)
               + jnp.dot(later, la_lo, preferred_element_type=f32))
        log_a.append(la)
        kdec.append((k_ref[bb].astype(f32) * jnp.exp(rev)).astype(bf16))
    hrow = lax.broadcasted_iota(jnp.int32, (B_HEADS * CHUNK, B_K_WIDTH), 0) // CHUNK
    hlane = lax.broadcasted_iota(jnp.int32, (B_HEADS * CHUNK, B_K_WIDTH), 1) // B_KEY_DIM
    own = hrow == hlane
    st = [state[bb] for bb in range(nb)]
    for c in range(tq // CHUNK):
        rows = slice(c * CHUNK, (c + 1) * CHUNK)
        for bb in range(nb):
            decay = jnp.exp(jnp.sum(log_a[bb][rows], axis=0, keepdims=True))
            kd = kdec[bb][rows]
            kd4 = jnp.where(own, jnp.concatenate([kd] * B_HEADS, axis=0), jnp.zeros((), bf16))
            v4 = jnp.concatenate([v_ref[bb, rows, h * B_VAL_DIM:(h + 1) * B_VAL_DIM] for h in range(B_HEADS)],
                                 axis=0)
            d_state = lax.dot_general(v4, kd4, tn, preferred_element_type=f32)
            st[bb] = decay * st[bb] + d_state
            qc = q_ref[bb, rows, :] * (B_KEY_DIM ** -0.5)
            q4 = jnp.where(own, jnp.concatenate([qc] * B_HEADS, axis=0), jnp.zeros((), bf16))
            o4 = lax.dot_general(q4, st[bb].astype(bf16), nt, preferred_element_type=f32)
            for h in range(B_HEADS):
                vcols = slice(h * B_VAL_DIM, (h + 1) * B_VAL_DIM)
                o = _rms(o4[h * CHUNK:(h + 1) * CHUNK], gn_ref[:, vcols])
                r = r_ref[bb, rows, vcols].astype(f32)
                o_ref[bb, rows, vcols] = (o * _silu(r)).astype(o_ref.dtype)
    for bb in range(nb):
        state[bb] = st[bb]


def _gla(proj3, gd3, w_up, b_gate, gla_norm, tq=256, nb=GLA_NB):
    b, s, _ = proj3.shape
    return pl.pallas_call(
        _gla_kernel,
        grid=(b // nb, s // tq),
        in_specs=[pl.BlockSpec((nb, tq, B_K_WIDTH), lambda bi, i: (bi, i, COL_QB // B_K_WIDTH)),
                  pl.BlockSpec((nb, tq, B_K_WIDTH), lambda bi, i: (bi, i, COL_KB // B_K_WIDTH)),
                  pl.BlockSpec((nb, tq, B_V_WIDTH), lambda bi, i: (bi, i, COL_VB // B_V_WIDTH)),
                  pl.BlockSpec((nb, tq, LANES), lambda bi, i: (bi, i, 0)),
                  pl.BlockSpec((nb, tq, B_V_WIDTH), lambda bi, i: (bi, i, COL_R // B_V_WIDTH)),
                  _const_spec((LANES, B_K_WIDTH)),
                  _const_spec((1, B_K_WIDTH)),
                  _const_spec((1, B_V_WIDTH))],
        out_specs=pl.BlockSpec((nb, tq, B_V_WIDTH), lambda bi, i: (bi, i, 0)),
        out_shape=jax.ShapeDtypeStruct((b, s, B_V_WIDTH), bf16),
        scratch_shapes=[pltpu.VMEM((nb, B_VAL_DIM, B_K_WIDTH), f32)],
        compiler_params=_params("parallel", "arbitrary"),
        name="gla",
    )(proj3, proj3, proj3, gd3, proj3, w_up, b_gate, gla_norm)


def _route_top2(h, rw_ref):
    logits = jnp.dot(h.astype(bf16), rw_ref[...], preferred_element_type=f32)
    lane = lax.broadcasted_iota(jnp.int32, logits.shape, 1).astype(f32)
    neg = -jnp.inf
    lg = jnp.where(lane < N_EXPERTS, logits, neg)
    m1 = jnp.max(lg, axis=-1, keepdims=True)
    i1 = jnp.min(jnp.where(lg == m1, lane, float(LANES)), axis=-1, keepdims=True)
    lg2 = jnp.where(lane == i1, neg, lg)
    m2 = jnp.max(lg2, axis=-1, keepdims=True)
    i2 = jnp.min(jnp.where(lg2 == m2, lane, float(LANES)), axis=-1, keepdims=True)
    e2 = jnp.exp(m2 - m1)
    w1 = 1.0 / (1.0 + e2)
    w2 = e2 / (1.0 + e2)
    return jnp.where(lane == 0, i1, jnp.where(lane == 1, i2, jnp.where(lane == 2, w1, jnp.where(lane == 3, w2, 0.0))))


def _mix_out_kernel(*refs, routed):
    if routed:
        (ya_ref, yb_ref, ga_ref, gb_ref, x_ref, wa_ref, wb_ref, wo_ref, gpost_ref, gpre_ref, rw_ref,
         xo_ref, ho_ref, route_ref) = refs
    else:
        (ya_ref, yb_ref, ga_ref, gb_ref, x_ref, wa_ref, wb_ref, wo_ref, gpost_ref, gpre_ref,
         xo_ref, ho_ref) = refs
    a = jnp.dot(ya_ref[...], wa_ref[...], preferred_element_type=f32)
    b = jnp.dot(yb_ref[...], wb_ref[...], preferred_element_type=f32)
    merged = _sigmoid(ga_ref[...].astype(f32)) * a + _sigmoid(gb_ref[...].astype(f32)) * b
    y = jnp.dot(merged.astype(bf16), wo_ref[...], preferred_element_type=f32)
    x1 = x_ref[...] + _rms(y, gpost_ref[...])
    h = _rms(x1, gpre_ref[...])
    xo_ref[...] = x1
    if routed:
        _store_row_tiles(ho_ref, h)
        route_ref[...] = _route_top2(h, rw_ref)
    else:
        ho_ref[...] = h.astype(ho_ref.dtype)


def _mix_out(ya, yb, proj, x, wa, wb, wo, g_post, g_pre, router_w=None, tm=512):
    t = x.shape[0]
    routed = router_w is not None
    row = lambda i: (i, 0)
    in_specs = [pl.BlockSpec((tm, A_WIDTH), row),
                pl.BlockSpec((tm, B_V_WIDTH), row),
                pl.BlockSpec((tm, D_MODEL), lambda i: (i, COL_GA // D_MODEL)),
                pl.BlockSpec((tm, D_MODEL), lambda i: (i, COL_GB // D_MODEL)),
                pl.BlockSpec((tm, D_MODEL), row),
                _const_spec((A_WIDTH, D_MODEL)),
                _const_spec((B_V_WIDTH, D_MODEL)),
                _const_spec((D_MODEL, D_MODEL)),
                _const_spec((1, D_MODEL)),
                _const_spec((1, D_MODEL))]
    args = [ya, yb, proj, proj, x, wa, wb, wo, g_post, g_pre]
    if routed:
        out_specs = [pl.BlockSpec((tm, D_MODEL), row), pl.BlockSpec((tm * ROW_TILE, LANES), row)]
        out_shape = [jax.ShapeDtypeStruct((t, D_MODEL), f32), jax.ShapeDtypeStruct((t * ROW_TILE, LANES), f32)]
    else:
        out_specs = [pl.BlockSpec((tm, D_MODEL), row), pl.BlockSpec((tm, D_MODEL), row)]
        out_shape = [jax.ShapeDtypeStruct((t, D_MODEL), f32), jax.ShapeDtypeStruct((t, D_MODEL), bf16)]
    if routed:
        in_specs.append(_const_spec((D_MODEL, LANES)))
        args.append(router_w)
        out_specs.append(pl.BlockSpec((tm, LANES), row))
        out_shape.append(jax.ShapeDtypeStruct((t, LANES), f32))
    return pl.pallas_call(
        functools.partial(_mix_out_kernel, routed=routed),
        grid=(t // tm,),
        in_specs=in_specs,
        out_specs=out_specs,
        out_shape=out_shape,
        compiler_params=_params("parallel"),
        name="mix_out",
    )(*args)


def _post_ffn(y, x1, p, gpost_ref, wpp_ref, wpg_ref, gple_ref):
    x2 = x1 + _rms(y, gpost_ref[...])
    e = jnp.dot(p.astype(bf16), wpp_ref[...], preferred_element_type=f32)
    e = e * _sigmoid(jnp.dot(x2.astype(bf16), wpg_ref[...], preferred_element_type=f32))
    return x2 + _rms(e, gple_ref[...])


def _swiglu_chunks(x, wg_ref, wu_ref, wd_ref, lead, width):
    acc = None
    for c0, c1 in _chunks(width, FF_CHUNK):
        g = jnp.dot(x, wg_ref[lead + (slice(None), slice(c0, c1))], preferred_element_type=f32)
        u = jnp.dot(x, wu_ref[lead + (slice(None), slice(c0, c1))], preferred_element_type=f32)
        act = (_silu(g) * u).astype(bf16)
        part = jnp.dot(act, wd_ref[lead + (slice(c0, c1), slice(None))], preferred_element_type=f32)
        acc = part if acc is None else acc + part
    return acc


def _dense_ffn_kernel(*refs, n_side):
    (h_ref, x_ref, p_ref, wg_ref, wu_ref, wd_ref, gpost_ref, wpp_ref, wpg_ref, gple_ref) = refs[:10]
    o_ref = refs[10 + n_side]
    finish = _side_cast(pl.program_id(0), pl.num_programs(0), refs[10:10 + n_side],
                        refs[11 + n_side:11 + 2 * n_side], refs[11 + 2 * n_side:])
    y = _swiglu_chunks(h_ref[...], wg_ref, wu_ref, wd_ref, (), wg_ref.shape[1])
    o_ref[...] = _post_ffn(y, x_ref[...], p_ref[...], gpost_ref, wpp_ref, wpg_ref, gple_ref)
    finish()


def _dense_ffn(h, x1, p, p_blk, wg, wu, wd, g_post, wpp, wpg, g_ple, side=(), tm=FFN_TM):
    t = x1.shape[0]
    ff = wg.shape[1]
    steps = t // tm
    row = lambda i: (i, 0)
    hbm = pl.BlockSpec(memory_space=pl.ANY)
    n_side = len(side)
    outs = pl.pallas_call(
        functools.partial(_dense_ffn_kernel, n_side=n_side),
        grid=(steps,),
        in_specs=[pl.BlockSpec((tm, D_MODEL), row),
                  pl.BlockSpec((tm, D_MODEL), row),
                  pl.BlockSpec((tm, PLE_DIM), lambda i: (i + p_blk, 0)),
                  _const_spec((D_MODEL, ff)),
                  _const_spec((D_MODEL, ff)),
                  _const_spec((ff, D_MODEL)),
                  _const_spec((1, D_MODEL)),
                  _const_spec((PLE_DIM, D_MODEL)),
                  _const_spec((D_MODEL, D_MODEL)),
                  _const_spec((1, D_MODEL))] + [hbm] * n_side,
        out_specs=[pl.BlockSpec((tm, D_MODEL), row)] + [hbm] * n_side,
        out_shape=[jax.ShapeDtypeStruct((t, D_MODEL), f32)] + [jax.ShapeDtypeStruct(m.shape, bf16) for m in side],
        scratch_shapes=_side_cast_scratch(side, steps),
        compiler_params=pltpu.CompilerParams(dimension_semantics=("arbitrary",), vmem_limit_bytes=VMEM_LIMIT_BIG),
        name="dense_ffn",
    )(h, x1, p, wg, wu, wd, g_post, wpp, wpg, g_ple, *side)
    return outs[0], tuple(outs[1:])


def _scatter_kernel(fill_ref, pos_ref, h_ref, xs_out, zbuf, sem, zsem):
    rows = h_ref.shape[0] // ROW_TILE

    @pl.when(pl.program_id(0) == 0)
    def _():
        zbuf[...] = jnp.zeros_like(zbuf)

        def zero_rows(first_row):
            start = pl.multiple_of(first_row * ROW_TILE, ROW_TILE)
            fill = pltpu.make_async_copy(zbuf, xs_out.at[pl.ds(start, MOE_TM * ROW_TILE)], zsem)
            fill.start()
            fill.wait()

        for e in range(N_EXPERTS):
            zero_rows(fill_ref[e])

        def zero_tile(tile, c):
            zero_rows(tile * MOE_TM)
            return c

        lax.fori_loop(fill_ref[N_EXPERTS], xs_out.shape[0] // (MOE_TM * ROW_TILE), zero_tile, 0)

    def issue(r, c):
        for k in range(2):
            pltpu.make_async_copy(_tile_rows(h_ref, r), _tile_rows(xs_out, pos_ref[2 * r + k]),
                                  sem.at[k]).start(priority=k)
        return c

    lax.fori_loop(0, rows, issue, 0, unroll=ISSUE_UNROLL)
    for k in range(2):
        pltpu.make_async_copy(h_ref, xs_out.at[pl.ds(0, rows * ROW_TILE)], sem.at[k]).wait()


def _scatter_rows(fill_plan, pos, h, n_rows):
    t = h.shape[0] // ROW_TILE
    return pl.pallas_call(
        _scatter_kernel,
        grid_spec=pltpu.PrefetchScalarGridSpec(
            num_scalar_prefetch=1,
            grid=(t // ROW_BLK,),
            in_specs=[pl.BlockSpec((2 * ROW_BLK,), lambda i, fill: (i,), memory_space=pltpu.SMEM),
                      pl.BlockSpec((ROW_BLK * ROW_TILE, LANES), lambda i, fill: (i, 0))],
            out_specs=pl.BlockSpec(memory_space=pl.ANY),
            scratch_shapes=[pltpu.VMEM((MOE_TM * ROW_TILE, LANES), f32),
                            pltpu.SemaphoreType.DMA((2,)),
                            pltpu.SemaphoreType.DMA(())]),
        out_shape=jax.ShapeDtypeStruct((n_rows * ROW_TILE, LANES), f32),
        compiler_params=_params("arbitrary"),
        name="moe_scatter",
    )(fill_plan, pos, h)


def _expert_kernel(te_ref, nu_ref, fill_ref, x_ref, wg_ref, wu_ref, wd_ref, o_ref, acc_ref):
    del te_ref
    i = pl.program_id(0)
    f = pl.program_id(1)
    used = i < nu_ref[0]
    half = MOE_TM // 2
    sparse = fill_ref[i] <= half
    last = f == pl.num_programs(1) - 1

    def run(rows):
        x = _load_row_tiles(x_ref, rows).astype(bf16)
        y = _swiglu_chunks(x, wg_ref, wu_ref, wd_ref, (0,), wg_ref.shape[2])

        @pl.when(f == 0)
        def _():
            acc_ref[0:rows] = y

        @pl.when(f > 0)
        def _():
            acc_ref[0:rows] += y

        @pl.when(last)
        def _():
            _store_row_tiles(o_ref, acc_ref[0:rows])

    @pl.when(jnp.logical_and(used, jnp.logical_not(sparse)))
    def _():
        run(MOE_TM)

    @pl.when(jnp.logical_and(used, sparse))
    def _():
        run(half)

        @pl.when(last)
        def _():
            o_ref[half * ROW_TILE:, :] = jnp.zeros(((MOE_TM - half) * ROW_TILE, LANES), o_ref.dtype)

    @pl.when(jnp.logical_and(last, jnp.logical_not(used)))
    def _():
        o_ref[...] = jnp.zeros_like(o_ref)


def _expert_ffn(tile_expert, n_used, tile_fill, xs, wg, wu, wd, tf=1792):
    n_rows = xs.shape[0] // ROW_TILE
    ff = wg.shape[2]
    nf = ff // tf

    def f_idx(i, f, nu):
        return jnp.where(i < nu[0], f, nf - 1)

    return pl.pallas_call(
        _expert_kernel,
        grid_spec=pltpu.PrefetchScalarGridSpec(
            num_scalar_prefetch=3,
            grid=(n_rows // MOE_TM, nf),
            in_specs=[pl.BlockSpec((MOE_TM * ROW_TILE, LANES), lambda i, f, te, nu, fill: (i, 0)),
                      pl.BlockSpec((1, D_MODEL, tf), lambda i, f, te, nu, fill: (te[i], 0, f_idx(i, f, nu))),
                      pl.BlockSpec((1, D_MODEL, tf), lambda i, f, te, nu, fill: (te[i], 0, f_idx(i, f, nu))),
                      pl.BlockSpec((1, tf, D_MODEL), lambda i, f, te, nu, fill: (te[i], f_idx(i, f, nu), 0))],
            out_specs=pl.BlockSpec((MOE_TM * ROW_TILE, LANES), lambda i, f, te, nu, fill: (i, 0)),
            scratch_shapes=[pltpu.VMEM((MOE_TM, D_MODEL), f32)]),
        out_shape=jax.ShapeDtypeStruct((n_rows * ROW_TILE, LANES), f32),
        compiler_params=_params("arbitrary", "arbitrary"),
        name="moe_experts",
    )(tile_expert, n_used, tile_fill, xs, wg, wu, wd)


def _combine_kernel(pos0_ref, pos_next_ref, ys_hbm, route_ref, x_ref, p_ref, gpost_ref, wpp_ref, wpg_ref, gple_ref,
                    o_ref, buf, sem):
    i = pl.program_id(0)
    rows = x_ref.shape[0]
    slot = i % 2

    def fetch(pos_ref, into):
        def issue(r, c):
            for k in range(2):
                pltpu.make_async_copy(_tile_rows(ys_hbm, pos_ref[2 * r + k]), _tile_rows(buf.at[into, k], r),
                                      sem.at[into, k]).start(priority=k)
            return c

        lax.fori_loop(0, rows, issue, 0, unroll=ISSUE_UNROLL)

    @pl.when(i == 0)
    def _():
        fetch(pos0_ref, 0)

    @pl.when(i + 1 < pl.num_programs(0))
    def _():
        fetch(pos_next_ref, 1 - slot)

    for k in range(2):
        pltpu.make_async_copy(ys_hbm.at[pl.ds(0, rows * ROW_TILE)], buf.at[slot, k], sem.at[slot, k]).wait()
    route = route_ref[...]
    y = (route[:, 2:3] * _load_row_tiles(buf.at[slot, 0], rows)
         + route[:, 3:4] * _load_row_tiles(buf.at[slot, 1], rows))
    o_ref[...] = _post_ffn(y, x_ref[...], p_ref[...], gpost_ref, wpp_ref, wpg_ref, gple_ref)


def _combine(pos, ys, route, x1, p, p_blk, g_post, wpp, wpg, g_ple):
    t = x1.shape[0]
    n_blk = t // ROW_BLK
    row = lambda i: (i, 0)
    smem = functools.partial(pl.BlockSpec, (2 * ROW_BLK,), memory_space=pltpu.SMEM)
    return pl.pallas_call(
        _combine_kernel,
        grid=(n_blk,),
        in_specs=[smem(lambda i: (0,)),
                  smem(lambda i: (jnp.minimum(i + 1, n_blk - 1),)),
                  pl.BlockSpec(memory_space=pl.ANY),
                  pl.BlockSpec((ROW_BLK, LANES), row),
                  pl.BlockSpec((ROW_BLK, D_MODEL), row),
                  pl.BlockSpec((ROW_BLK, PLE_DIM), lambda i: (i + p_blk, 0)),
                  _const_spec((1, D_MODEL)),
                  _const_spec((PLE_DIM, D_MODEL)),
                  _const_spec((D_MODEL, D_MODEL)),
                  _const_spec((1, D_MODEL))],
        out_specs=pl.BlockSpec((ROW_BLK, D_MODEL), row),
        out_shape=jax.ShapeDtypeStruct((t, D_MODEL), f32),
        scratch_shapes=[pltpu.VMEM((2, 2, ROW_BLK * ROW_TILE, LANES), f32),
                        pltpu.SemaphoreType.DMA((2, 2))],
        compiler_params=_params("arbitrary"),
        name="moe_combine",
    )(pos, pos, ys, route, x1, p, g_post, wpp, wpg, g_ple)


def _route_slots(route, n_tiles):
    ids = route[:, 0:2].astype(jnp.int32).reshape(-1)
    onehot = (ids[None, :] == jnp.arange(N_EXPERTS, dtype=jnp.int32)[:, None]).astype(jnp.int32)
    csum = jnp.cumsum(onehot, axis=1)
    rank = jnp.sum(onehot * csum, axis=0) - 1
    counts = csum[:, -1]
    padded = ((counts + MOE_TM - 1) // MOE_TM) * MOE_TM
    ends = jnp.cumsum(padded)
    pos = (ends - padded)[ids] + rank
    tile_start = jnp.arange(n_tiles, dtype=jnp.int32) * MOE_TM
    tile_expert = jnp.minimum(jnp.sum((tile_start[:, None] >= ends[None, :]).astype(jnp.int32), axis=1),
                              N_EXPERTS - 1)
    n_used = (ends[-1] // MOE_TM).reshape(1)
    fill_plan = jnp.concatenate([ends - padded + counts, n_used])
    tile_fill = jnp.clip((ends - padded + counts)[tile_expert] - tile_start, 0, MOE_TM)
    return (pos.astype(jnp.int32), tile_expert.astype(jnp.int32), n_used.astype(jnp.int32),
            fill_plan.astype(jnp.int32), tile_fill.astype(jnp.int32))


def _moe(h, route, x1, p, p_blk, wg, wu, wd, g_post, wpp, wpg, g_ple):
    t = x1.shape[0]
    n_tiles = 2 * t // MOE_TM + N_EXPERTS
    pos, tile_expert, n_used, fill_plan, tile_fill = _route_slots(route, n_tiles)
    xs = _scatter_rows(fill_plan, pos, h, n_tiles * MOE_TM)
    ys = _expert_ffn(tile_expert, n_used, tile_fill, xs, wg, wu, wd)
    return _combine(pos, ys, route, x1, p, p_blk, g_post, wpp, wpg, g_ple)


def _prep_w_in(w):
    main = jnp.concatenate([w[:, SRC_KA:SRC_VA], w[:, SRC_QB:SRC_GD], w[:, SRC_R:SRC_END]], axis=1)
    wq_t = w[:, SRC_QA:SRC_KA].T
    wv_t = w[:, SRC_VA:SRC_QB].T
    w_gd = jnp.pad(w[:, SRC_GD:SRC_R], ((0, 0), (0, LANES - GATE_RANK)))
    return main.astype(bf16), wq_t.astype(bf16), wv_t.astype(bf16), w_gd.astype(bf16)


def kernel(x, p, w_in, rel_bias, w_gla_gate_up, b_gla_gate, gla_norm, w_branch_a, w_branch_b, w_out, norm_mix_pre, norm_mix_post, norm_ffn_pre, norm_ffn_post, ffn_w_gate, ffn_w_up, ffn_w_down, router_w, moe_w_gate, moe_w_up, moe_w_down, ple_w_proj, ple_w_gate, ple_norm):
    b, s, d = x.shape
    t = b * s
    depth = w_in.shape[0]
    x = x.reshape(t, d)
    vec = lambda a: a.reshape(1, -1).astype(f32)
    moe_bf16 = None
    p = p.reshape(depth * t, PLE_DIM)
    for i in range(depth):
        w_main, wq_t, wv_t, w_gd = _prep_w_in(w_in[i])
        h_in, qt, vt, gd = _qv_proj(x, vec(norm_mix_pre[i]), wq_t, wv_t, w_gd, b, s)
        routed = i % 2 == 1
        nxt = (i + 1) // 2
        flat = lambda w: w[nxt].reshape(-1, w.shape[-1])
        ahead = (not routed) and i + 1 < depth
        proj, _ = _in_proj(h_in, w_main)
        proj3 = proj.reshape(b, s, MAIN_WIDTH)
        ya = _attention(qt, proj3, vt, _attn_bias(rel_bias[i])).reshape(t, A_WIDTH)
        w_up = jnp.pad(w_gla_gate_up[i], ((0, LANES - GATE_RANK), (0, 0))).astype(bf16)
        yb = _gla(proj3, gd.reshape(b, s, LANES), w_up, vec(b_gla_gate[i]), vec(gla_norm[i])).reshape(t, B_V_WIDTH)
        j = i // 2
        rw = jnp.pad(router_w[j], ((0, 0), (0, LANES - N_EXPERTS))).astype(bf16) if routed else None
        outs = _mix_out(ya, yb, proj, x, w_branch_a[i].astype(bf16), w_branch_b[i].astype(bf16),
                        w_out[i].astype(bf16), vec(norm_mix_post[i]), vec(norm_ffn_pre[i]), rw)
        tail = (vec(norm_ffn_post[i]), ple_w_proj[i].astype(bf16), ple_w_gate[i].astype(bf16), vec(ple_norm[i]))
        if routed:
            x1, h, route = outs
            if moe_bf16 is None:
                moe_bf16 = tuple(w[j].astype(bf16) for w in (moe_w_gate, moe_w_up, moe_w_down))
            x = _moe(h, route, x1, p, i * t // ROW_BLK, *moe_bf16, *tail)
            moe_bf16 = None
        else:
            x1, h = outs
            x, cast = _dense_ffn(h, x1, p, i * t // FFN_TM, ffn_w_gate[j].astype(bf16), ffn_w_up[j].astype(bf16),
                                 ffn_w_down[j].astype(bf16), *tail,
                                 side=(flat(moe_w_gate), flat(moe_w_up)) if ahead else ())
            if ahead:
                moe_bf16 = tuple(c.reshape(w.shape[1:]) for c, w in zip(cast, (moe_w_gate, moe_w_up)))
                moe_bf16 += (moe_w_down[nxt].astype(bf16),)
    return x.reshape(b, s, d)
```

```python
import functools

import numpy as np
import jax
import jax.numpy as jnp
from jax import lax
from jax.experimental import pallas as pl
from jax.experimental.pallas import tpu as pltpu

f32 = jnp.float32
bf16 = jnp.bfloat16

D_MODEL = 1024
CHUNK = 64
N_PREV_CHUNKS = 8
A_HEADS = 8
A_HEAD_DIM = 64
A_WIDTH = A_HEADS * A_HEAD_DIM
REL_CLIP = 128
B_HEADS = 4
B_KEY_DIM = 64
B_VAL_DIM = 128
B_K_WIDTH = B_HEADS * B_KEY_DIM
B_V_WIDTH = B_HEADS * B_VAL_DIM
GATE_RANK = 16
GATE_TEMP = 16.0
N_EXPERTS = 8
PLE_DIM = 256
NORM_EPS = 1e-6

LANES = 128
VMEM_LIMIT = 56 * 1024 * 1024
VMEM_LIMIT_BIG = 62 * 1024 * 1024

SRC_QA, SRC_KA, SRC_VA, SRC_QB, SRC_GD, SRC_R = 0, 512, 1024, 1536, 2560, 2576
SRC_END = 5136
COL_KA, COL_QB, COL_KB, COL_VB, COL_R, COL_GA, COL_GB = 0, 512, 768, 1024, 1536, 2048, 3072
MAIN_WIDTH = 4096
IN_TILE_N = 2048

QBLK = 2 * CHUNK
KBAND = (N_PREV_CHUNKS + 2) * CHUNK
KPAD = N_PREV_CHUNKS * CHUNK

SUBLANES = 8
ROW_TILE = D_MODEL // LANES
assert ROW_TILE == SUBLANES
MOE_TM = 512
ROW_BLK = 512
FFN_TM = 512
FF_CHUNK = 512
ISSUE_UNROLL = 8
GLA_NB = 2


def _params(*sem):
    return pltpu.CompilerParams(dimension_semantics=sem, vmem_limit_bytes=VMEM_LIMIT)


def _rms(x, g):
    return x * lax.rsqrt(jnp.mean(x * x, axis=-1, keepdims=True) + NORM_EPS) * g


def _sigmoid(x):
    return 1.0 / (1.0 + jnp.exp(-x))


def _silu(x):
    return x * _sigmoid(x)


def _chunks(n, c):
    return [(s, min(s + c, n)) for s in range(0, n, c)]


def _store_row_tiles(ref, x):
    rows = x.shape[0]
    for c in range(ROW_TILE):
        ref[pl.ds(c, rows, stride=ROW_TILE), :] = x[:, c * LANES:(c + 1) * LANES].astype(ref.dtype)


def _load_row_tiles(ref, rows):
    return jnp.concatenate([ref[pl.ds(c, rows, stride=ROW_TILE), :] for c in range(ROW_TILE)], axis=1)


def _tile_rows(ref, r):
    return ref.at[pl.ds(pl.multiple_of(r * ROW_TILE, ROW_TILE), ROW_TILE)]


def _const_spec(shape):
    nd = len(shape)
    return pl.BlockSpec(shape, lambda *_: (0,) * nd, pipeline_mode=pl.Buffered(1))


def _side_cast_scratch(side, steps):
    slabs = [(m.shape[0] // steps, m.shape[1]) for m in side]
    assert all(r * steps == m.shape[0] for (r, _), m in zip(slabs, side))
    scratch = [pltpu.VMEM((2,) + sl, f32) for sl in slabs] + [pltpu.VMEM(sl, bf16) for sl in slabs]
    if side:
        scratch += [pltpu.SemaphoreType.DMA((len(side), 2)), pltpu.SemaphoreType.DMA((len(side),))]
    return scratch


def _side_cast(step, n_steps, side_in, side_out, scratch):
    n_side = len(side_in)
    if not n_side:
        return lambda: None
    stage_in, stage_out = scratch[:n_side], scratch[n_side:2 * n_side]
    sem_in, sem_out = scratch[2 * n_side], scratch[2 * n_side + 1]
    slot = step % 2

    def slab(ref, k, at):
        rows = stage_out[k].shape[0]
        return ref.at[pl.ds(pl.multiple_of(at * rows, rows), rows)]

    def fetch(at, into):
        for k in range(n_side):
            pltpu.make_async_copy(slab(side_in[k], k, at), stage_in[k].at[into], sem_in.at[k, into]).start()

    @pl.when(step == 0)
    def _():
        fetch(0, 0)

    @pl.when(step + 1 < n_steps)
    def _():
        fetch(step + 1, 1 - slot)

    for k in range(n_side):
        pltpu.make_async_copy(slab(side_in[k], k, step), stage_in[k].at[slot], sem_in.at[k, slot]).wait()

    @pl.when(step > 0)
    def _():
        for k in range(n_side):
            pltpu.make_async_copy(stage_out[k], slab(side_out[k], k, step - 1), sem_out.at[k]).wait()

    for k in range(n_side):
        stage_out[k][...] = stage_in[k][slot].astype(stage_out[k].dtype)
    for k in range(n_side):
        pltpu.make_async_copy(stage_out[k], slab(side_out[k], k, step), sem_out.at[k]).start()

    def finish():
        @pl.when(step == n_steps - 1)
        def _():
            for k in range(n_side):
                pltpu.make_async_copy(stage_out[k], slab(side_out[k], k, step), sem_out.at[k]).wait()

    return finish


def _qv_proj_kernel(x_ref, g_ref, wq_ref, wv_ref, wgd_ref, h_ref, qt_ref, vt_ref, gd_ref):
    nt = (((1,), (1,)), ((), ()))
    h = _rms(x_ref[...], g_ref[...]).astype(bf16)
    h_ref[...] = h
    qt_ref[0] = lax.dot_general(wq_ref[...], h, nt, preferred_element_type=f32).astype(qt_ref.dtype)
    vt = lax.dot_general(wv_ref[...], h, nt, preferred_element_type=f32).astype(vt_ref.dtype)
    for kt in range(vt_ref.shape[1]):
        vt_ref[0, kt] = vt[:, kt * LANES:(kt + 1) * LANES]
    gd_ref[...] = jnp.dot(h, wgd_ref[...], preferred_element_type=f32).astype(gd_ref.dtype)


def _qv_proj(x, g, wq_t, wv_t, w_gd, b, s, tm=1024):
    t = x.shape[0]
    per_b = s // tm
    return pl.pallas_call(
        _qv_proj_kernel,
        grid=(t // tm,),
        in_specs=[pl.BlockSpec((tm, D_MODEL), lambda i: (i, 0)),
                  _const_spec((1, D_MODEL)),
                  _const_spec((A_WIDTH, D_MODEL)),
                  _const_spec((A_WIDTH, D_MODEL)),
                  _const_spec((D_MODEL, LANES))],
        out_specs=[pl.BlockSpec((tm, D_MODEL), lambda i: (i, 0)),
                   pl.BlockSpec((1, A_WIDTH, tm), lambda i: (i // per_b, 0, i % per_b)),
                   pl.BlockSpec((1, tm // LANES, A_WIDTH, LANES), lambda i: (i // per_b, i % per_b, 0, 0)),
                   pl.BlockSpec((tm, LANES), lambda i: (i, 0))],
        out_shape=[jax.ShapeDtypeStruct((t, D_MODEL), bf16),
                   jax.ShapeDtypeStruct((b, A_WIDTH, s), bf16),
                   jax.ShapeDtypeStruct((b, s // LANES, A_WIDTH, LANES), bf16),
                   jax.ShapeDtypeStruct((t, LANES), bf16)],
        compiler_params=_params("parallel"),
        name="qv_proj",
    )(x, g, wq_t, wv_t, w_gd)


def _in_proj_kernel(*refs, n_side):
    h_ref, w_ref = refs[:2]
    o_ref = refs[2 + n_side]
    step = pl.program_id(0) * pl.num_programs(1) + pl.program_id(1)
    finish = _side_cast(step, pl.num_programs(0) * pl.num_programs(1), refs[2:2 + n_side],
                        refs[3 + n_side:3 + 2 * n_side], refs[3 + 2 * n_side:])
    o_ref[...] = jnp.dot(h_ref[...], w_ref[...], preferred_element_type=f32).astype(o_ref.dtype)
    finish()


def _in_proj(h, w, side=(), tm=1024):
    t = h.shape[0]
    n = w.shape[1]
    grid = (t // tm, n // IN_TILE_N)
    hbm = pl.BlockSpec(memory_space=pl.ANY)
    outs = pl.pallas_call(
        functools.partial(_in_proj_kernel, n_side=len(side)),
        grid=grid,
        in_specs=[pl.BlockSpec((tm, D_MODEL), lambda i, j: (i, 0)),
                  pl.BlockSpec((D_MODEL, IN_TILE_N), lambda i, j: (0, j))] + [hbm] * len(side),
        out_specs=[pl.BlockSpec((tm, IN_TILE_N), lambda i, j: (i, j))] + [hbm] * len(side),
        out_shape=[jax.ShapeDtypeStruct((t, n), bf16)] + [jax.ShapeDtypeStruct(m.shape, bf16) for m in side],
        scratch_shapes=_side_cast_scratch(side, grid[0] * grid[1]),
        compiler_params=_params("arbitrary", "arbitrary"),
        name="in_proj",
    )(h, w, *side)
    return outs[0], tuple(outs[1:])


def _attn_kernel(*refs, n_side):
    qt_ref, k_ref, vt_ref, bias_ref = refs[:4]
    o_ref = refs[4 + n_side]
    step = pl.program_id(0) * pl.num_programs(1) + pl.program_id(1)
    finish = _side_cast(step, pl.num_programs(0) * pl.num_programs(1), refs[4:4 + n_side],
                        refs[5 + n_side:5 + 2 * n_side], refs[5 + 2 * n_side:])
    i = pl.program_id(1)
    row = lax.broadcasted_iota(jnp.int32, (LANES, QBLK), 0)
    upper = row >= A_HEAD_DIM
    n_kt = KBAND // LANES
    kt_seq = [i + kt - KPAD // LANES for kt in range(n_kt)]
    kt_clamped = [jnp.maximum(j, 0) for j in kt_seq]
    for hp in range(A_WIDTH // LANES):
        pair = slice(hp * LANES, (hp + 1) * LANES)
        qt = qt_ref[0, pair, :] * (A_HEAD_DIM ** -0.5)
        kps = [k_ref[0, pl.ds(pl.multiple_of(j * LANES, LANES), LANES), pair] for j in kt_clamped]
        outs = []
        for s in range(2):
            h = 2 * hp + s
            qm = jnp.where(upper == bool(s), qt, jnp.zeros_like(qt))
            tiles = []
            for kt in range(n_kt):
                t = jnp.dot(kps[kt], qm, preferred_element_type=f32) + bias_ref[h, kt]
                tiles.append(jnp.where(kt_seq[kt] >= 0, t, -1e30))
            m = tiles[0]
            for t in tiles[1:]:
                m = jnp.maximum(m, t)
            m = jnp.max(m, axis=0, keepdims=True)
            l = None
            o = None
            for kt in range(n_kt):
                p = jnp.exp(tiles[kt] - m)
                ls = jnp.sum(p, axis=0, keepdims=True)
                vt = vt_ref[0, kt_clamped[kt], h * A_HEAD_DIM:(h + 1) * A_HEAD_DIM, :]
                part = jnp.dot(vt, p.astype(bf16), preferred_element_type=f32)
                l = ls if l is None else l + ls
                o = part if o is None else o + part
            outs.append(o / l)
        o_ref[0, :, pair] = jnp.concatenate(outs, axis=0).T.astype(o_ref.dtype)
    finish()


def _attention(qt, proj3, vt, bias_t, side=()):
    b, s, _ = proj3.shape
    grid = (b, s // QBLK)
    hbm = pl.BlockSpec(memory_space=pl.ANY)
    outs = pl.pallas_call(
        functools.partial(_attn_kernel, n_side=len(side)),
        grid=grid,
        in_specs=[pl.BlockSpec((1, A_WIDTH, QBLK), lambda bi, i: (bi, 0, i)),
                  pl.BlockSpec((1, s, A_WIDTH), lambda bi, i: (bi, 0, COL_KA // A_WIDTH)),
                  pl.BlockSpec((1, s // LANES, A_WIDTH, LANES), lambda bi, i: (bi, 0, 0, 0)),
                  _const_spec((A_HEADS, KBAND // LANES, LANES, QBLK))] + [hbm] * len(side),
        out_specs=[pl.BlockSpec((1, QBLK, A_WIDTH), lambda bi, i: (bi, i, 0))] + [hbm] * len(side),
        out_shape=[jax.ShapeDtypeStruct((b, s, A_WIDTH), bf16)] + [jax.ShapeDtypeStruct(m.shape, bf16) for m in side],
        scratch_shapes=_side_cast_scratch(side, grid[0] * grid[1]),
        compiler_params=_params("arbitrary", "arbitrary"),
        name="chunk_attention",
    )(qt, proj3, vt, bias_t, *side)
    return outs[0], tuple(outs[1:])


def _attn_bias(rel_bias):
    h = rel_bias.shape[0]
    r = np.arange(QBLK)[:, None]
    k = np.arange(KBAND)[None, :]
    visible = np.where(r < CHUNK, k < KBAND - CHUNK, k >= CHUNK)
    span = QBLK + KBAND - 1
    n_far = span - 2 * REL_CLIP
    line = jnp.concatenate([jnp.broadcast_to(rel_bias[:, -1:], (h, n_far)), rel_bias[:, :0:-1]], axis=1)
    line = jnp.pad(line.astype(f32), ((0, 0), (0, 1)))
    skew = jnp.tile(line, (1, QBLK))[:, :QBLK * span].reshape(h, QBLK, span)
    bias = jnp.where(visible[None], skew[:, :, QBLK - 1:], -1e30)
    return jnp.swapaxes(bias, 1, 2).reshape(h, KBAND // LANES, LANES, QBLK)


def _gla_kernel(q_ref, k_ref, v_ref, gd_ref, r_ref, wup_ref, b_ref, gn_ref, o_ref, state):
    @pl.when(pl.program_id(1) == 0)
    def _():
        state[...] = jnp.zeros_like(state)

    nb, tq = q_ref.shape[0], q_ref.shape[1]
    tn = (((0,), (0,)), ((), ()))
    nt = (((1,), (1,)), ((), ()))
    row = lax.broadcasted_iota(jnp.int32, (tq, tq), 0)
    col = lax.broadcasted_iota(jnp.int32, (tq, tq), 1)
    later = jnp.logical_and(col > row, col // CHUNK == row // CHUNK)
    later = jnp.where(later, 1.0, 0.0).astype(bf16)
    log_a, kdec = [], []
    for bb in range(nb):
        z = jnp.dot(gd_ref[bb], wup_ref[...], preferred_element_type=f32) + b_ref[...]
        la = (jnp.minimum(z, 0.0) - jnp.log(1.0 + jnp.exp(-jnp.abs(z)))) / GATE_TEMP
        la_hi = la.astype(bf16)
        la_lo = (la - la_hi.astype(f32)).astype(bf16)
        rev = (jnp.dot(later, la_hi, preferred_element_type=f32)
               + jnp.dot(later, la_lo, preferred_element_type=f32))
        log_a.append(la)
        kdec.append((k_ref[bb].astype(f32) * jnp.exp(rev)).astype(bf16))
    hrow = lax.broadcasted_iota(jnp.int32, (B_HEADS * CHUNK, B_K_WIDTH), 0) // CHUNK
    hlane = lax.broadcasted_iota(jnp.int32, (B_HEADS * CHUNK, B_K_WIDTH), 1) // B_KEY_DIM
    own = hrow == hlane
    st = [state[bb] for bb in range(nb)]
    for c in range(tq // CHUNK):
        rows = slice(c * CHUNK, (c + 1) * CHUNK)
        for bb in range(nb):
            decay = jnp.exp(jnp.sum(log_a[bb][rows], axis=0, keepdims=True))
            kd = kdec[bb][rows]
            kd4 = jnp.where(own, jnp.concatenate([kd] * B_HEADS, axis=0), jnp.zeros((), bf16))
            v4 = jnp.concatenate([v_ref[bb, rows, h * B_VAL_DIM:(h + 1) * B_VAL_DIM] for h in range(B_HEADS)],
                                 axis=0)
            d_state = lax.dot_general(v4, kd4, tn, preferred_element_type=f32)
            st[bb] = decay * st[bb] + d_state
            qc = q_ref[bb, rows, :] * (B_KEY_DIM ** -0.5)
            q4 = jnp.where(own, jnp.concatenate([qc] * B_HEADS, axis=0), jnp.zeros((), bf16))
            o4 = lax.dot_general(q4, st[bb].astype(bf16), nt, preferred_element_type=f32)
            for h in range(B_HEADS):
                vcols = slice(h * B_VAL_DIM, (h + 1) * B_VAL_DIM)
                o = _rms(o4[h * CHUNK:(h + 1) * CHUNK], gn_ref[:, vcols])
                r = r_ref[bb, rows, vcols].astype(f32)
                o_ref[bb, rows, vcols] = (o * _silu(r)).astype(o_ref.dtype)
    for bb in range(nb):
        state[bb] = st[bb]


def _gla(proj3, gd3, w_up, b_gate, gla_norm, tq=256, nb=GLA_NB):
    b, s, _ = proj3.shape
    return pl.pallas_call(
        _gla_kernel,
        grid=(b // nb, s // tq),
        in_specs=[pl.BlockSpec((nb, tq, B_K_WIDTH), lambda bi, i: (bi, i, COL_QB // B_K_WIDTH)),
                  pl.BlockSpec((nb, tq, B_K_WIDTH), lambda bi, i: (bi, i, COL_KB // B_K_WIDTH)),
                  pl.BlockSpec((nb, tq, B_V_WIDTH), lambda bi, i: (bi, i, COL_VB // B_V_WIDTH)),
                  pl.BlockSpec((nb, tq, LANES), lambda bi, i: (bi, i, 0)),
                  pl.BlockSpec((nb, tq, B_V_WIDTH), lambda bi, i: (bi, i, COL_R // B_V_WIDTH)),
                  _const_spec((LANES, B_K_WIDTH)),
                  _const_spec((1, B_K_WIDTH)),
                  _const_spec((1, B_V_WIDTH))],
        out_specs=pl.BlockSpec((nb, tq, B_V_WIDTH), lambda bi, i: (bi, i, 0)),
        out_shape=jax.ShapeDtypeStruct((b, s, B_V_WIDTH), bf16),
        scratch_shapes=[pltpu.VMEM((nb, B_VAL_DIM, B_K_WIDTH), f32)],
        compiler_params=_params("parallel", "arbitrary"),
        name="gla",
    )(proj3, proj3, proj3, gd3, proj3, w_up, b_gate, gla_norm)


def _route_top2(h, rw_ref):
    logits = jnp.dot(h.astype(bf16), rw_ref[...], preferred_element_type=f32)
    lane = lax.broadcasted_iota(jnp.int32, logits.shape, 1).astype(f32)
    neg = -jnp.inf
    lg = jnp.where(lane < N_EXPERTS, logits, neg)
    m1 = jnp.max(lg, axis=-1, keepdims=True)
    i1 = jnp.min(jnp.where(lg == m1, lane, float(LANES)), axis=-1, keepdims=True)
    lg2 = jnp.where(lane == i1, neg, lg)
    m2 = jnp.max(lg2, axis=-1, keepdims=True)
    i2 = jnp.min(jnp.where(lg2 == m2, lane, float(LANES)), axis=-1, keepdims=True)
    e2 = jnp.exp(m2 - m1)
    w1 = 1.0 / (1.0 + e2)
    w2 = e2 / (1.0 + e2)
    return jnp.where(lane == 0, i1, jnp.where(lane == 1, i2, jnp.where(lane == 2, w1, jnp.where(lane == 3, w2, 0.0))))


def _mix_out_kernel(*refs, routed):
    if routed:
        (ya_ref, yb_ref, ga_ref, gb_ref, x_ref, wa_ref, wb_ref, wo_ref, gpost_ref, gpre_ref, rw_ref,
         xo_ref, ho_ref, route_ref) = refs
    else:
        (ya_ref, yb_ref, ga_ref, gb_ref, x_ref, wa_ref, wb_ref, wo_ref, gpost_ref, gpre_ref,
         xo_ref, ho_ref) = refs
    a = jnp.dot(ya_ref[...], wa_ref[...], preferred_element_type=f32)
    b = jnp.dot(yb_ref[...], wb_ref[...], preferred_element_type=f32)
    merged = _sigmoid(ga_ref[...].astype(f32)) * a + _sigmoid(gb_ref[...].astype(f32)) * b
    y = jnp.dot(merged.astype(bf16), wo_ref[...], preferred_element_type=f32)
    x1 = x_ref[...] + _rms(y, gpost_ref[...])
    h = _rms(x1, gpre_ref[...])
    xo_ref[...] = x1
    if routed:
        _store_row_tiles(ho_ref, h)
        route_ref[...] = _route_top2(h, rw_ref)
    else:
        ho_ref[...] = h.astype(ho_ref.dtype)


def _mix_out(ya, yb, proj, x, wa, wb, wo, g_post, g_pre, router_w=None, tm=512):
    t = x.shape[0]
    routed = router_w is not None
    row = lambda i: (i, 0)
    in_specs = [pl.BlockSpec((tm, A_WIDTH), row),
                pl.BlockSpec((tm, B_V_WIDTH), row),
                pl.BlockSpec((tm, D_MODEL), lambda i: (i, COL_GA // D_MODEL)),
                pl.BlockSpec((tm, D_MODEL), lambda i: (i, COL_GB // D_MODEL)),
                pl.BlockSpec((tm, D_MODEL), row),
                _const_spec((A_WIDTH, D_MODEL)),
                _const_spec((B_V_WIDTH, D_MODEL)),
                _const_spec((D_MODEL, D_MODEL)),
                _const_spec((1, D_MODEL)),
                _const_spec((1, D_MODEL))]
    args = [ya, yb, proj, proj, x, wa, wb, wo, g_post, g_pre]
    if routed:
        out_specs = [pl.BlockSpec((tm, D_MODEL), row), pl.BlockSpec((tm * ROW_TILE, LANES), row)]
        out_shape = [jax.ShapeDtypeStruct((t, D_MODEL), f32), jax.ShapeDtypeStruct((t * ROW_TILE, LANES), f32)]
    else:
        out_specs = [pl.BlockSpec((tm, D_MODEL), row), pl.BlockSpec((tm, D_MODEL), row)]
        out_shape = [jax.ShapeDtypeStruct((t, D_MODEL), f32), jax.ShapeDtypeStruct((t, D_MODEL), bf16)]
    if routed:
        in_specs.append(_const_spec((D_MODEL, LANES)))
        args.append(router_w)
        out_specs.append(pl.BlockSpec((tm, LANES), row))
        out_shape.append(jax.ShapeDtypeStruct((t, LANES), f32))
    return pl.pallas_call(
        functools.partial(_mix_out_kernel, routed=routed),
        grid=(t // tm,),
        in_specs=in_specs,
        out_specs=out_specs,
        out_shape=out_shape,
        compiler_params=_params("parallel"),
        name="mix_out",
    )(*args)


def _post_ffn(y, x1, p, gpost_ref, wpp_ref, wpg_ref, gple_ref):
    x2 = x1 + _rms(y, gpost_ref[...])
    e = jnp.dot(p.astype(bf16), wpp_ref[...], preferred_element_type=f32)
    e = e * _sigmoid(jnp.dot(x2.astype(bf16), wpg_ref[...], preferred_element_type=f32))
    return x2 + _rms(e, gple_ref[...])


def _swiglu_chunks(x, wg_ref, wu_ref, wd_ref, lead, width):
    acc = None
    for c0, c1 in _chunks(width, FF_CHUNK):
        g = jnp.dot(x, wg_ref[lead + (slice(None), slice(c0, c1))], preferred_element_type=f32)
        u = jnp.dot(x, wu_ref[lead + (slice(None), slice(c0, c1))], preferred_element_type=f32)
        act = (_silu(g) * u).astype(bf16)
        part = jnp.dot(act, wd_ref[lead + (slice(c0, c1), slice(None))], preferred_element_type=f32)
        acc = part if acc is None else acc + part
    return acc


def _dense_ffn_kernel(*refs, n_side):
    (h_ref, x_ref, p_ref, wg_ref, wu_ref, wd_ref, gpost_ref, wpp_ref, wpg_ref, gple_ref) = refs[:10]
    o_ref = refs[10 + n_side]
    finish = _side_cast(pl.program_id(0), pl.num_programs(0), refs[10:10 + n_side],
                        refs[11 + n_side:11 + 2 * n_side], refs[11 + 2 * n_side:])
    y = _swiglu_chunks(h_ref[...], wg_ref, wu_ref, wd_ref, (), wg_ref.shape[1])
    o_ref[...] = _post_ffn(y, x_ref[...], p_ref[...], gpost_ref, wpp_ref, wpg_ref, gple_ref)
    finish()


def _dense_ffn(h, x1, p, p_blk, wg, wu, wd, g_post, wpp, wpg, g_ple, side=(), tm=FFN_TM):
    t = x1.shape[0]
    ff = wg.shape[1]
    steps = t // tm
    row = lambda i: (i, 0)
    hbm = pl.BlockSpec(memory_space=pl.ANY)
    n_side = len(side)
    outs = pl.pallas_call(
        functools.partial(_dense_ffn_kernel, n_side=n_side),
        grid=(steps,),
        in_specs=[pl.BlockSpec((tm, D_MODEL), row),
                  pl.BlockSpec((tm, D_MODEL), row),
                  pl.BlockSpec((tm, PLE_DIM), lambda i: (i + p_blk, 0)),
                  _const_spec((D_MODEL, ff)),
                  _const_spec((D_MODEL, ff)),
                  _const_spec((ff, D_MODEL)),
                  _const_spec((1, D_MODEL)),
                  _const_spec((PLE_DIM, D_MODEL)),
                  _const_spec((D_MODEL, D_MODEL)),
                  _const_spec((1, D_MODEL))] + [hbm] * n_side,
        out_specs=[pl.BlockSpec((tm, D_MODEL), row)] + [hbm] * n_side,
        out_shape=[jax.ShapeDtypeStruct((t, D_MODEL), f32)] + [jax.ShapeDtypeStruct(m.shape, bf16) for m in side],
        scratch_shapes=_side_cast_scratch(side, steps),
        compiler_params=pltpu.CompilerParams(dimension_semantics=("arbitrary",), vmem_limit_bytes=VMEM_LIMIT_BIG),
        name="dense_ffn",
    )(h, x1, p, wg, wu, wd, g_post, wpp, wpg, g_ple, *side)
    return outs[0], tuple(outs[1:])


def _scatter_kernel(fill_ref, pos_ref, h_ref, xs_out, zbuf, sem, zsem):
    rows = h_ref.shape[0] // ROW_TILE

    @pl.when(pl.program_id(0) == 0)
    def _():
        zbuf[...] = jnp.zeros_like(zbuf)

        def zero_rows(first_row):
            start = pl.multiple_of(first_row * ROW_TILE, ROW_TILE)
            fill = pltpu.make_async_copy(zbuf, xs_out.at[pl.ds(start, MOE_TM * ROW_TILE)], zsem)
            fill.start()
            fill.wait()

        for e in range(N_EXPERTS):
            zero_rows(fill_ref[e])

        def zero_tile(tile, c):
            zero_rows(tile * MOE_TM)
            return c

        lax.fori_loop(fill_ref[N_EXPERTS], xs_out.shape[0] // (MOE_TM * ROW_TILE), zero_tile, 0)

    def issue(r, c):
        for k in range(2):
            pltpu.make_async_copy(_tile_rows(h_ref, r), _tile_rows(xs_out, pos_ref[2 * r + k]),
                                  sem.at[k]).start(priority=k)
        return c

    lax.fori_loop(0, rows, issue, 0, unroll=ISSUE_UNROLL)
    for k in range(2):
        pltpu.make_async_copy(h_ref, xs_out.at[pl.ds(0, rows * ROW_TILE)], sem.at[k]).wait()


def _scatter_rows(fill_plan, pos, h, n_rows):
    t = h.shape[0] // ROW_TILE
    return pl.pallas_call(
        _scatter_kernel,
        grid_spec=pltpu.PrefetchScalarGridSpec(
            num_scalar_prefetch=1,
            grid=(t // ROW_BLK,),
            in_specs=[pl.BlockSpec((2 * ROW_BLK,), lambda i, fill: (i,), memory_space=pltpu.SMEM),
                      pl.BlockSpec((ROW_BLK * ROW_TILE, LANES), lambda i, fill: (i, 0))],
            out_specs=pl.BlockSpec(memory_space=pl.ANY),
            scratch_shapes=[pltpu.VMEM((MOE_TM * ROW_TILE, LANES), f32),
                            pltpu.SemaphoreType.DMA((2,)),
                            pltpu.SemaphoreType.DMA(())]),
        out_shape=jax.ShapeDtypeStruct((n_rows * ROW_TILE, LANES), f32),
        compiler_params=_params("arbitrary"),
        name="moe_scatter",
    )(fill_plan, pos, h)


def _expert_kernel(te_ref, nu_ref, fill_ref, x_ref, wg_ref, wu_ref, wd_ref, o_ref, acc_ref):
    del te_ref
    i = pl.program_id(0)
    f = pl.program_id(1)
    used = i < nu_ref[0]
    half = MOE_TM // 2
    sparse = fill_ref[i] <= half
    last = f == pl.num_programs(1) - 1

    def run(rows):
        x = _load_row_tiles(x_ref, rows).astype(bf16)
        y = _swiglu_chunks(x, wg_ref, wu_ref, wd_ref, (0,), wg_ref.shape[2])

        @pl.when(f == 0)
        def _():
            acc_ref[0:rows] = y

        @pl.when(f > 0)
        def _():
            acc_ref[0:rows] += y

        @pl.when(last)
        def _():
            _store_row_tiles(o_ref, acc_ref[0:rows])

    @pl.when(jnp.logical_and(used, jnp.logical_not(sparse)))
    def _():
        run(MOE_TM)

    @pl.when(jnp.logical_and(used, sparse))
    def _():
        run(half)

        @pl.when(last)
        def _():
            o_ref[half * ROW_TILE:, :] = jnp.zeros(((MOE_TM - half) * ROW_TILE, LANES), o_ref.dtype)

    @pl.when(jnp.logical_and(last, jnp.logical_not(used)))
    def _():
        o_ref[...] = jnp.zeros_like(o_ref)


def _expert_ffn(tile_expert, n_used, tile_fill, xs, wg, wu, wd, tf=1792):
    n_rows = xs.shape[0] // ROW_TILE
    ff = wg.shape[2]
    nf = ff // tf

    def f_idx(i, f, nu):
        return jnp.where(i < nu[0], f, nf - 1)

    return pl.pallas_call(
        _expert_kernel,
        grid_spec=pltpu.PrefetchScalarGridSpec(
            num_scalar_prefetch=3,
            grid=(n_rows // MOE_TM, nf),
            in_specs=[pl.BlockSpec((MOE_TM * ROW_TILE, LANES), lambda i, f, te, nu, fill: (i, 0)),
                      pl.BlockSpec((1, D_MODEL, tf), lambda i, f, te, nu, fill: (te[i], 0, f_idx(i, f, nu))),
                      pl.BlockSpec((1, D_MODEL, tf), lambda i, f, te, nu, fill: (te[i], 0, f_idx(i, f, nu))),
                      pl.BlockSpec((1, tf, D_MODEL), lambda i, f, te, nu, fill: (te[i], f_idx(i, f, nu), 0))],
            out_specs=pl.BlockSpec((MOE_TM * ROW_TILE, LANES), lambda i, f, te, nu, fill: (i, 0)),
            scratch_shapes=[pltpu.VMEM((MOE_TM, D_MODEL), f32)]),
        out_shape=jax.ShapeDtypeStruct((n_rows * ROW_TILE, LANES), f32),
        compiler_params=_params("arbitrary", "arbitrary"),
        name="moe_experts",
    )(tile_expert, n_used, tile_fill, xs, wg, wu, wd)


def _combine_kernel(pos0_ref, pos_next_ref, ys_hbm, route_ref, x_ref, p_ref, gpost_ref, wpp_ref, wpg_ref, gple_ref,
                    o_ref, buf, sem):
    i = pl.program_id(0)
    rows = x_ref.shape[0]
    slot = i % 2

    def fetch(pos_ref, into):
        def issue(r, c):
            for k in range(2):
                pltpu.make_async_copy(_tile_rows(ys_hbm, pos_ref[2 * r + k]), _tile_rows(buf.at[into, k], r),
                                      sem.at[into, k]).start(priority=k)
            return c

        lax.fori_loop(0, rows, issue, 0, unroll=ISSUE_UNROLL)

    @pl.when(i == 0)
    def _():
        fetch(pos0_ref, 0)

    @pl.when(i + 1 < pl.num_programs(0))
    def _():
        fetch(pos_next_ref, 1 - slot)

    for k in range(2):
        pltpu.make_async_copy(ys_hbm.at[pl.ds(0, rows * ROW_TILE)], buf.at[slot, k], sem.at[slot, k]).wait()
    route = route_ref[...]
    y = (route[:, 2:3] * _load_row_tiles(buf.at[slot, 0], rows)
         + route[:, 3:4] * _load_row_tiles(buf.at[slot, 1], rows))
    o_ref[...] = _post_ffn(y, x_ref[...], p_ref[...], gpost_ref, wpp_ref, wpg_ref, gple_ref)


def _combine(pos, ys, route, x1, p, p_blk, g_post, wpp, wpg, g_ple):
    t = x1.shape[0]
    n_blk = t // ROW_BLK
    row = lambda i: (i, 0)
    smem = functools.partial(pl.BlockSpec, (2 * ROW_BLK,), memory_space=pltpu.SMEM)
    return pl.pallas_call(
        _combine_kernel,
        grid=(n_blk,),
        in_specs=[smem(lambda i: (0,)),
                  smem(lambda i: (jnp.minimum(i + 1, n_blk - 1),)),
                  pl.BlockSpec(memory_space=pl.ANY),
                  pl.BlockSpec((ROW_BLK, LANES), row),
                  pl.BlockSpec((ROW_BLK, D_MODEL), row),
                  pl.BlockSpec((ROW_BLK, PLE_DIM), lambda i: (i + p_blk, 0)),
                  _const_spec((1, D_MODEL)),
                  _const_spec((PLE_DIM, D_MODEL)),
                  _const_spec((D_MODEL, D_MODEL)),
                  _const_spec((1, D_MODEL))],
        out_specs=pl.BlockSpec((ROW_BLK, D_MODEL), row),
        out_shape=jax.ShapeDtypeStruct((t, D_MODEL), f32),
        scratch_shapes=[pltpu.VMEM((2, 2, ROW_BLK * ROW_TILE, LANES), f32),
                        pltpu.SemaphoreType.DMA((2, 2))],
        compiler_params=_params("arbitrary"),
        name="moe_combine",
    )(pos, pos, ys, route, x1, p, g_post, wpp, wpg, g_ple)


def _route_slots(route, n_tiles):
    ids = route[:, 0:2].astype(jnp.int32).reshape(-1)
    onehot = (ids[None, :] == jnp.arange(N_EXPERTS, dtype=jnp.int32)[:, None]).astype(jnp.int32)
    csum = jnp.cumsum(onehot, axis=1)
    rank = jnp.sum(onehot * csum, axis=0) - 1
    counts = csum[:, -1]
    padded = ((counts + MOE_TM - 1) // MOE_TM) * MOE_TM
    ends = jnp.cumsum(padded)
    pos = (ends - padded)[ids] + rank
    tile_start = jnp.arange(n_tiles, dtype=jnp.int32) * MOE_TM
    tile_expert = jnp.minimum(jnp.sum((tile_start[:, None] >= ends[None, :]).astype(jnp.int32), axis=1),
                              N_EXPERTS - 1)
    n_used = (ends[-1] // MOE_TM).reshape(1)
    fill_plan = jnp.concatenate([ends - padded + counts, n_used])
    tile_fill = jnp.clip((ends - padded + counts)[tile_expert] - tile_start, 0, MOE_TM)
    return (pos.astype(jnp.int32), tile_expert.astype(jnp.int32), n_used.astype(jnp.int32),
            fill_plan.astype(jnp.int32), tile_fill.astype(jnp.int32))


def _moe(h, route, x1, p, p_blk, wg, wu, wd, g_post, wpp, wpg, g_ple):
    t = x1.shape[0]
    n_tiles = 2 * t // MOE_TM + N_EXPERTS
    pos, tile_expert, n_used, fill_plan, tile_fill = _route_slots(route, n_tiles)
    xs = _scatter_rows(fill_plan, pos, h, n_tiles * MOE_TM)
    ys = _expert_ffn(tile_expert, n_used, tile_fill, xs, wg, wu, wd)
    return _combine(pos, ys, route, x1, p, p_blk, g_post, wpp, wpg, g_ple)


def _prep_w_in(w):
    main = jnp.concatenate([w[:, SRC_KA:SRC_VA], w[:, SRC_QB:SRC_GD], w[:, SRC_R:SRC_END]], axis=1)
    wq_t = w[:, SRC_QA:SRC_KA].T
    wv_t = w[:, SRC_VA:SRC_QB].T
    w_gd = jnp.pad(w[:, SRC_GD:SRC_R], ((0, 0), (0, LANES - GATE_RANK)))
    return main.astype(bf16), wq_t.astype(bf16), wv_t.astype(bf16), w_gd.astype(bf16)


def kernel(x, p, w_in, rel_bias, w_gla_gate_up, b_gla_gate, gla_norm, w_branch_a, w_branch_b, w_out, norm_mix_pre, norm_mix_post, norm_ffn_pre, norm_ffn_post, ffn_w_gate, ffn_w_up, ffn_w_down, router_w, moe_w_gate, moe_w_up, moe_w_down, ple_w_proj, ple_w_gate, ple_norm):
    b, s, d = x.shape
    t = b * s
    depth = w_in.shape[0]
    x = x.reshape(t, d)
    vec = lambda a: a.reshape(1, -1).astype(f32)
    moe_bf16 = None
    p = p.reshape(depth * t, PLE_DIM)
    for i in range(depth):
        w_main, wq_t, wv_t, w_gd = _prep_w_in(w_in[i])
        h_in, qt, vt, gd = _qv_proj(x, vec(norm_mix_pre[i]), wq_t, wv_t, w_gd, b, s)
        routed = i % 2 == 1
        nxt = (i + 1) // 2
        flat = lambda w: w[nxt].reshape(-1, w.shape[-1])
        ahead = (not routed) and i + 1 < depth
        proj, _ = _in_proj(h_in, w_main)
        proj3 = proj.reshape(b, s, MAIN_WIDTH)
        ya, down_bf16 = _attention(qt, proj3, vt, _attn_bias(rel_bias[i]),
                                   side=(flat(moe_w_down),) if ahead else ())
        ya = ya.reshape(t, A_WIDTH)
        w_up = jnp.pad(w_gla_gate_up[i], ((0, LANES - GATE_RANK), (0, 0))).astype(bf16)
        yb = _gla(proj3, gd.reshape(b, s, LANES), w_up, vec(b_gla_gate[i]), vec(gla_norm[i])).reshape(t, B_V_WIDTH)
        j = i // 2
        rw = jnp.pad(router_w[j], ((0, 0), (0, LANES - N_EXPERTS))).astype(bf16) if routed else None
        outs = _mix_out(ya, yb, proj, x, w_branch_a[i].astype(bf16), w_branch_b[i].astype(bf16),
                        w_out[i].astype(bf16), vec(norm_mix_post[i]), vec(norm_ffn_pre[i]), rw)
        tail = (vec(norm_ffn_post[i]), ple_w_proj[i].astype(bf16), ple_w_gate[i].astype(bf16), vec(ple_norm[i]))
        if routed:
            x1, h, route = outs
            if moe_bf16 is None:
                moe_bf16 = tuple(w[j].astype(bf16) for w in (moe_w_gate, moe_w_up, moe_w_down))
            x = _moe(h, route, x1, p, i * t // ROW_BLK, *moe_bf16, *tail)
            moe_bf16 = None
        else:
            x1, h = outs
            x, cast = _dense_ffn(h, x1, p, i * t // FFN_TM, ffn_w_gate[j].astype(bf16), ffn_w_up[j].astype(bf16),
                                 ffn_w_down[j].astype(bf16), *tail,
                                 side=(flat(moe_w_gate), flat(moe_w_up)) if ahead else ())
            if ahead:
                moe_bf16 = tuple(c.reshape(w.shape[1:])
                                 for c, w in zip(cast + down_bf16, (moe_w_gate, moe_w_up, moe_w_down)))
    return x.reshape(b, s, d)
```

```python
import functools

import numpy as np
import jax
import jax.numpy as jnp
from jax import lax
from jax.experimental import pallas as pl
from jax.experimental.pallas import tpu as pltpu

f32 = jnp.float32
bf16 = jnp.bfloat16

D_MODEL = 1024
CHUNK = 64
N_PREV_CHUNKS = 8
A_HEADS = 8
A_HEAD_DIM = 64
A_WIDTH = A_HEADS * A_HEAD_DIM
REL_CLIP = 128
B_HEADS = 4
B_KEY_DIM = 64
B_VAL_DIM = 128
B_K_WIDTH = B_HEADS * B_KEY_DIM
B_V_WIDTH = B_HEADS * B_VAL_DIM
GATE_RANK = 16
GATE_TEMP = 16.0
N_EXPERTS = 8
PLE_DIM = 256
NORM_EPS = 1e-6

LANES = 128
VMEM_LIMIT = 56 * 1024 * 1024
VMEM_LIMIT_BIG = 62 * 1024 * 1024

SRC_QA, SRC_KA, SRC_VA, SRC_QB, SRC_GD, SRC_R = 0, 512, 1024, 1536, 2560, 2576
SRC_END = 5136
COL_KA, COL_QB, COL_KB, COL_VB, COL_R, COL_GA, COL_GB = 0, 512, 768, 1024, 1536, 2048, 3072
MAIN_WIDTH = 4096
IN_TILE_N = 2048

QBLK = 2 * CHUNK
KBAND = (N_PREV_CHUNKS + 2) * CHUNK
KPAD = N_PREV_CHUNKS * CHUNK

SUBLANES = 8
ROW_TILE = D_MODEL // LANES
assert ROW_TILE == SUBLANES
MOE_TM = 512
ROW_BLK = 512
FFN_TM = 512
FF_CHUNK = 512
ISSUE_UNROLL = 8
GLA_NB = 4


def _params(*sem):
    return pltpu.CompilerParams(dimension_semantics=sem, vmem_limit_bytes=VMEM_LIMIT)


def _rms(x, g):
    return x * lax.rsqrt(jnp.mean(x * x, axis=-1, keepdims=True) + NORM_EPS) * g


def _sigmoid(x):
    return 1.0 / (1.0 + jnp.exp(-x))


def _silu(x):
    return x * _sigmoid(x)


def _chunks(n, c):
    return [(s, min(s + c, n)) for s in range(0, n, c)]


def _store_row_tiles(ref, x):
    rows = x.shape[0]
    for c in range(ROW_TILE):
        ref[pl.ds(c, rows, stride=ROW_TILE), :] = x[:, c * LANES:(c + 1) * LANES].astype(ref.dtype)


def _load_row_tiles(ref, rows):
    return jnp.concatenate([ref[pl.ds(c, rows, stride=ROW_TILE), :] for c in range(ROW_TILE)], axis=1)


def _tile_rows(ref, r):
    return ref.at[pl.ds(pl.multiple_of(r * ROW_TILE, ROW_TILE), ROW_TILE)]


def _const_spec(shape):
    nd = len(shape)
    return pl.BlockSpec(shape, lambda *_: (0,) * nd, pipeline_mode=pl.Buffered(1))


def _side_cast_scratch(side, steps):
    slabs = [(m.shape[0] // steps, m.shape[1]) for m in side]
    assert all(r * steps == m.shape[0] for (r, _), m in zip(slabs, side))
    scratch = [pltpu.VMEM((2,) + sl, f32) for sl in slabs] + [pltpu.VMEM(sl, bf16) for sl in slabs]
    if side:
        scratch += [pltpu.SemaphoreType.DMA((len(side), 2)), pltpu.SemaphoreType.DMA((len(side),))]
    return scratch


def _side_cast(step, n_steps, side_in, side_out, scratch):
    n_side = len(side_in)
    if not n_side:
        return lambda: None
    stage_in, stage_out = scratch[:n_side], scratch[n_side:2 * n_side]
    sem_in, sem_out = scratch[2 * n_side], scratch[2 * n_side + 1]
    slot = step % 2

    def slab(ref, k, at):
        rows = stage_out[k].shape[0]
        return ref.at[pl.ds(pl.multiple_of(at * rows, rows), rows)]

    def fetch(at, into):
        for k in range(n_side):
            pltpu.make_async_copy(slab(side_in[k], k, at), stage_in[k].at[into], sem_in.at[k, into]).start()

    @pl.when(step == 0)
    def _():
        fetch(0, 0)

    @pl.when(step + 1 < n_steps)
    def _():
        fetch(step + 1, 1 - slot)

    for k in range(n_side):
        pltpu.make_async_copy(slab(side_in[k], k, step), stage_in[k].at[slot], sem_in.at[k, slot]).wait()

    @pl.when(step > 0)
    def _():
        for k in range(n_side):
            pltpu.make_async_copy(stage_out[k], slab(side_out[k], k, step - 1), sem_out.at[k]).wait()

    for k in range(n_side):
        stage_out[k][...] = stage_in[k][slot].astype(stage_out[k].dtype)
    for k in range(n_side):
        pltpu.make_async_copy(stage_out[k], slab(side_out[k], k, step), sem_out.at[k]).start()

    def finish():
        @pl.when(step == n_steps - 1)
        def _():
            for k in range(n_side):
                pltpu.make_async_copy(stage_out[k], slab(side_out[k], k, step), sem_out.at[k]).wait()

    return finish


def _qv_proj_kernel(x_ref, g_ref, wq_ref, wv_ref, wgd_ref, h_ref, qt_ref, vt_ref, gd_ref):
    nt = (((1,), (1,)), ((), ()))
    h = _rms(x_ref[...], g_ref[...]).astype(bf16)
    h_ref[...] = h
    qt_ref[0] = lax.dot_general(wq_ref[...], h, nt, preferred_element_type=f32).astype(qt_ref.dtype)
    vt = lax.dot_general(wv_ref[...], h, nt, preferred_element_type=f32).astype(vt_ref.dtype)
    for kt in range(vt_ref.shape[1]):
        vt_ref[0, kt] = vt[:, kt * LANES:(kt + 1) * LANES]
    gd_ref[...] = jnp.dot(h, wgd_ref[...], preferred_element_type=f32).astype(gd_ref.dtype)


def _qv_proj(x, g, wq_t, wv_t, w_gd, b, s, tm=1024):
    t = x.shape[0]
    per_b = s // tm
    return pl.pallas_call(
        _qv_proj_kernel,
        grid=(t // tm,),
        in_specs=[pl.BlockSpec((tm, D_MODEL), lambda i: (i, 0)),
                  _const_spec((1, D_MODEL)),
                  _const_spec((A_WIDTH, D_MODEL)),
                  _const_spec((A_WIDTH, D_MODEL)),
                  _const_spec((D_MODEL, LANES))],
        out_specs=[pl.BlockSpec((tm, D_MODEL), lambda i: (i, 0)),
                   pl.BlockSpec((1, A_WIDTH, tm), lambda i: (i // per_b, 0, i % per_b)),
                   pl.BlockSpec((1, tm // LANES, A_WIDTH, LANES), lambda i: (i // per_b, i % per_b, 0, 0)),
                   pl.BlockSpec((tm, LANES), lambda i: (i, 0))],
        out_shape=[jax.ShapeDtypeStruct((t, D_MODEL), bf16),
                   jax.ShapeDtypeStruct((b, A_WIDTH, s), bf16),
                   jax.ShapeDtypeStruct((b, s // LANES, A_WIDTH, LANES), bf16),
                   jax.ShapeDtypeStruct((t, LANES), bf16)],
        compiler_params=_params("parallel"),
        name="qv_proj",
    )(x, g, wq_t, wv_t, w_gd)


def _in_proj_kernel(*refs, n_side):
    h_ref, w_ref = refs[:2]
    o_ref = refs[2 + n_side]
    step = pl.program_id(0) * pl.num_programs(1) + pl.program_id(1)
    finish = _side_cast(step, pl.num_programs(0) * pl.num_programs(1), refs[2:2 + n_side],
                        refs[3 + n_side:3 + 2 * n_side], refs[3 + 2 * n_side:])
    o_ref[...] = jnp.dot(h_ref[...], w_ref[...], preferred_element_type=f32).astype(o_ref.dtype)
    finish()


def _in_proj(h, w, side=(), tm=1024):
    t = h.shape[0]
    n = w.shape[1]
    grid = (t // tm, n // IN_TILE_N)
    hbm = pl.BlockSpec(memory_space=pl.ANY)
    outs = pl.pallas_call(
        functools.partial(_in_proj_kernel, n_side=len(side)),
        grid=grid,
        in_specs=[pl.BlockSpec((tm, D_MODEL), lambda i, j: (i, 0)),
                  pl.BlockSpec((D_MODEL, IN_TILE_N), lambda i, j: (0, j))] + [hbm] * len(side),
        out_specs=[pl.BlockSpec((tm, IN_TILE_N), lambda i, j: (i, j))] + [hbm] * len(side),
        out_shape=[jax.ShapeDtypeStruct((t, n), bf16)] + [jax.ShapeDtypeStruct(m.shape, bf16) for m in side],
        scratch_shapes=_side_cast_scratch(side, grid[0] * grid[1]),
        compiler_params=_params("arbitrary", "arbitrary"),
        name="in_proj",
    )(h, w, *side)
    return outs[0], tuple(outs[1:])


def _attn_kernel(qt_ref, k_ref, vt_ref, bias_ref, o_ref):
    i = pl.program_id(1)
    row = lax.broadcasted_iota(jnp.int32, (LANES, QBLK), 0)
    upper = row >= A_HEAD_DIM
    n_kt = KBAND // LANES
    kt_seq = [i + kt - KPAD // LANES for kt in range(n_kt)]
    kt_clamped = [jnp.maximum(j, 0) for j in kt_seq]
    for hp in range(A_WIDTH // LANES):
        pair = slice(hp * LANES, (hp + 1) * LANES)
        qt = qt_ref[0, pair, :] * (A_HEAD_DIM ** -0.5)
        kps = [k_ref[0, pl.ds(pl.multiple_of(j * LANES, LANES), LANES), pair] for j in kt_clamped]
        outs = []
        for s in range(2):
            h = 2 * hp + s
            qm = jnp.where(upper == bool(s), qt, jnp.zeros_like(qt))
            tiles = []
            for kt in range(n_kt):
                t = jnp.dot(kps[kt], qm, preferred_element_type=f32) + bias_ref[h, kt]
                tiles.append(jnp.where(kt_seq[kt] >= 0, t, -1e30))
            m = tiles[0]
            for t in tiles[1:]:
                m = jnp.maximum(m, t)
            m = jnp.max(m, axis=0, keepdims=True)
            l = None
            o = None
            for kt in range(n_kt):
                p = jnp.exp(tiles[kt] - m)
                ls = jnp.sum(p, axis=0, keepdims=True)
                vt = vt_ref[0, kt_clamped[kt], h * A_HEAD_DIM:(h + 1) * A_HEAD_DIM, :]
                part = jnp.dot(vt, p.astype(bf16), preferred_element_type=f32)
                l = ls if l is None else l + ls
                o = part if o is None else o + part
            outs.append(o / l)
        o_ref[0, :, pair] = jnp.concatenate(outs, axis=0).T.astype(o_ref.dtype)


def _attention(qt, proj3, vt, bias_t):
    b, s, _ = proj3.shape
    return pl.pallas_call(
        _attn_kernel,
        grid=(b, s // QBLK),
        in_specs=[pl.BlockSpec((1, A_WIDTH, QBLK), lambda bi, i: (bi, 0, i)),
                  pl.BlockSpec((1, s, A_WIDTH), lambda bi, i: (bi, 0, COL_KA // A_WIDTH)),
                  pl.BlockSpec((1, s // LANES, A_WIDTH, LANES), lambda bi, i: (bi, 0, 0, 0)),
                  _const_spec((A_HEADS, KBAND // LANES, LANES, QBLK))],
        out_specs=pl.BlockSpec((1, QBLK, A_WIDTH), lambda bi, i: (bi, i, 0)),
        out_shape=jax.ShapeDtypeStruct((b, s, A_WIDTH), bf16),
        compiler_params=_params("parallel", "arbitrary"),
        name="chunk_attention",
    )(qt, proj3, vt, bias_t)


def _attn_bias(rel_bias):
    h = rel_bias.shape[0]
    r = np.arange(QBLK)[:, None]
    k = np.arange(KBAND)[None, :]
    visible = np.where(r < CHUNK, k < KBAND - CHUNK, k >= CHUNK)
    span = QBLK + KBAND - 1
    n_far = span - 2 * REL_CLIP
    line = jnp.concatenate([jnp.broadcast_to(rel_bias[:, -1:], (h, n_far)), rel_bias[:, :0:-1]], axis=1)
    line = jnp.pad(line.astype(f32), ((0, 0), (0, 1)))
    skew = jnp.tile(line, (1, QBLK))[:, :QBLK * span].reshape(h, QBLK, span)
    bias = jnp.where(visible[None], skew[:, :, QBLK - 1:], -1e30)
    return jnp.swapaxes(bias, 1, 2).reshape(h, KBAND // LANES, LANES, QBLK)


def _gla_kernel(q_ref, k_ref, v_ref, gd_ref, r_ref, wup_ref, b_ref, gn_ref, o_ref, state):
    @pl.when(pl.program_id(1) == 0)
    def _():
        state[...] = jnp.zeros_like(state)

    nb, tq = q_ref.shape[0], q_ref.shape[1]
    tn = (((0,), (0,)), ((), ()))
    nt = (((1,), (1,)), ((), ()))
    row = lax.broadcasted_iota(jnp.int32, (tq, tq), 0)
    col = lax.broadcasted_iota(jnp.int32, (tq, tq), 1)
    later = jnp.logical_and(col > row, col // CHUNK == row // CHUNK)
    later = jnp.where(later, 1.0, 0.0).astype(bf16)
    log_a, kdec = [], []
    for bb in range(nb):
        z = jnp.dot(gd_ref[bb], wup_ref[...], preferred_element_type=f32) + b_ref[...]
        la = (jnp.minimum(z, 0.0) - jnp.log(1.0 + jnp.exp(-jnp.abs(z)))) / GATE_TEMP
        la_hi = la.astype(bf16)
        la_lo = (la - la_hi.astype(f32)).astype(bf16)
        rev = (jnp.dot(later, la_hi, preferred_element_type=f32)
               + jnp.dot(later, la_lo, preferred_element_type=f32))
        log_a.append(la)
        kdec.append((k_ref[bb].astype(f32) * jnp.exp(rev)).astype(bf16))
    hrow = lax.broadcasted_iota(jnp.int32, (B_HEADS * CHUNK, B_K_WIDTH), 0) // CHUNK
    hlane = lax.broadcasted_iota(jnp.int32, (B_HEADS * CHUNK, B_K_WIDTH), 1) // B_KEY_DIM
    own = hrow == hlane
    st = [state[bb] for bb in range(nb)]
    for c in range(tq // CHUNK):
        rows = slice(c * CHUNK, (c + 1) * CHUNK)
        for bb in range(nb):
            decay = jnp.exp(jnp.sum(log_a[bb][rows], axis=0, keepdims=True))
            kd = kdec[bb][rows]
            kd4 = jnp.where(own, jnp.concatenate([kd] * B_HEADS, axis=0), jnp.zeros((), bf16))
            v4 = jnp.concatenate([v_ref[bb, rows, h * B_VAL_DIM:(h + 1) * B_VAL_DIM] for h in range(B_HEADS)],
                                 axis=0)
            d_state = lax.dot_general(v4, kd4, tn, preferred_element_type=f32)
            st[bb] = decay * st[bb] + d_state
            qc = q_ref[bb, rows, :] * (B_KEY_DIM ** -0.5)
            q4 = jnp.where(own, jnp.concatenate([qc] * B_HEADS, axis=0), jnp.zeros((), bf16))
            o4 = lax.dot_general(q4, st[bb].astype(bf16), nt, preferred_element_type=f32)
            for h in range(B_HEADS):
                vcols = slice(h * B_VAL_DIM, (h + 1) * B_VAL_DIM)
                o = _rms(o4[h * CHUNK:(h + 1) * CHUNK], gn_ref[:, vcols])
                r = r_ref[bb, rows, vcols].astype(f32)
                o_ref[bb, rows, vcols] = (o * _silu(r)).astype(o_ref.dtype)
    for bb in range(nb):
        state[bb] = st[bb]


def _gla(proj3, gd3, w_up, b_gate, gla_norm, tq=256, nb=GLA_NB):
    b, s, _ = proj3.shape
    return pl.pallas_call(
        _gla_kernel,
        grid=(b // nb, s // tq),
        in_specs=[pl.BlockSpec((nb, tq, B_K_WIDTH), lambda bi, i: (bi, i, COL_QB // B_K_WIDTH)),
                  pl.BlockSpec((nb, tq, B_K_WIDTH), lambda bi, i: (bi, i, COL_KB // B_K_WIDTH)),
                  pl.BlockSpec((nb, tq, B_V_WIDTH), lambda bi, i: (bi, i, COL_VB // B_V_WIDTH)),
                  pl.BlockSpec((nb, tq, LANES), lambda bi, i: (bi, i, 0)),
                  pl.BlockSpec((nb, tq, B_V_WIDTH), lambda bi, i: (bi, i, COL_R // B_V_WIDTH)),
                  _const_spec((LANES, B_K_WIDTH)),
                  _const_spec((1, B_K_WIDTH)),
                  _const_spec((1, B_V_WIDTH))],
        out_specs=pl.BlockSpec((nb, tq, B_V_WIDTH), lambda bi, i: (bi, i, 0)),
        out_shape=jax.ShapeDtypeStruct((b, s, B_V_WIDTH), bf16),
        scratch_shapes=[pltpu.VMEM((nb, B_VAL_DIM, B_K_WIDTH), f32)],
        compiler_params=_params("parallel", "arbitrary"),
        name="gla",
    )(proj3, proj3, proj3, gd3, proj3, w_up, b_gate, gla_norm)


def _route_top2(h, rw_ref):
    logits = jnp.dot(h.astype(bf16), rw_ref[...], preferred_element_type=f32)
    lane = lax.broadcasted_iota(jnp.int32, logits.shape, 1).astype(f32)
    neg = -jnp.inf
    lg = jnp.where(lane < N_EXPERTS, logits, neg)
    m1 = jnp.max(lg, axis=-1, keepdims=True)
    i1 = jnp.min(jnp.where(lg == m1, lane, float(LANES)), axis=-1, keepdims=True)
    lg2 = jnp.where(lane == i1, neg, lg)
    m2 = jnp.max(lg2, axis=-1, keepdims=True)
    i2 = jnp.min(jnp.where(lg2 == m2, lane, float(LANES)), axis=-1, keepdims=True)
    e2 = jnp.exp(m2 - m1)
    w1 = 1.0 / (1.0 + e2)
    w2 = e2 / (1.0 + e2)
    return jnp.where(lane == 0, i1, jnp.where(lane == 1, i2, jnp.where(lane == 2, w1, jnp.where(lane == 3, w2, 0.0))))


def _mix_out_kernel(*refs, routed):
    if routed:
        (ya_ref, yb_ref, ga_ref, gb_ref, x_ref, wa_ref, wb_ref, wo_ref, gpost_ref, gpre_ref, rw_ref,
         xo_ref, ho_ref, route_ref) = refs
    else:
        (ya_ref, yb_ref, ga_ref, gb_ref, x_ref, wa_ref, wb_ref, wo_ref, gpost_ref, gpre_ref,
         xo_ref, ho_ref) = refs
    a = jnp.dot(ya_ref[...], wa_ref[...], preferred_element_type=f32)
    b = jnp.dot(yb_ref[...], wb_ref[...], preferred_element_type=f32)
    merged = _sigmoid(ga_ref[...].astype(f32)) * a + _sigmoid(gb_ref[...].astype(f32)) * b
    y = jnp.dot(merged.astype(bf16), wo_ref[...], preferred_element_type=f32)
    x1 = x_ref[...] + _rms(y, gpost_ref[...])
    h = _rms(x1, gpre_ref[...])
    xo_ref[...] = x1
    if routed:
        _store_row_tiles(ho_ref, h)
        route_ref[...] = _route_top2(h, rw_ref)
    else:
        ho_ref[...] = h.astype(ho_ref.dtype)


def _mix_out(ya, yb, proj, x, wa, wb, wo, g_post, g_pre, router_w=None, tm=512):
    t = x.shape[0]
    routed = router_w is not None
    row = lambda i: (i, 0)
    in_specs = [pl.BlockSpec((tm, A_WIDTH), row),
                pl.BlockSpec((tm, B_V_WIDTH), row),
                pl.BlockSpec((tm, D_MODEL), lambda i: (i, COL_GA // D_MODEL)),
                pl.BlockSpec((tm, D_MODEL), lambda i: (i, COL_GB // D_MODEL)),
                pl.BlockSpec((tm, D_MODEL), row),
                _const_spec((A_WIDTH, D_MODEL)),
                _const_spec((B_V_WIDTH, D_MODEL)),
                _const_spec((D_MODEL, D_MODEL)),
                _const_spec((1, D_MODEL)),
                _const_spec((1, D_MODEL))]
    args = [ya, yb, proj, proj, x, wa, wb, wo, g_post, g_pre]
    if routed:
        out_specs = [pl.BlockSpec((tm, D_MODEL), row), pl.BlockSpec((tm * ROW_TILE, LANES), row)]
        out_shape = [jax.ShapeDtypeStruct((t, D_MODEL), f32), jax.ShapeDtypeStruct((t * ROW_TILE, LANES), f32)]
    else:
        out_specs = [pl.BlockSpec((tm, D_MODEL), row), pl.BlockSpec((tm, D_MODEL), row)]
        out_shape = [jax.ShapeDtypeStruct((t, D_MODEL), f32), jax.ShapeDtypeStruct((t, D_MODEL), bf16)]
    if routed:
        in_specs.append(_const_spec((D_MODEL, LANES)))
        args.append(router_w)
        out_specs.append(pl.BlockSpec((tm, LANES), row))
        out_shape.append(jax.ShapeDtypeStruct((t, LANES), f32))
    return pl.pallas_call(
        functools.partial(_mix_out_kernel, routed=routed),
        grid=(t // tm,),
        in_specs=in_specs,
        out_specs=out_specs,
        out_shape=out_shape,
        compiler_params=_params("parallel"),
        name="mix_out",
    )(*args)


def _post_ffn(y, x1, p, gpost_ref, wpp_ref, wpg_ref, gple_ref):
    x2 = x1 + _rms(y, gpost_ref[...])
    e = jnp.dot(p.astype(bf16), wpp_ref[...], preferred_element_type=f32)
    e = e * _sigmoid(jnp.dot(x2.astype(bf16), wpg_ref[...], preferred_element_type=f32))
    return x2 + _rms(e, gple_ref[...])


def _swiglu_chunks(x, wg_ref, wu_ref, wd_ref, lead, width):
    acc = None
    for c0, c1 in _chunks(width, FF_CHUNK):
        g = jnp.dot(x, wg_ref[lead + (slice(None), slice(c0, c1))], preferred_element_type=f32)
        u = jnp.dot(x, wu_ref[lead + (slice(None), slice(c0, c1))], preferred_element_type=f32)
        act = (_silu(g) * u).astype(bf16)
        part = jnp.dot(act, wd_ref[lead + (slice(c0, c1), slice(None))], preferred_element_type=f32)
        acc = part if acc is None else acc + part
    return acc


def _dense_ffn_kernel(*refs, n_side):
    (h_ref, x_ref, p_ref, wg_ref, wu_ref, wd_ref, gpost_ref, wpp_ref, wpg_ref, gple_ref) = refs[:10]
    o_ref = refs[10 + n_side]
    finish = _side_cast(pl.program_id(0), pl.num_programs(0), refs[10:10 + n_side],
                        refs[11 + n_side:11 + 2 * n_side], refs[11 + 2 * n_side:])
    y = _swiglu_chunks(h_ref[...], wg_ref, wu_ref, wd_ref, (), wg_ref.shape[1])
    o_ref[...] = _post_ffn(y, x_ref[...], p_ref[...], gpost_ref, wpp_ref, wpg_ref, gple_ref)
    finish()


def _dense_ffn(h, x1, p, p_blk, wg, wu, wd, g_post, wpp, wpg, g_ple, side=(), tm=FFN_TM):
    t = x1.shape[0]
    ff = wg.shape[1]
    steps = t // tm
    row = lambda i: (i, 0)
    hbm = pl.BlockSpec(memory_space=pl.ANY)
    n_side = len(side)
    outs = pl.pallas_call(
        functools.partial(_dense_ffn_kernel, n_side=n_side),
        grid=(steps,),
        in_specs=[pl.BlockSpec((tm, D_MODEL), row),
                  pl.BlockSpec((tm, D_MODEL), row),
                  pl.BlockSpec((tm, PLE_DIM), lambda i: (i + p_blk, 0)),
                  _const_spec((D_MODEL, ff)),
                  _const_spec((D_MODEL, ff)),
                  _const_spec((ff, D_MODEL)),
                  _const_spec((1, D_MODEL)),
                  _const_spec((PLE_DIM, D_MODEL)),
                  _const_spec((D_MODEL, D_MODEL)),
                  _const_spec((1, D_MODEL))] + [hbm] * n_side,
        out_specs=[pl.BlockSpec((tm, D_MODEL), row)] + [hbm] * n_side,
        out_shape=[jax.ShapeDtypeStruct((t, D_MODEL), f32)] + [jax.ShapeDtypeStruct(m.shape, bf16) for m in side],
        scratch_shapes=_side_cast_scratch(side, steps),
        compiler_params=pltpu.CompilerParams(dimension_semantics=("arbitrary",), vmem_limit_bytes=VMEM_LIMIT_BIG),
        name="dense_ffn",
    )(h, x1, p, wg, wu, wd, g_post, wpp, wpg, g_ple, *side)
    return outs[0], tuple(outs[1:])


def _scatter_kernel(fill_ref, pos_ref, h_ref, xs_out, zbuf, sem, zsem):
    rows = h_ref.shape[0] // ROW_TILE

    @pl.when(pl.program_id(0) == 0)
    def _():
        zbuf[...] = jnp.zeros_like(zbuf)

        def zero_rows(first_row):
            start = pl.multiple_of(first_row * ROW_TILE, ROW_TILE)
            fill = pltpu.make_async_copy(zbuf, xs_out.at[pl.ds(start, MOE_TM * ROW_TILE)], zsem)
            fill.start()
            fill.wait()

        for e in range(N_EXPERTS):
            zero_rows(fill_ref[e])

        def zero_tile(tile, c):
            zero_rows(tile * MOE_TM)
            return c

        lax.fori_loop(fill_ref[N_EXPERTS], xs_out.shape[0] // (MOE_TM * ROW_TILE), zero_tile, 0)

    def issue(r, c):
        for k in range(2):
            pltpu.make_async_copy(_tile_rows(h_ref, r), _tile_rows(xs_out, pos_ref[2 * r + k]),
                                  sem.at[k]).start(priority=k)
        return c

    lax.fori_loop(0, rows, issue, 0, unroll=ISSUE_UNROLL)
    for k in range(2):
        pltpu.make_async_copy(h_ref, xs_out.at[pl.ds(0, rows * ROW_TILE)], sem.at[k]).wait()


def _scatter_rows(fill_plan, pos, h, n_rows):
    t = h.shape[0] // ROW_TILE
    return pl.pallas_call(
        _scatter_kernel,
        grid_spec=pltpu.PrefetchScalarGridSpec(
            num_scalar_prefetch=1,
            grid=(t // ROW_BLK,),
            in_specs=[pl.BlockSpec((2 * ROW_BLK,), lambda i, fill: (i,), memory_space=pltpu.SMEM),
                      pl.BlockSpec((ROW_BLK * ROW_TILE, LANES), lambda i, fill: (i, 0))],
            out_specs=pl.BlockSpec(memory_space=pl.ANY),
            scratch_shapes=[pltpu.VMEM((MOE_TM * ROW_TILE, LANES), f32),
                            pltpu.SemaphoreType.DMA((2,)),
                            pltpu.SemaphoreType.DMA(())]),
        out_shape=jax.ShapeDtypeStruct((n_rows * ROW_TILE, LANES), f32),
        compiler_params=_params("arbitrary"),
        name="moe_scatter",
    )(fill_plan, pos, h)


def _expert_kernel(te_ref, nu_ref, fill_ref, x_ref, wg_ref, wu_ref, wd_ref, o_ref, acc_ref):
    del te_ref
    i = pl.program_id(0)
    f = pl.program_id(1)
    used = i < nu_ref[0]
    half = MOE_TM // 2
    sparse = fill_ref[i] <= half
    last = f == pl.num_programs(1) - 1

    def run(rows):
        x = _load_row_tiles(x_ref, rows).astype(bf16)
        y = _swiglu_chunks(x, wg_ref, wu_ref, wd_ref, (0,), wg_ref.shape[2])

        @pl.when(f == 0)
        def _():
            acc_ref[0:rows] = y

        @pl.when(f > 0)
        def _():
            acc_ref[0:rows] += y

        @pl.when(last)
        def _():
            _store_row_tiles(o_ref, acc_ref[0:rows])

    @pl.when(jnp.logical_and(used, jnp.logical_not(sparse)))
    def _():
        run(MOE_TM)

    @pl.when(jnp.logical_and(used, sparse))
    def _():
        run(half)

        @pl.when(last)
        def _():
            o_ref[half * ROW_TILE:, :] = jnp.zeros(((MOE_TM - half) * ROW_TILE, LANES), o_ref.dtype)

    @pl.when(jnp.logical_and(last, jnp.logical_not(used)))
    def _():
        o_ref[...] = jnp.zeros_like(o_ref)


def _expert_ffn(tile_expert, n_used, tile_fill, xs, wg, wu, wd, tf=1792):
    n_rows = xs.shape[0] // ROW_TILE
    ff = wg.shape[2]
    nf = ff // tf

    def f_idx(i, f, nu):
        return jnp.where(i < nu[0], f, nf - 1)

    return pl.pallas_call(
        _expert_kernel,
        grid_spec=pltpu.PrefetchScalarGridSpec(
            num_scalar_prefetch=3,
            grid=(n_rows // MOE_TM, nf),
            in_specs=[pl.BlockSpec((MOE_TM * ROW_TILE, LANES), lambda i, f, te, nu, fill: (i, 0)),
                      pl.BlockSpec((1, D_MODEL, tf), lambda i, f, te, nu, fill: (te[i], 0, f_idx(i, f, nu))),
                      pl.BlockSpec((1, D_MODEL, tf), lambda i, f, te, nu, fill: (te[i], 0, f_idx(i, f, nu))),
                      pl.BlockSpec((1, tf, D_MODEL), lambda i, f, te, nu, fill: (te[i], f_idx(i, f, nu), 0))],
            out_specs=pl.BlockSpec((MOE_TM * ROW_TILE, LANES), lambda i, f, te, nu, fill: (i, 0)),
            scratch_shapes=[pltpu.VMEM((MOE_TM, D_MODEL), f32)]),
        out_shape=jax.ShapeDtypeStruct((n_rows * ROW_TILE, LANES), f32),
        compiler_params=_params("arbitrary", "arbitrary"),
        name="moe_experts",
    )(tile_expert, n_used, tile_fill, xs, wg, wu, wd)


def _combine_kernel(pos0_ref, pos_next_ref, ys_hbm, route_ref, x_ref, p_ref, gpost_ref, wpp_ref, wpg_ref, gple_ref,
                    o_ref, buf, sem):
    i = pl.program_id(0)
    rows = x_ref.shape[0]
    slot = i % 2

    def fetch(pos_ref, into):
        def issue(r, c):
            for k in range(2):
                pltpu.make_async_copy(_tile_rows(ys_hbm, pos_ref[2 * r + k]), _tile_rows(buf.at[into, k], r),
                                      sem.at[into, k]).start(priority=k)
            return c

        lax.fori_loop(0, rows, issue, 0, unroll=ISSUE_UNROLL)

    @pl.when(i == 0)
    def _():
        fetch(pos0_ref, 0)

    @pl.when(i + 1 < pl.num_programs(0))
    def _():
        fetch(pos_next_ref, 1 - slot)

    for k in range(2):
        pltpu.make_async_copy(ys_hbm.at[pl.ds(0, rows * ROW_TILE)], buf.at[slot, k], sem.at[slot, k]).wait()
    route = route_ref[...]
    y = (route[:, 2:3] * _load_row_tiles(buf.at[slot, 0], rows)
         + route[:, 3:4] * _load_row_tiles(buf.at[slot, 1], rows))
    o_ref[...] = _post_ffn(y, x_ref[...], p_ref[...], gpost_ref, wpp_ref, wpg_ref, gple_ref)


def _combine(pos, ys, route, x1, p, p_blk, g_post, wpp, wpg, g_ple):
    t = x1.shape[0]
    n_blk = t // ROW_BLK
    row = lambda i: (i, 0)
    smem = functools.partial(pl.BlockSpec, (2 * ROW_BLK,), memory_space=pltpu.SMEM)
    return pl.pallas_call(
        _combine_kernel,
        grid=(n_blk,),
        in_specs=[smem(lambda i: (0,)),
                  smem(lambda i: (jnp.minimum(i + 1, n_blk - 1),)),
                  pl.BlockSpec(memory_space=pl.ANY),
                  pl.BlockSpec((ROW_BLK, LANES), row),
                  pl.BlockSpec((ROW_BLK, D_MODEL), row),
                  pl.BlockSpec((ROW_BLK, PLE_DIM), lambda i: (i + p_blk, 0)),
                  _const_spec((1, D_MODEL)),
                  _const_spec((PLE_DIM, D_MODEL)),
                  _const_spec((D_MODEL, D_MODEL)),
                  _const_spec((1, D_MODEL))],
        out_specs=pl.BlockSpec((ROW_BLK, D_MODEL), row),
        out_shape=jax.ShapeDtypeStruct((t, D_MODEL), f32),
        scratch_shapes=[pltpu.VMEM((2, 2, ROW_BLK * ROW_TILE, LANES), f32),
                        pltpu.SemaphoreType.DMA((2, 2))],
        compiler_params=_params("arbitrary"),
        name="moe_combine",
    )(pos, pos, ys, route, x1, p, g_post, wpp, wpg, g_ple)


def _route_slots(route, n_tiles):
    ids = route[:, 0:2].astype(jnp.int32).reshape(-1)
    onehot = (ids[None, :] == jnp.arange(N_EXPERTS, dtype=jnp.int32)[:, None]).astype(jnp.int32)
    csum = jnp.cumsum(onehot, axis=1)
    rank = jnp.sum(onehot * csum, axis=0) - 1
    counts = csum[:, -1]
    padded = ((counts + MOE_TM - 1) // MOE_TM) * MOE_TM
    ends = jnp.cumsum(padded)
    pos = (ends - padded)[ids] + rank
    tile_start = jnp.arange(n_tiles, dtype=jnp.int32) * MOE_TM
    tile_expert = jnp.minimum(jnp.sum((tile_start[:, None] >= ends[None, :]).astype(jnp.int32), axis=1),
                              N_EXPERTS - 1)
    n_used = (ends[-1] // MOE_TM).reshape(1)
    fill_plan = jnp.concatenate([ends - padded + counts, n_used])
    tile_fill = jnp.clip((ends - padded + counts)[tile_expert] - tile_start, 0, MOE_TM)
    return (pos.astype(jnp.int32), tile_expert.astype(jnp.int32), n_used.astype(jnp.int32),
            fill_plan.astype(jnp.int32), tile_fill.astype(jnp.int32))


def _moe(h, route, x1, p, p_blk, wg, wu, wd, g_post, wpp, wpg, g_ple):
    t = x1.shape[0]
    n_tiles = 2 * t // MOE_TM + N_EXPERTS
    pos, tile_expert, n_used, fill_plan, tile_fill = _route_slots(route, n_tiles)
    xs = _scatter_rows(fill_plan, pos, h, n_tiles * MOE_TM)
    ys = _expert_ffn(tile_expert, n_used, tile_fill, xs, wg, wu, wd)
    return _combine(pos, ys, route, x1, p, p_blk, g_post, wpp, wpg, g_ple)


def _prep_w_in(w):
    main = jnp.concatenate([w[:, SRC_KA:SRC_VA], w[:, SRC_QB:SRC_GD], w[:, SRC_R:SRC_END]], axis=1)
    wq_t = w[:, SRC_QA:SRC_KA].T
    wv_t = w[:, SRC_VA:SRC_QB].T
    w_gd = jnp.pad(w[:, SRC_GD:SRC_R], ((0, 0), (0, LANES - GATE_RANK)))
    return main.astype(bf16), wq_t.astype(bf16), wv_t.astype(bf16), w_gd.astype(bf16)


def kernel(x, p, w_in, rel_bias, w_gla_gate_up, b_gla_gate, gla_norm, w_branch_a, w_branch_b, w_out, norm_mix_pre, norm_mix_post, norm_ffn_pre, norm_ffn_post, ffn_w_gate, ffn_w_up, ffn_w_down, router_w, moe_w_gate, moe_w_up, moe_w_down, ple_w_proj, ple_w_gate, ple_norm):
    b, s, d = x.shape
    t = b * s
    depth = w_in.shape[0]
    x = x.reshape(t, d)
    vec = lambda a: a.reshape(1, -1).astype(f32)
    moe_bf16 = None
    p = p.reshape(depth * t, PLE_DIM)
    for i in range(depth):
        w_main, wq_t, wv_t, w_gd = _prep_w_in(w_in[i])
        h_in, qt, vt, gd = _qv_proj(x, vec(norm_mix_pre[i]), wq_t, wv_t, w_gd, b, s)
        routed = i % 2 == 1
        nxt = (i + 1) // 2
        flat = lambda w: w[nxt].reshape(-1, w.shape[-1])
        ahead = (not routed) and i + 1 < depth
        proj, _ = _in_proj(h_in, w_main)
        proj3 = proj.reshape(b, s, MAIN_WIDTH)
        ya = _attention(qt, proj3, vt, _attn_bias(rel_bias[i])).reshape(t, A_WIDTH)
        w_up = jnp.pad(w_gla_gate_up[i], ((0, LANES - GATE_RANK), (0, 0))).astype(bf16)
        yb = _gla(proj3, gd.reshape(b, s, LANES), w_up, vec(b_gla_gate[i]), vec(gla_norm[i])).reshape(t, B_V_WIDTH)
        j = i // 2
        rw = jnp.pad(router_w[j], ((0, 0), (0, LANES - N_EXPERTS))).astype(bf16) if routed else None
        outs = _mix_out(ya, yb, proj, x, w_branch_a[i].astype(bf16), w_branch_b[i].astype(bf16),
                        w_out[i].astype(bf16), vec(norm_mix_post[i]), vec(norm_ffn_pre[i]), rw)
        tail = (vec(norm_ffn_post[i]), ple_w_proj[i].astype(bf16), ple_w_gate[i].astype(bf16), vec(ple_norm[i]))
        if routed:
            x1, h, route = outs
            if moe_bf16 is None:
                moe_bf16 = tuple(w[j].astype(bf16) for w in (moe_w_gate, moe_w_up, moe_w_down))
            x = _moe(h, route, x1, p, i * t // ROW_BLK, *moe_bf16, *tail)
            moe_bf16 = None
        else:
            x1, h = outs
            x, cast = _dense_ffn(h, x1, p, i * t // FFN_TM, ffn_w_gate[j].astype(bf16), ffn_w_up[j].astype(bf16),
                                 ffn_w_down[j].astype(bf16), *tail,
                                 side=(flat(moe_w_gate), flat(moe_w_up)) if ahead else ())
            if ahead:
                moe_bf16 = tuple(c.reshape(w.shape[1:]) for c, w in zip(cast, (moe_w_gate, moe_w_up)))
                moe_bf16 += (moe_w_down[nxt].astype(bf16),)
    return x.reshape(b, s, d)
```

```python
import functools

import numpy as np
import jax
import jax.numpy as jnp
from jax import lax
from jax.experimental import pallas as pl
from jax.experimental.pallas import tpu as pltpu

f32 = jnp.float32
bf16 = jnp.bfloat16

D_MODEL = 1024
CHUNK = 64
N_PREV_CHUNKS = 8
A_HEADS = 8
A_HEAD_DIM = 64
A_WIDTH = A_HEADS * A_HEAD_DIM
REL_CLIP = 128
B_HEADS = 4
B_KEY_DIM = 64
B_VAL_DIM = 128
B_K_WIDTH = B_HEADS * B_KEY_DIM
B_V_WIDTH = B_HEADS * B_VAL_DIM
GATE_RANK = 16
GATE_TEMP = 16.0
N_EXPERTS = 8
PLE_DIM = 256
NORM_EPS = 1e-6

LANES = 128
VMEM_LIMIT = 56 * 1024 * 1024
VMEM_LIMIT_BIG = 62 * 1024 * 1024

SRC_QA, SRC_KA, SRC_VA, SRC_QB, SRC_GD, SRC_R = 0, 512, 1024, 1536, 2560, 2576
SRC_END = 5136
COL_KA, COL_QB, COL_KB, COL_VB, COL_R, COL_GA, COL_GB = 0, 512, 768, 1024, 1536, 2048, 3072
MAIN_WIDTH = 4096
IN_TILE_N = 2048

QBLK = 2 * CHUNK
KBAND = (N_PREV_CHUNKS + 2) * CHUNK
KPAD = N_PREV_CHUNKS * CHUNK

SUBLANES = 8
ROW_TILE = D_MODEL // LANES
assert ROW_TILE == SUBLANES
MOE_TM = 512
ROW_BLK = 512
FFN_TM = 512
FF_CHUNK = 512
ISSUE_UNROLL = 8
GLA_NB = 8


def _params(*sem):
    return pltpu.CompilerParams(dimension_semantics=sem, vmem_limit_bytes=VMEM_LIMIT)


def _rms(x, g):
    return x * lax.rsqrt(jnp.mean(x * x, axis=-1, keepdims=True) + NORM_EPS) * g


def _sigmoid(x):
    return 1.0 / (1.0 + jnp.exp(-x))


def _silu(x):
    return x * _sigmoid(x)


def _chunks(n, c):
    return [(s, min(s + c, n)) for s in range(0, n, c)]


def _store_row_tiles(ref, x):
    rows = x.shape[0]
    for c in range(ROW_TILE):
        ref[pl.ds(c, rows, stride=ROW_TILE), :] = x[:, c * LANES:(c + 1) * LANES].astype(ref.dtype)


def _load_row_tiles(ref, rows):
    return jnp.concatenate([ref[pl.ds(c, rows, stride=ROW_TILE), :] for c in range(ROW_TILE)], axis=1)


def _tile_rows(ref, r):
    return ref.at[pl.ds(pl.multiple_of(r * ROW_TILE, ROW_TILE), ROW_TILE)]


def _const_spec(shape):
    nd = len(shape)
    return pl.BlockSpec(shape, lambda *_: (0,) * nd, pipeline_mode=pl.Buffered(1))


def _side_cast_scratch(side, steps):
    slabs = [(m.shape[0] // steps, m.shape[1]) for m in side]
    assert all(r * steps == m.shape[0] for (r, _), m in zip(slabs, side))
    scratch = [pltpu.VMEM((2,) + sl, f32) for sl in slabs] + [pltpu.VMEM(sl, bf16) for sl in slabs]
    if side:
        scratch += [pltpu.SemaphoreType.DMA((len(side), 2)), pltpu.SemaphoreType.DMA((len(side),))]
    return scratch


def _side_cast(step, n_steps, side_in, side_out, scratch):
    n_side = len(side_in)
    if not n_side:
        return lambda: None
    stage_in, stage_out = scratch[:n_side], scratch[n_side:2 * n_side]
    sem_in, sem_out = scratch[2 * n_side], scratch[2 * n_side + 1]
    slot = step % 2

    def slab(ref, k, at):
        rows = stage_out[k].shape[0]
        return ref.at[pl.ds(pl.multiple_of(at * rows, rows), rows)]

    def fetch(at, into):
        for k in range(n_side):
            pltpu.make_async_copy(slab(side_in[k], k, at), stage_in[k].at[into], sem_in.at[k, into]).start()

    @pl.when(step == 0)
    def _():
        fetch(0, 0)

    @pl.when(step + 1 < n_steps)
    def _():
        fetch(step + 1, 1 - slot)

    for k in range(n_side):
        pltpu.make_async_copy(slab(side_in[k], k, step), stage_in[k].at[slot], sem_in.at[k, slot]).wait()

    @pl.when(step > 0)
    def _():
        for k in range(n_side):
            pltpu.make_async_copy(stage_out[k], slab(side_out[k], k, step - 1), sem_out.at[k]).wait()

    for k in range(n_side):
        stage_out[k][...] = stage_in[k][slot].astype(stage_out[k].dtype)
    for k in range(n_side):
        pltpu.make_async_copy(stage_out[k], slab(side_out[k], k, step), sem_out.at[k]).start()

    def finish():
        @pl.when(step == n_steps - 1)
        def _():
            for k in range(n_side):
                pltpu.make_async_copy(stage_out[k], slab(side_out[k], k, step), sem_out.at[k]).wait()

    return finish


def _qv_proj_kernel(x_ref, g_ref, wq_ref, wv_ref, wgd_ref, h_ref, qt_ref, vt_ref, gd_ref):
    nt = (((1,), (1,)), ((), ()))
    h = _rms(x_ref[...], g_ref[...]).astype(bf16)
    h_ref[...] = h
    qt_ref[0] = lax.dot_general(wq_ref[...], h, nt, preferred_element_type=f32).astype(qt_ref.dtype)
    vt = lax.dot_general(wv_ref[...], h, nt, preferred_element_type=f32).astype(vt_ref.dtype)
    for kt in range(vt_ref.shape[1]):
        vt_ref[0, kt] = vt[:, kt * LANES:(kt + 1) * LANES]
    gd_ref[...] = jnp.dot(h, wgd_ref[...], preferred_element_type=f32).astype(gd_ref.dtype)


def _qv_proj(x, g, wq_t, wv_t, w_gd, b, s, tm=1024):
    t = x.shape[0]
    per_b = s // tm
    return pl.pallas_call(
        _qv_proj_kernel,
        grid=(t // tm,),
        in_specs=[pl.BlockSpec((tm, D_MODEL), lambda i: (i, 0)),
                  _const_spec((1, D_MODEL)),
                  _const_spec((A_WIDTH, D_MODEL)),
                  _const_spec((A_WIDTH, D_MODEL)),
                  _const_spec((D_MODEL, LANES))],
        out_specs=[pl.BlockSpec((tm, D_MODEL), lambda i: (i, 0)),
                   pl.BlockSpec((1, A_WIDTH, tm), lambda i: (i // per_b, 0, i % per_b)),
                   pl.BlockSpec((1, tm // LANES, A_WIDTH, LANES), lambda i: (i // per_b, i % per_b, 0, 0)),
                   pl.BlockSpec((tm, LANES), lambda i: (i, 0))],
        out_shape=[jax.ShapeDtypeStruct((t, D_MODEL), bf16),
                   jax.ShapeDtypeStruct((b, A_WIDTH, s), bf16),
                   jax.ShapeDtypeStruct((b, s // LANES, A_WIDTH, LANES), bf16),
                   jax.ShapeDtypeStruct((t, LANES), bf16)],
        compiler_params=_params("parallel"),
        name="qv_proj",
    )(x, g, wq_t, wv_t, w_gd)


def _in_proj_kernel(*refs, n_side):
    h_ref, w_ref = refs[:2]
    o_ref = refs[2 + n_side]
    step = pl.program_id(0) * pl.num_programs(1) + pl.program_id(1)
    finish = _side_cast(step, pl.num_programs(0) * pl.num_programs(1), refs[2:2 + n_side],
                        refs[3 + n_side:3 + 2 * n_side], refs[3 + 2 * n_side:])
    o_ref[...] = jnp.dot(h_ref[...], w_ref[...], preferred_element_type=f32).astype(o_ref.dtype)
    finish()


def _in_proj(h, w, side=(), tm=1024):
    t = h.shape[0]
    n = w.shape[1]
    grid = (t // tm, n // IN_TILE_N)
    hbm = pl.BlockSpec(memory_space=pl.ANY)
    outs = pl.pallas_call(
        functools.partial(_in_proj_kernel, n_side=len(side)),
        grid=grid,
        in_specs=[pl.BlockSpec((tm, D_MODEL), lambda i, j: (i, 0)),
                  pl.BlockSpec((D_MODEL, IN_TILE_N), lambda i, j: (0, j))] + [hbm] * len(side),
        out_specs=[pl.BlockSpec((tm, IN_TILE_N), lambda i, j: (i, j))] + [hbm] * len(side),
        out_shape=[jax.ShapeDtypeStruct((t, n), bf16)] + [jax.ShapeDtypeStruct(m.shape, bf16) for m in side],
        scratch_shapes=_side_cast_scratch(side, grid[0] * grid[1]),
        compiler_params=_params("arbitrary", "arbitrary"),
        name="in_proj",
    )(h, w, *side)
    return outs[0], tuple(outs[1:])


def _attn_kernel(qt_ref, k_ref, vt_ref, bias_ref, o_ref):
    i = pl.program_id(1)
    row = lax.broadcasted_iota(jnp.int32, (LANES, QBLK), 0)
    upper = row >= A_HEAD_DIM
    n_kt = KBAND // LANES
    kt_seq = [i + kt - KPAD // LANES for kt in range(n_kt)]
    kt_clamped = [jnp.maximum(j, 0) for j in kt_seq]
    for hp in range(A_WIDTH // LANES):
        pair = slice(hp * LANES, (hp + 1) * LANES)
        qt = qt_ref[0, pair, :] * (A_HEAD_DIM ** -0.5)
        kps = [k_ref[0, pl.ds(pl.multiple_of(j * LANES, LANES), LANES), pair] for j in kt_clamped]
        outs = []
        for s in range(2):
            h = 2 * hp + s
            qm = jnp.where(upper == bool(s), qt, jnp.zeros_like(qt))
            tiles = []
            for kt in range(n_kt):
                t = jnp.dot(kps[kt], qm, preferred_element_type=f32) + bias_ref[h, kt]
                tiles.append(jnp.where(kt_seq[kt] >= 0, t, -1e30))
            m = tiles[0]
            for t in tiles[1:]:
                m = jnp.maximum(m, t)
            m = jnp.max(m, axis=0, keepdims=True)
            l = None
            o = None
            for kt in range(n_kt):
                p = jnp.exp(tiles[kt] - m)
                ls = jnp.sum(p, axis=0, keepdims=True)
                vt = vt_ref[0, kt_clamped[kt], h * A_HEAD_DIM:(h + 1) * A_HEAD_DIM, :]
                part = jnp.dot(vt, p.astype(bf16), preferred_element_type=f32)
                l = ls if l is None else l + ls
                o = part if o is None else o + part
            outs.append(o / l)
        o_ref[0, :, pair] = jnp.concatenate(outs, axis=0).T.astype(o_ref.dtype)


def _attention(qt, proj3, vt, bias_t):
    b, s, _ = proj3.shape
    return pl.pallas_call(
        _attn_kernel,
        grid=(b, s // QBLK),
        in_specs=[pl.BlockSpec((1, A_WIDTH, QBLK), lambda bi, i: (bi, 0, i)),
                  pl.BlockSpec((1, s, A_WIDTH), lambda bi, i: (bi, 0, COL_KA // A_WIDTH)),
                  pl.BlockSpec((1, s // LANES, A_WIDTH, LANES), lambda bi, i: (bi, 0, 0, 0)),
                  _const_spec((A_HEADS, KBAND // LANES, LANES, QBLK))],
        out_specs=pl.BlockSpec((1, QBLK, A_WIDTH), lambda bi, i: (bi, i, 0)),
        out_shape=jax.ShapeDtypeStruct((b, s, A_WIDTH), bf16),
        compiler_params=_params("parallel", "arbitrary"),
        name="chunk_attention",
    )(qt, proj3, vt, bias_t)


def _attn_bias(rel_bias):
    h = rel_bias.shape[0]
    r = np.arange(QBLK)[:, None]
    k = np.arange(KBAND)[None, :]
    visible = np.where(r < CHUNK, k < KBAND - CHUNK, k >= CHUNK)
    span = QBLK + KBAND - 1
    n_far = span - 2 * REL_CLIP
    line = jnp.concatenate([jnp.broadcast_to(rel_bias[:, -1:], (h, n_far)), rel_bias[:, :0:-1]], axis=1)
    line = jnp.pad(line.astype(f32), ((0, 0), (0, 1)))
    skew = jnp.tile(line, (1, QBLK))[:, :QBLK * span].reshape(h, QBLK, span)
    bias = jnp.where(visible[None], skew[:, :, QBLK - 1:], -1e30)
    return jnp.swapaxes(bias, 1, 2).reshape(h, KBAND // LANES, LANES, QBLK)


def _gla_kernel(q_ref, k_ref, v_ref, gd_ref, r_ref, wup_ref, b_ref, gn_ref, o_ref, state):
    @pl.when(pl.program_id(1) == 0)
    def _():
        state[...] = jnp.zeros_like(state)

    nb, tq = q_ref.shape[0], q_ref.shape[1]
    tn = (((0,), (0,)), ((), ()))
    nt = (((1,), (1,)), ((), ()))
    row = lax.broadcasted_iota(jnp.int32, (tq, tq), 0)
    col = lax.broadcasted_iota(jnp.int32, (tq, tq), 1)
    later = jnp.logical_and(col > row, col // CHUNK == row // CHUNK)
    later = jnp.where(later, 1.0, 0.0).astype(bf16)
    log_a, kdec = [], []
    for bb in range(nb):
        z = jnp.dot(gd_ref[bb], wup_ref[...], preferred_element_type=f32) + b_ref[...]
        la = (jnp.minimum(z, 0.0) - jnp.log(1.0 + jnp.exp(-jnp.abs(z)))) / GATE_TEMP
        la_hi = la.astype(bf16)
        la_lo = (la - la_hi.astype(f32)).astype(bf16)
        rev = (jnp.dot(later, la_hi, preferred_element_type=f32)
               + jnp.dot(later, la_lo, preferred_element_type=f32))
        log_a.append(la)
        kdec.append((k_ref[bb].astype(f32) * jnp.exp(rev)).astype(bf16))
    hrow = lax.broadcasted_iota(jnp.int32, (B_HEADS * CHUNK, B_K_WIDTH), 0) // CHUNK
    hlane = lax.broadcasted_iota(jnp.int32, (B_HEADS * CHUNK, B_K_WIDTH), 1) // B_KEY_DIM
    own = hrow == hlane
    st = [state[bb] for bb in range(nb)]
    for c in range(tq // CHUNK):
        rows = slice(c * CHUNK, (c + 1) * CHUNK)
        for bb in range(nb):
            decay = jnp.exp(jnp.sum(log_a[bb][rows], axis=0, keepdims=True))
            kd = kdec[bb][rows]
            kd4 = jnp.where(own, jnp.concatenate([kd] * B_HEADS, axis=0), jnp.zeros((), bf16))
            v4 = jnp.concatenate([v_ref[bb, rows, h * B_VAL_DIM:(h + 1) * B_VAL_DIM] for h in range(B_HEADS)],
                                 axis=0)
            d_state = lax.dot_general(v4, kd4, tn, preferred_element_type=f32)
            st[bb] = decay * st[bb] + d_state
            qc = q_ref[bb, rows, :] * (B_KEY_DIM ** -0.5)
            q4 = jnp.where(own, jnp.concatenate([qc] * B_HEADS, axis=0), jnp.zeros((), bf16))
            o4 = lax.dot_general(q4, st[bb].astype(bf16), nt, preferred_element_type=f32)
            for h in range(B_HEADS):
                vcols = slice(h * B_VAL_DIM, (h + 1) * B_VAL_DIM)
                o = _rms(o4[h * CHUNK:(h + 1) * CHUNK], gn_ref[:, vcols])
                r = r_ref[bb, rows, vcols].astype(f32)
                o_ref[bb, rows, vcols] = (o * _silu(r)).astype(o_ref.dtype)
    for bb in range(nb):
        state[bb] = st[bb]


def _gla(proj3, gd3, w_up, b_gate, gla_norm, tq=256, nb=GLA_NB):
    b, s, _ = proj3.shape
    return pl.pallas_call(
        _gla_kernel,
        grid=(b // nb, s // tq),
        in_specs=[pl.BlockSpec((nb, tq, B_K_WIDTH), lambda bi, i: (bi, i, COL_QB // B_K_WIDTH)),
                  pl.BlockSpec((nb, tq, B_K_WIDTH), lambda bi, i: (bi, i, COL_KB // B_K_WIDTH)),
                  pl.BlockSpec((nb, tq, B_V_WIDTH), lambda bi, i: (bi, i, COL_VB // B_V_WIDTH)),
                  pl.BlockSpec((nb, tq, LANES), lambda bi, i: (bi, i, 0)),
                  pl.BlockSpec((nb, tq, B_V_WIDTH), lambda bi, i: (bi, i, COL_R // B_V_WIDTH)),
                  _const_spec((LANES, B_K_WIDTH)),
                  _const_spec((1, B_K_WIDTH)),
                  _const_spec((1, B_V_WIDTH))],
        out_specs=pl.BlockSpec((nb, tq, B_V_WIDTH), lambda bi, i: (bi, i, 0)),
        out_shape=jax.ShapeDtypeStruct((b, s, B_V_WIDTH), bf16),
        scratch_shapes=[pltpu.VMEM((nb, B_VAL_DIM, B_K_WIDTH), f32)],
        compiler_params=_params("parallel", "arbitrary"),
        name="gla",
    )(proj3, proj3, proj3, gd3, proj3, w_up, b_gate, gla_norm)


def _route_top2(h, rw_ref):
    logits = jnp.dot(h.astype(bf16), rw_ref[...], preferred_element_type=f32)
    lane = lax.broadcasted_iota(jnp.int32, logits.shape, 1).astype(f32)
    neg = -jnp.inf
    lg = jnp.where(lane < N_EXPERTS, logits, neg)
    m1 = jnp.max(lg, axis=-1, keepdims=True)
    i1 = jnp.min(jnp.where(lg == m1, lane, float(LANES)), axis=-1, keepdims=True)
    lg2 = jnp.where(lane == i1, neg, lg)
    m2 = jnp.max(lg2, axis=-1, keepdims=True)
    i2 = jnp.min(jnp.where(lg2 == m2, lane, float(LANES)), axis=-1, keepdims=True)
    e2 = jnp.exp(m2 - m1)
    w1 = 1.0 / (1.0 + e2)
    w2 = e2 / (1.0 + e2)
    return jnp.where(lane == 0, i1, jnp.where(lane == 1, i2, jnp.where(lane == 2, w1, jnp.where(lane == 3, w2, 0.0))))


def _mix_out_kernel(*refs, routed):
    if routed:
        (ya_ref, yb_ref, ga_ref, gb_ref, x_ref, wa_ref, wb_ref, wo_ref, gpost_ref, gpre_ref, rw_ref,
         xo_ref, ho_ref, route_ref) = refs
    else:
        (ya_ref, yb_ref, ga_ref, gb_ref, x_ref, wa_ref, wb_ref, wo_ref, gpost_ref, gpre_ref,
         xo_ref, ho_ref) = refs
    a = jnp.dot(ya_ref[...], wa_ref[...], preferred_element_type=f32)
    b = jnp.dot(yb_ref[...], wb_ref[...], preferred_element_type=f32)
    merged = _sigmoid(ga_ref[...].astype(f32)) * a + _sigmoid(gb_ref[...].astype(f32)) * b
    y = jnp.dot(merged.astype(bf16), wo_ref[...], preferred_element_type=f32)
    x1 = x_ref[...] + _rms(y, gpost_ref[...])
    h = _rms(x1, gpre_ref[...])
    xo_ref[...] = x1
    if routed:
        _store_row_tiles(ho_ref, h)
        route_ref[...] = _route_top2(h, rw_ref)
    else:
        ho_ref[...] = h.astype(ho_ref.dtype)


def _mix_out(ya, yb, proj, x, wa, wb, wo, g_post, g_pre, router_w=None, tm=1024):
    t = x.shape[0]
    routed = router_w is not None
    row = lambda i: (i, 0)
    in_specs = [pl.BlockSpec((tm, A_WIDTH), row),
                pl.BlockSpec((tm, B_V_WIDTH), row),
                pl.BlockSpec((tm, D_MODEL), lambda i: (i, COL_GA // D_MODEL)),
                pl.BlockSpec((tm, D_MODEL), lambda i: (i, COL_GB // D_MODEL)),
                pl.BlockSpec((tm, D_MODEL), row),
                _const_spec((A_WIDTH, D_MODEL)),
                _const_spec((B_V_WIDTH, D_MODEL)),
                _const_spec((D_MODEL, D_MODEL)),
                _const_spec((1, D_MODEL)),
                _const_spec((1, D_MODEL))]
    args = [ya, yb, proj, proj, x, wa, wb, wo, g_post, g_pre]
    if routed:
        out_specs = [pl.BlockSpec((tm, D_MODEL), row), pl.BlockSpec((tm * ROW_TILE, LANES), row)]
        out_shape = [jax.ShapeDtypeStruct((t, D_MODEL), f32), jax.ShapeDtypeStruct((t * ROW_TILE, LANES), f32)]
    else:
        out_specs = [pl.BlockSpec((tm, D_MODEL), row), pl.BlockSpec((tm, D_MODEL), row)]
        out_shape = [jax.ShapeDtypeStruct((t, D_MODEL), f32), jax.ShapeDtypeStruct((t, D_MODEL), bf16)]
    if routed:
        in_specs.append(_const_spec((D_MODEL, LANES)))
        args.append(router_w)
        out_specs.append(pl.BlockSpec((tm, LANES), row))
        out_shape.append(jax.ShapeDtypeStruct((t, LANES), f32))
    return pl.pallas_call(
        functools.partial(_mix_out_kernel, routed=routed),
        grid=(t // tm,),
        in_specs=in_specs,
        out_specs=out_specs,
        out_shape=out_shape,
        compiler_params=_params("parallel"),
        name="mix_out",
    )(*args)


def _post_ffn(y, x1, p, gpost_ref, wpp_ref, wpg_ref, gple_ref):
    x2 = x1 + _rms(y, gpost_ref[...])
    e = jnp.dot(p.astype(bf16), wpp_ref[...], preferred_element_type=f32)
    e = e * _sigmoid(jnp.dot(x2.astype(bf16), wpg_ref[...], preferred_element_type=f32))
    return x2 + _rms(e, gple_ref[...])


def _swiglu_chunks(x, wg_ref, wu_ref, wd_ref, lead, width):
    acc = None
    for c0, c1 in _chunks(width, FF_CHUNK):
        g = jnp.dot(x, wg_ref[lead + (slice(None), slice(c0, c1))], preferred_element_type=f32)
        u = jnp.dot(x, wu_ref[lead + (slice(None), slice(c0, c1))], preferred_element_type=f32)
        act = (_silu(g) * u).astype(bf16)
        part = jnp.dot(act, wd_ref[lead + (slice(c0, c1), slice(None))], preferred_element_type=f32)
        acc = part if acc is None else acc + part
    return acc


def _dense_ffn_kernel(*refs, n_side):
    (h_ref, x_ref, p_ref, wg_ref, wu_ref, wd_ref, gpost_ref, wpp_ref, wpg_ref, gple_ref) = refs[:10]
    o_ref = refs[10 + n_side]
    finish = _side_cast(pl.program_id(0), pl.num_programs(0), refs[10:10 + n_side],
                        refs[11 + n_side:11 + 2 * n_side], refs[11 + 2 * n_side:])
    y = _swiglu_chunks(h_ref[...], wg_ref, wu_ref, wd_ref, (), wg_ref.shape[1])
    o_ref[...] = _post_ffn(y, x_ref[...], p_ref[...], gpost_ref, wpp_ref, wpg_ref, gple_ref)
    finish()


def _dense_ffn(h, x1, p, p_blk, wg, wu, wd, g_post, wpp, wpg, g_ple, side=(), tm=FFN_TM):
    t = x1.shape[0]
    ff = wg.shape[1]
    steps = t // tm
    row = lambda i: (i, 0)
    hbm = pl.BlockSpec(memory_space=pl.ANY)
    n_side = len(side)
    outs = pl.pallas_call(
        functools.partial(_dense_ffn_kernel, n_side=n_side),
        grid=(steps,),
        in_specs=[pl.BlockSpec((tm, D_MODEL), row),
                  pl.BlockSpec((tm, D_MODEL), row),
                  pl.BlockSpec((tm, PLE_DIM), lambda i: (i + p_blk, 0)),
                  _const_spec((D_MODEL, ff)),
                  _const_spec((D_MODEL, ff)),
                  _const_spec((ff, D_MODEL)),
                  _const_spec((1, D_MODEL)),
                  _const_spec((PLE_DIM, D_MODEL)),
                  _const_spec((D_MODEL, D_MODEL)),
                  _const_spec((1, D_MODEL))] + [hbm] * n_side,
        out_specs=[pl.BlockSpec((tm, D_MODEL), row)] + [hbm] * n_side,
        out_shape=[jax.ShapeDtypeStruct((t, D_MODEL), f32)] + [jax.ShapeDtypeStruct(m.shape, bf16) for m in side],
        scratch_shapes=_side_cast_scratch(side, steps),
        compiler_params=pltpu.CompilerParams(dimension_semantics=("arbitrary",), vmem_limit_bytes=VMEM_LIMIT_BIG),
        name="dense_ffn",
    )(h, x1, p, wg, wu, wd, g_post, wpp, wpg, g_ple, *side)
    return outs[0], tuple(outs[1:])


def _scatter_kernel(fill_ref, pos_ref, h_ref, xs_out, zbuf, sem, zsem):
    rows = h_ref.shape[0] // ROW_TILE

    @pl.when(pl.program_id(0) == 0)
    def _():
        zbuf[...] = jnp.zeros_like(zbuf)

        def zero_rows(first_row):
            start = pl.multiple_of(first_row * ROW_TILE, ROW_TILE)
            fill = pltpu.make_async_copy(zbuf, xs_out.at[pl.ds(start, MOE_TM * ROW_TILE)], zsem)
            fill.start()
            fill.wait()

        for e in range(N_EXPERTS):
            zero_rows(fill_ref[e])

        def zero_tile(tile, c):
            zero_rows(tile * MOE_TM)
            return c

        lax.fori_loop(fill_ref[N_EXPERTS], xs_out.shape[0] // (MOE_TM * ROW_TILE), zero_tile, 0)

    def issue(r, c):
        for k in range(2):
            pltpu.make_async_copy(_tile_rows(h_ref, r), _tile_rows(xs_out, pos_ref[2 * r + k]),
                                  sem.at[k]).start(priority=k)
        return c

    lax.fori_loop(0, rows, issue, 0, unroll=ISSUE_UNROLL)
    for k in range(2):
        pltpu.make_async_copy(h_ref, xs_out.at[pl.ds(0, rows * ROW_TILE)], sem.at[k]).wait()


def _scatter_rows(fill_plan, pos, h, n_rows):
    t = h.shape[0] // ROW_TILE
    return pl.pallas_call(
        _scatter_kernel,
        grid_spec=pltpu.PrefetchScalarGridSpec(
            num_scalar_prefetch=1,
            grid=(t // ROW_BLK,),
            in_specs=[pl.BlockSpec((2 * ROW_BLK,), lambda i, fill: (i,), memory_space=pltpu.SMEM),
                      pl.BlockSpec((ROW_BLK * ROW_TILE, LANES), lambda i, fill: (i, 0))],
            out_specs=pl.BlockSpec(memory_space=pl.ANY),
            scratch_shapes=[pltpu.VMEM((MOE_TM * ROW_TILE, LANES), f32),
                            pltpu.SemaphoreType.DMA((2,)),
                            pltpu.SemaphoreType.DMA(())]),
        out_shape=jax.ShapeDtypeStruct((n_rows * ROW_TILE, LANES), f32),
        compiler_params=_params("arbitrary"),
        name="moe_scatter",
    )(fill_plan, pos, h)


def _expert_kernel(te_ref, nu_ref, fill_ref, x_ref, wg_ref, wu_ref, wd_ref, o_ref, acc_ref):
    del te_ref
    i = pl.program_id(0)
    f = pl.program_id(1)
    used = i < nu_ref[0]
    half = MOE_TM // 2
    sparse = fill_ref[i] <= half
    last = f == pl.num_programs(1) - 1

    def run(rows):
        x = _load_row_tiles(x_ref, rows).astype(bf16)
        y = _swiglu_chunks(x, wg_ref, wu_ref, wd_ref, (0,), wg_ref.shape[2])

        @pl.when(f == 0)
        def _():
            acc_ref[0:rows] = y

        @pl.when(f > 0)
        def _():
            acc_ref[0:rows] += y

        @pl.when(last)
        def _():
            _store_row_tiles(o_ref, acc_ref[0:rows])

    @pl.when(jnp.logical_and(used, jnp.logical_not(sparse)))
    def _():
        run(MOE_TM)

    @pl.when(jnp.logical_and(used, sparse))
    def _():
        run(half)

        @pl.when(last)
        def _():
            o_ref[half * ROW_TILE:, :] = jnp.zeros(((MOE_TM - half) * ROW_TILE, LANES), o_ref.dtype)

    @pl.when(jnp.logical_and(last, jnp.logical_not(used)))
    def _():
        o_ref[...] = jnp.zeros_like(o_ref)


def _expert_ffn(tile_expert, n_used, tile_fill, xs, wg, wu, wd, tf=1792):
    n_rows = xs.shape[0] // ROW_TILE
    ff = wg.shape[2]
    nf = ff // tf

    def f_idx(i, f, nu):
        return jnp.where(i < nu[0], f, nf - 1)

    return pl.pallas_call(
        _expert_kernel,
        grid_spec=pltpu.PrefetchScalarGridSpec(
            num_scalar_prefetch=3,
            grid=(n_rows // MOE_TM, nf),
            in_specs=[pl.BlockSpec((MOE_TM * ROW_TILE, LANES), lambda i, f, te, nu, fill: (i, 0)),
                      pl.BlockSpec((1, D_MODEL, tf), lambda i, f, te, nu, fill: (te[i], 0, f_idx(i, f, nu))),
                      pl.BlockSpec((1, D_MODEL, tf), lambda i, f, te, nu, fill: (te[i], 0, f_idx(i, f, nu))),
                      pl.BlockSpec((1, tf, D_MODEL), lambda i, f, te, nu, fill: (te[i], f_idx(i, f, nu), 0))],
            out_specs=pl.BlockSpec((MOE_TM * ROW_TILE, LANES), lambda i, f, te, nu, fill: (i, 0)),
            scratch_shapes=[pltpu.VMEM((MOE_TM, D_MODEL), f32)]),
        out_shape=jax.ShapeDtypeStruct((n_rows * ROW_TILE, LANES), f32),
        compiler_params=_params("arbitrary", "arbitrary"),
        name="moe_experts",
    )(tile_expert, n_used, tile_fill, xs, wg, wu, wd)


def _combine_kernel(pos0_ref, pos_next_ref, ys_hbm, route_ref, x_ref, p_ref, gpost_ref, wpp_ref, wpg_ref, gple_ref,
                    o_ref, buf, sem):
    i = pl.program_id(0)
    rows = x_ref.shape[0]
    slot = i % 2

    def fetch(pos_ref, into):
        def issue(r, c):
            for k in range(2):
                pltpu.make_async_copy(_tile_rows(ys_hbm, pos_ref[2 * r + k]), _tile_rows(buf.at[into, k], r),
                                      sem.at[into, k]).start(priority=k)
            return c

        lax.fori_loop(0, rows, issue, 0, unroll=ISSUE_UNROLL)

    @pl.when(i == 0)
    def _():
        fetch(pos0_ref, 0)

    @pl.when(i + 1 < pl.num_programs(0))
    def _():
        fetch(pos_next_ref, 1 - slot)

    for k in range(2):
        pltpu.make_async_copy(ys_hbm.at[pl.ds(0, rows * ROW_TILE)], buf.at[slot, k], sem.at[slot, k]).wait()
    route = route_ref[...]
    y = (route[:, 2:3] * _load_row_tiles(buf.at[slot, 0], rows)
         + route[:, 3:4] * _load_row_tiles(buf.at[slot, 1], rows))
    o_ref[...] = _post_ffn(y, x_ref[...], p_ref[...], gpost_ref, wpp_ref, wpg_ref, gple_ref)


def _combine(pos, ys, route, x1, p, p_blk, g_post, wpp, wpg, g_ple):
    t = x1.shape[0]
    n_blk = t // ROW_BLK
    row = lambda i: (i, 0)
    smem = functools.partial(pl.BlockSpec, (2 * ROW_BLK,), memory_space=pltpu.SMEM)
    return pl.pallas_call(
        _combine_kernel,
        grid=(n_blk,),
        in_specs=[smem(lambda i: (0,)),
                  smem(lambda i: (jnp.minimum(i + 1, n_blk - 1),)),
                  pl.BlockSpec(memory_space=pl.ANY),
                  pl.BlockSpec((ROW_BLK, LANES), row),
                  pl.BlockSpec((ROW_BLK, D_MODEL), row),
                  pl.BlockSpec((ROW_BLK, PLE_DIM), lambda i: (i + p_blk, 0)),
                  _const_spec((1, D_MODEL)),
                  _const_spec((PLE_DIM, D_MODEL)),
                  _const_spec((D_MODEL, D_MODEL)),
                  _const_spec((1, D_MODEL))],
        out_specs=pl.BlockSpec((ROW_BLK, D_MODEL), row),
        out_shape=jax.ShapeDtypeStruct((t, D_MODEL), f32),
        scratch_shapes=[pltpu.VMEM((2, 2, ROW_BLK * ROW_TILE, LANES), f32),
                        pltpu.SemaphoreType.DMA((2, 2))],
        compiler_params=_params("arbitrary"),
        name="moe_combine",
    )(pos, pos, ys, route, x1, p, g_post, wpp, wpg, g_ple)


def _route_slots(route, n_tiles):
    ids = route[:, 0:2].astype(jnp.int32).reshape(-1)
    onehot = (ids[None, :] == jnp.arange(N_EXPERTS, dtype=jnp.int32)[:, None]).astype(jnp.int32)
    csum = jnp.cumsum(onehot, axis=1)
    rank = jnp.sum(onehot * csum, axis=0) - 1
    counts = csum[:, -1]
    padded = ((counts + MOE_TM - 1) // MOE_TM) * MOE_TM
    ends = jnp.cumsum(padded)
    pos = (ends - padded)[ids] + rank
    tile_start = jnp.arange(n_tiles, dtype=jnp.int32) * MOE_TM
    tile_expert = jnp.minimum(jnp.sum((tile_start[:, None] >= ends[None, :]).astype(jnp.int32), axis=1),
                              N_EXPERTS - 1)
    n_used = (ends[-1] // MOE_TM).reshape(1)
    fill_plan = jnp.concatenate([ends - padded + counts, n_used])
    tile_fill = jnp.clip((ends - padded + counts)[tile_expert] - tile_start, 0, MOE_TM)
    return (pos.astype(jnp.int32), tile_expert.astype(jnp.int32), n_used.astype(jnp.int32),
            fill_plan.astype(jnp.int32), tile_fill.astype(jnp.int32))


def _moe(h, route, x1, p, p_blk, wg, wu, wd, g_post, wpp, wpg, g_ple):
    t = x1.shape[0]
    n_tiles = 2 * t // MOE_TM + N_EXPERTS
    pos, tile_expert, n_used, fill_plan, tile_fill = _route_slots(route, n_tiles)
    xs = _scatter_rows(fill_plan, pos, h, n_tiles * MOE_TM)
    ys = _expert_ffn(tile_expert, n_used, tile_fill, xs, wg, wu, wd)
    return _combine(pos, ys, route, x1, p, p_blk, g_post, wpp, wpg, g_ple)


def _prep_w_in(w):
    main = jnp.concatenate([w[:, SRC_KA:SRC_VA], w[:, SRC_QB:SRC_GD], w[:, SRC_R:SRC_END]], axis=1)
    wq_t = w[:, SRC_QA:SRC_KA].T
    wv_t = w[:, SRC_VA:SRC_QB].T
    w_gd = jnp.pad(w[:, SRC_GD:SRC_R], ((0, 0), (0, LANES - GATE_RANK)))
    return main.astype(bf16), wq_t.astype(bf16), wv_t.astype(bf16), w_gd.astype(bf16)


def kernel(x, p, w_in, rel_bias, w_gla_gate_up, b_gla_gate, gla_norm, w_branch_a, w_branch_b, w_out, norm_mix_pre, norm_mix_post, norm_ffn_pre, norm_ffn_post, ffn_w_gate, ffn_w_up, ffn_w_down, router_w, moe_w_gate, moe_w_up, moe_w_down, ple_w_proj, ple_w_gate, ple_norm):
    b, s, d = x.shape
    t = b * s
    depth = w_in.shape[0]
    x = x.reshape(t, d)
    vec = lambda a: a.reshape(1, -1).astype(f32)
    moe_bf16 = None
    p = p.reshape(depth * t, PLE_DIM)
    for i in range(depth):
        w_main, wq_t, wv_t, w_gd = _prep_w_in(w_in[i])
        h_in, qt, vt, gd = _qv_proj(x, vec(norm_mix_pre[i]), wq_t, wv_t, w_gd, b, s)
        routed = i % 2 == 1
        nxt = (i + 1) // 2
        flat = lambda w: w[nxt].reshape(-1, w.shape[-1])
        ahead = (not routed) and i + 1 < depth
        proj, _ = _in_proj(h_in, w_main)
        proj3 = proj.reshape(b, s, MAIN_WIDTH)
        ya = _attention(qt, proj3, vt, _attn_bias(rel_bias[i])).reshape(t, A_WIDTH)
        w_up = jnp.pad(w_gla_gate_up[i], ((0, LANES - GATE_RANK), (0, 0))).astype(bf16)
        yb = _gla(proj3, gd.reshape(b, s, LANES), w_up, vec(b_gla_gate[i]), vec(gla_norm[i])).reshape(t, B_V_WIDTH)
        j = i // 2
        rw = jnp.pad(router_w[j], ((0, 0), (0, LANES - N_EXPERTS))).astype(bf16) if routed else None
        outs = _mix_out(ya, yb, proj, x, w_branch_a[i].astype(bf16), w_branch_b[i].astype(bf16),
                        w_out[i].astype(bf16), vec(norm_mix_post[i]), vec(norm_ffn_pre[i]), rw)
        tail = (vec(norm_ffn_post[i]), ple_w_proj[i].astype(bf16), ple_w_gate[i].astype(bf16), vec(ple_norm[i]))
        if routed:
            x1, h, route = outs
            if moe_bf16 is None:
                moe_bf16 = tuple(w[j].astype(bf16) for w in (moe_w_gate, moe_w_up, moe_w_down))
            x = _moe(h, route, x1, p, i * t // ROW_BLK, *moe_bf16, *tail)
            moe_bf16 = None
        else:
            x1, h = outs
            x, cast = _dense_ffn(h, x1, p, i * t // FFN_TM, ffn_w_gate[j].astype(bf16), ffn_w_up[j].astype(bf16),
                                 ffn_w_down[j].astype(bf16), *tail,
                                 side=(flat(moe_w_gate), flat(moe_w_up)) if ahead else ())
            if ahead:
                moe_bf16 = tuple(c.reshape(w.shape[1:]) for c, w in zip(cast, (moe_w_gate, moe_w_up)))
                moe_bf16 += (moe_w_down[nxt].astype(bf16),)
    return x.reshape(b, s, d)
```

```python
import functools

import numpy as np
import jax
import jax.numpy as jnp
from jax import lax
from jax.experimental import pallas as pl
from jax.experimental.pallas import tpu as pltpu

f32 = jnp.float32
bf16 = jnp.bfloat16

D_MODEL = 1024
CHUNK = 64
N_PREV_CHUNKS = 8
A_HEADS = 8
A_HEAD_DIM = 64
A_WIDTH = A_HEADS * A_HEAD_DIM
REL_CLIP = 128
B_HEADS = 4
B_KEY_DIM = 64
B_VAL_DIM = 128
B_K_WIDTH = B_HEADS * B_KEY_DIM
B_V_WIDTH = B_HEADS * B_VAL_DIM
GATE_RANK = 16
GATE_TEMP = 16.0
N_EXPERTS = 8
PLE_DIM = 256
NORM_EPS = 1e-6

LANES = 128
VMEM_LIMIT = 56 * 1024 * 1024
VMEM_LIMIT_BIG = 62 * 1024 * 1024

SRC_QA, SRC_KA, SRC_VA, SRC_QB, SRC_GD, SRC_R = 0, 512, 1024, 1536, 2560, 2576
SRC_END = 5136
COL_KA, COL_QB, COL_KB, COL_VB, COL_R, COL_GA, COL_GB = 0, 512, 768, 1024, 1536, 2048, 3072
MAIN_WIDTH = 4096
IN_TILE_N = 2048

QBLK = 2 * CHUNK
KBAND = (N_PREV_CHUNKS + 2) * CHUNK
KPAD = N_PREV_CHUNKS * CHUNK

SUBLANES = 8
ROW_TILE = D_MODEL // LANES
assert ROW_TILE == SUBLANES
MOE_TM = 512
ROW_BLK = 512
FFN_TM = 512
FF_CHUNK = 512
ISSUE_UNROLL = 8
GLA_NB = 4


def _params(*sem):
    return pltpu.CompilerParams(dimension_semantics=sem, vmem_limit_bytes=VMEM_LIMIT)


def _rms(x, g):
    return x * lax.rsqrt(jnp.mean(x * x, axis=-1, keepdims=True) + NORM_EPS) * g


def _sigmoid(x):
    return 1.0 / (1.0 + jnp.exp(-x))


def _silu(x):
    return x * _sigmoid(x)


def _chunks(n, c):
    return [(s, min(s + c, n)) for s in range(0, n, c)]


def _store_row_tiles(ref, x):
    rows = x.shape[0]
    for c in range(ROW_TILE):
        ref[pl.ds(c, rows, stride=ROW_TILE), :] = x[:, c * LANES:(c + 1) * LANES].astype(ref.dtype)


def _load_row_tiles(ref, rows):
    return jnp.concatenate([ref[pl.ds(c, rows, stride=ROW_TILE), :] for c in range(ROW_TILE)], axis=1)


def _tile_rows(ref, r):
    return ref.at[pl.ds(pl.multiple_of(r * ROW_TILE, ROW_TILE), ROW_TILE)]


def _const_spec(shape):
    nd = len(shape)
    return pl.BlockSpec(shape, lambda *_: (0,) * nd, pipeline_mode=pl.Buffered(1))


def _side_cast_scratch(side, steps):
    slabs = [(m.shape[0] // steps, m.shape[1]) for m in side]
    assert all(r * steps == m.shape[0] for (r, _), m in zip(slabs, side))
    scratch = [pltpu.VMEM((2,) + sl, f32) for sl in slabs] + [pltpu.VMEM(sl, bf16) for sl in slabs]
    if side:
        scratch += [pltpu.SemaphoreType.DMA((len(side), 2)), pltpu.SemaphoreType.DMA((len(side),))]
    return scratch


def _side_cast(step, n_steps, side_in, side_out, scratch):
    n_side = len(side_in)
    if not n_side:
        return lambda: None
    stage_in, stage_out = scratch[:n_side], scratch[n_side:2 * n_side]
    sem_in, sem_out = scratch[2 * n_side], scratch[2 * n_side + 1]
    slot = step % 2

    def slab(ref, k, at):
        rows = stage_out[k].shape[0]
        return ref.at[pl.ds(pl.multiple_of(at * rows, rows), rows)]

    def fetch(at, into):
        for k in range(n_side):
            pltpu.make_async_copy(slab(side_in[k], k, at), stage_in[k].at[into], sem_in.at[k, into]).start()

    @pl.when(step == 0)
    def _():
        fetch(0, 0)

    @pl.when(step + 1 < n_steps)
    def _():
        fetch(step + 1, 1 - slot)

    for k in range(n_side):
        pltpu.make_async_copy(slab(side_in[k], k, step), stage_in[k].at[slot], sem_in.at[k, slot]).wait()

    @pl.when(step > 0)
    def _():
        for k in range(n_side):
            pltpu.make_async_copy(stage_out[k], slab(side_out[k], k, step - 1), sem_out.at[k]).wait()

    for k in range(n_side):
        stage_out[k][...] = stage_in[k][slot].astype(stage_out[k].dtype)
    for k in range(n_side):
        pltpu.make_async_copy(stage_out[k], slab(side_out[k], k, step), sem_out.at[k]).start()

    def finish():
        @pl.when(step == n_steps - 1)
        def _():
            for k in range(n_side):
                pltpu.make_async_copy(stage_out[k], slab(side_out[k], k, step), sem_out.at[k]).wait()

    return finish


def _qv_proj_kernel(x_ref, g_ref, wq_ref, wv_ref, wgd_ref, h_ref, qt_ref, vt_ref, gd_ref):
    nt = (((1,), (1,)), ((), ()))
    h = _rms(x_ref[...], g_ref[...]).astype(bf16)
    h_ref[...] = h
    qt_ref[0] = lax.dot_general(wq_ref[...], h, nt, preferred_element_type=f32).astype(qt_ref.dtype)
    vt = lax.dot_general(wv_ref[...], h, nt, preferred_element_type=f32).astype(vt_ref.dtype)
    for kt in range(vt_ref.shape[1]):
        vt_ref[0, kt] = vt[:, kt * LANES:(kt + 1) * LANES]
    gd_ref[...] = jnp.dot(h, wgd_ref[...], preferred_element_type=f32).astype(gd_ref.dtype)


def _qv_proj(x, g, wq_t, wv_t, w_gd, b, s, tm=1024):
    t = x.shape[0]
    per_b = s // tm
    return pl.pallas_call(
        _qv_proj_kernel,
        grid=(t // tm,),
        in_specs=[pl.BlockSpec((tm, D_MODEL), lambda i: (i, 0)),
                  _const_spec((1, D_MODEL)),
                  _const_spec((A_WIDTH, D_MODEL)),
                  _const_spec((A_WIDTH, D_MODEL)),
                  _const_spec((D_MODEL, LANES))],
        out_specs=[pl.BlockSpec((tm, D_MODEL), lambda i: (i, 0)),
                   pl.BlockSpec((1, A_WIDTH, tm), lambda i: (i // per_b, 0, i % per_b)),
                   pl.BlockSpec((1, tm // LANES, A_WIDTH, LANES), lambda i: (i // per_b, i % per_b, 0, 0)),
                   pl.BlockSpec((tm, LANES), lambda i: (i, 0))],
        out_shape=[jax.ShapeDtypeStruct((t, D_MODEL), bf16),
                   jax.ShapeDtypeStruct((b, A_WIDTH, s), bf16),
                   jax.ShapeDtypeStruct((b, s // LANES, A_WIDTH, LANES), bf16),
                   jax.ShapeDtypeStruct((t, LANES), bf16)],
        compiler_params=_params("parallel"),
        name="qv_proj",
    )(x, g, wq_t, wv_t, w_gd)


def _in_proj_kernel(*refs, n_side):
    h_ref, w_ref = refs[:2]
    o_ref = refs[2 + n_side]
    step = pl.program_id(0) * pl.num_programs(1) + pl.program_id(1)
    finish = _side_cast(step, pl.num_programs(0) * pl.num_programs(1), refs[2:2 + n_side],
                        refs[3 + n_side:3 + 2 * n_side], refs[3 + 2 * n_side:])
    o_ref[...] = jnp.dot(h_ref[...], w_ref[...], preferred_element_type=f32).astype(o_ref.dtype)
    finish()


def _in_proj(h, w, side=(), tm=1024):
    t = h.shape[0]
    n = w.shape[1]
    grid = (t // tm, n // IN_TILE_N)
    hbm = pl.BlockSpec(memory_space=pl.ANY)
    outs = pl.pallas_call(
        functools.partial(_in_proj_kernel, n_side=len(side)),
        grid=grid,
        in_specs=[pl.BlockSpec((tm, D_MODEL), lambda i, j: (i, 0)),
                  pl.BlockSpec((D_MODEL, IN_TILE_N), lambda i, j: (0, j))] + [hbm] * len(side),
        out_specs=[pl.BlockSpec((tm, IN_TILE_N), lambda i, j: (i, j))] + [hbm] * len(side),
        out_shape=[jax.ShapeDtypeStruct((t, n), bf16)] + [jax.ShapeDtypeStruct(m.shape, bf16) for m in side],
        scratch_shapes=_side_cast_scratch(side, grid[0] * grid[1]),
        compiler_params=_params("arbitrary", "arbitrary"),
        name="in_proj",
    )(h, w, *side)
    return outs[0], tuple(outs[1:])


def _attn_kernel(qt_ref, k_ref, vt_ref, bias_ref, o_ref):
    i = pl.program_id(1)
    row = lax.broadcasted_iota(jnp.int32, (LANES, QBLK), 0)
    upper = row >= A_HEAD_DIM
    n_kt = KBAND // LANES
    kt_seq = [i + kt - KPAD // LANES for kt in range(n_kt)]
    kt_clamped = [jnp.maximum(j, 0) for j in kt_seq]
    for hp in range(A_WIDTH // LANES):
        pair = slice(hp * LANES, (hp + 1) * LANES)
        qt = qt_ref[0, pair, :] * (A_HEAD_DIM ** -0.5)
        kps = [k_ref[0, pl.ds(pl.multiple_of(j * LANES, LANES), LANES), pair] for j in kt_clamped]
        outs = []
        for s in range(2):
            h = 2 * hp + s
            qm = jnp.where(upper == bool(s), qt, jnp.zeros_like(qt))
            tiles = []
            for kt in range(n_kt):
                t = jnp.dot(kps[kt], qm, preferred_element_type=f32) + bias_ref[h, kt]
                tiles.append(jnp.where(kt_seq[kt] >= 0, t, -1e30))
            m = tiles[0]
            for t in tiles[1:]:
                m = jnp.maximum(m, t)
            m = jnp.max(m, axis=0, keepdims=True)
            l = None
            o = None
            for kt in range(n_kt):
                p = jnp.exp(tiles[kt] - m)
                ls = jnp.sum(p, axis=0, keepdims=True)
                vt = vt_ref[0, kt_clamped[kt], h * A_HEAD_DIM:(h + 1) * A_HEAD_DIM, :]
                part = jnp.dot(vt, p.astype(bf16), preferred_element_type=f32)
                l = ls if l is None else l + ls
                o = part if o is None else o + part
            outs.append(o / l)
        o_ref[0, :, pair] = jnp.concatenate(outs, axis=0).T.astype(o_ref.dtype)


def _attention(qt, proj3, vt, bias_t):
    b, s, _ = proj3.shape
    return pl.pallas_call(
        _attn_kernel,
        grid=(b, s // QBLK),
        in_specs=[pl.BlockSpec((1, A_WIDTH, QBLK), lambda bi, i: (bi, 0, i)),
                  pl.BlockSpec((1, s, A_WIDTH), lambda bi, i: (bi, 0, COL_KA // A_WIDTH)),
                  pl.BlockSpec((1, s // LANES, A_WIDTH, LANES), lambda bi, i: (bi, 0, 0, 0)),
                  _const_spec((A_HEADS, KBAND // LANES, LANES, QBLK))],
        out_specs=pl.BlockSpec((1, QBLK, A_WIDTH), lambda bi, i: (bi, i, 0)),
        out_shape=jax.ShapeDtypeStruct((b, s, A_WIDTH), bf16),
        compiler_params=_params("parallel", "arbitrary"),
        name="chunk_attention",
    )(qt, proj3, vt, bias_t)


def _attn_bias(rel_bias):
    h = rel_bias.shape[0]
    r = np.arange(QBLK)[:, None]
    k = np.arange(KBAND)[None, :]
    visible = np.where(r < CHUNK, k < KBAND - CHUNK, k >= CHUNK)
    span = QBLK + KBAND - 1
    n_far = span - 2 * REL_CLIP
    line = jnp.concatenate([jnp.broadcast_to(rel_bias[:, -1:], (h, n_far)), rel_bias[:, :0:-1]], axis=1)
    line = jnp.pad(line.astype(f32), ((0, 0), (0, 1)))
    skew = jnp.tile(line, (1, QBLK))[:, :QBLK * span].reshape(h, QBLK, span)
    bias = jnp.where(visible[None], skew[:, :, QBLK - 1:], -1e30)
    return jnp.swapaxes(bias, 1, 2).reshape(h, KBAND // LANES, LANES, QBLK)


def _gla_kernel(q_ref, k_ref, v_ref, gd_ref, r_ref, wup_ref, b_ref, gn_ref, o_ref, state):
    @pl.when(pl.program_id(1) == 0)
    def _():
        state[...] = jnp.zeros_like(state)

    nb, tq = q_ref.shape[0], q_ref.shape[1]
    tn = (((0,), (0,)), ((), ()))
    nt = (((1,), (1,)), ((), ()))
    row = lax.broadcasted_iota(jnp.int32, (tq, tq), 0)
    col = lax.broadcasted_iota(jnp.int32, (tq, tq), 1)
    later = jnp.logical_and(col > row, col // CHUNK == row // CHUNK)
    later = jnp.where(later, 1.0, 0.0).astype(bf16)
    log_a, kdec = [], []
    for bb in range(nb):
        z = jnp.dot(gd_ref[bb], wup_ref[...], preferred_element_type=f32) + b_ref[...]
        la = (jnp.minimum(z, 0.0) - jnp.log(1.0 + jnp.exp(-jnp.abs(z)))) / GATE_TEMP
        la_hi = la.astype(bf16)
        la_lo = (la - la_hi.astype(f32)).astype(bf16)
        rev = (jnp.dot(later, la_hi, preferred_element_type=f32)
               + jnp.dot(later, la_lo, preferred_element_type=f32))
        log_a.append(la)
        kdec.append((k_ref[bb].astype(f32) * jnp.exp(rev)).astype(bf16))
    hrow = lax.broadcasted_iota(jnp.int32, (B_HEADS * CHUNK, B_K_WIDTH), 0) // CHUNK
    hlane = lax.broadcasted_iota(jnp.int32, (B_HEADS * CHUNK, B_K_WIDTH), 1) // B_KEY_DIM
    own = hrow == hlane
    st = [state[bb] for bb in range(nb)]
    for c in range(tq // CHUNK):
        rows = slice(c * CHUNK, (c + 1) * CHUNK)
        for bb in range(nb):
            decay = jnp.exp(jnp.sum(log_a[bb][rows], axis=0, keepdims=True))
            kd = kdec[bb][rows]
            kd4 = jnp.where(own, jnp.concatenate([kd] * B_HEADS, axis=0), jnp.zeros((), bf16))
            v4 = jnp.concatenate([v_ref[bb, rows, h * B_VAL_DIM:(h + 1) * B_VAL_DIM] for h in range(B_HEADS)],
                                 axis=0)
            d_state = lax.dot_general(v4, kd4, tn, preferred_element_type=f32)
            st[bb] = decay * st[bb] + d_state
            qc = q_ref[bb, rows, :] * (B_KEY_DIM ** -0.5)
            q4 = jnp.where(own, jnp.concatenate([qc] * B_HEADS, axis=0), jnp.zeros((), bf16))
            o4 = lax.dot_general(q4, st[bb].astype(bf16), nt, preferred_element_type=f32)
            for h in range(B_HEADS):
                vcols = slice(h * B_VAL_DIM, (h + 1) * B_VAL_DIM)
                o = _rms(o4[h * CHUNK:(h + 1) * CHUNK], gn_ref[:, vcols])
                r = r_ref[bb, rows, vcols].astype(f32)
                o_ref[bb, rows, vcols] = (o * _silu(r)).astype(o_ref.dtype)
    for bb in range(nb):
        state[bb] = st[bb]


def _gla(proj3, gd3, w_up, b_gate, gla_norm, tq=256, nb=GLA_NB):
    b, s, _ = proj3.shape
    return pl.pallas_call(
        _gla_kernel,
        grid=(b // nb, s // tq),
        in_specs=[pl.BlockSpec((nb, tq, B_K_WIDTH), lambda bi, i: (bi, i, COL_QB // B_K_WIDTH)),
                  pl.BlockSpec((nb, tq, B_K_WIDTH), lambda bi, i: (bi, i, COL_KB // B_K_WIDTH)),
                  pl.BlockSpec((nb, tq, B_V_WIDTH), lambda bi, i: (bi, i, COL_VB // B_V_WIDTH)),
                  pl.BlockSpec((nb, tq, LANES), lambda bi, i: (bi, i, 0)),
                  pl.BlockSpec((nb, tq, B_V_WIDTH), lambda bi, i: (bi, i, COL_R // B_V_WIDTH)),
                  _const_spec((LANES, B_K_WIDTH)),
                  _const_spec((1, B_K_WIDTH)),
                  _const_spec((1, B_V_WIDTH))],
        out_specs=pl.BlockSpec((nb, tq, B_V_WIDTH), lambda bi, i: (bi, i, 0)),
        out_shape=jax.ShapeDtypeStruct((b, s, B_V_WIDTH), bf16),
        scratch_shapes=[pltpu.VMEM((nb, B_VAL_DIM, B_K_WIDTH), f32)],
        compiler_params=_params("parallel", "arbitrary"),
        name="gla",
    )(proj3, proj3, proj3, gd3, proj3, w_up, b_gate, gla_norm)


def _route_top2(h, rw_ref):
    logits = jnp.dot(h.astype(bf16), rw_ref[...], preferred_element_type=f32)
    lane = lax.broadcasted_iota(jnp.int32, logits.shape, 1).astype(f32)
    neg = -jnp.inf
    lg = jnp.where(lane < N_EXPERTS, logits, neg)
    m1 = jnp.max(lg, axis=-1, keepdims=True)
    i1 = jnp.min(jnp.where(lg == m1, lane, float(LANES)), axis=-1, keepdims=True)
    lg2 = jnp.where(lane == i1, neg, lg)
    m2 = jnp.max(lg2, axis=-1, keepdims=True)
    i2 = jnp.min(jnp.where(lg2 == m2, lane, float(LANES)), axis=-1, keepdims=True)
    e2 = jnp.exp(m2 - m1)
    w1 = 1.0 / (1.0 + e2)
    w2 = e2 / (1.0 + e2)
    return jnp.where(lane == 0, i1, jnp.where(lane == 1, i2, jnp.where(lane == 2, w1, jnp.where(lane == 3, w2, 0.0))))


def _mix_out_kernel(*refs, routed):
    if routed:
        (ya_ref, yb_ref, ga_ref, gb_ref, x_ref, wa_ref, wb_ref, wo_ref, gpost_ref, gpre_ref, rw_ref,
         xo_ref, ho_ref, route_ref) = refs
    else:
        (ya_ref, yb_ref, ga_ref, gb_ref, x_ref, wa_ref, wb_ref, wo_ref, gpost_ref, gpre_ref,
         xo_ref, ho_ref) = refs
    a = jnp.dot(ya_ref[...], wa_ref[...], preferred_element_type=f32)
    b = jnp.dot(yb_ref[...], wb_ref[...], preferred_element_type=f32)
    merged = _sigmoid(ga_ref[...].astype(f32)) * a + _sigmoid(gb_ref[...].astype(f32)) * b
    y = jnp.dot(merged.astype(bf16), wo_ref[...], preferred_element_type=f32)
    x1 = x_ref[...] + _rms(y, gpost_ref[...])
    h = _rms(x1, gpre_ref[...])
    xo_ref[...] = x1
    if routed:
        _store_row_tiles(ho_ref, h)
        route_ref[...] = _route_top2(h, rw_ref)
    else:
        ho_ref[...] = h.astype(ho_ref.dtype)


def _mix_out(ya, yb, proj, x, wa, wb, wo, g_post, g_pre, router_w=None, tm=1024):
    t = x.shape[0]
    routed = router_w is not None
    row = lambda i: (i, 0)
    in_specs = [pl.BlockSpec((tm, A_WIDTH), row),
                pl.BlockSpec((tm, B_V_WIDTH), row),
                pl.BlockSpec((tm, D_MODEL), lambda i: (i, COL_GA // D_MODEL)),
                pl.BlockSpec((tm, D_MODEL), lambda i: (i, COL_GB // D_MODEL)),
                pl.BlockSpec((tm, D_MODEL), row),
                _const_spec((A_WIDTH, D_MODEL)),
                _const_spec((B_V_WIDTH, D_MODEL)),
                _const_spec((D_MODEL, D_MODEL)),
                _const_spec((1, D_MODEL)),
                _const_spec((1, D_MODEL))]
    args = [ya, yb, proj, proj, x, wa, wb, wo, g_post, g_pre]
    if routed:
        out_specs = [pl.BlockSpec((tm, D_MODEL), row), pl.BlockSpec((tm * ROW_TILE, LANES), row)]
        out_shape = [jax.ShapeDtypeStruct((t, D_MODEL), f32), jax.ShapeDtypeStruct((t * ROW_TILE, LANES), f32)]
    else:
        out_specs = [pl.BlockSpec((tm, D_MODEL), row), pl.BlockSpec((tm, D_MODEL), row)]
        out_shape = [jax.ShapeDtypeStruct((t, D_MODEL), f32), jax.ShapeDtypeStruct((t, D_MODEL), bf16)]
    if routed:
        in_specs.append(_const_spec((D_MODEL, LANES)))
        args.append(router_w)
        out_specs.append(pl.BlockSpec((tm, LANES), row))
        out_shape.append(jax.ShapeDtypeStruct((t, LANES), f32))
    return pl.pallas_call(
        functools.partial(_mix_out_kernel, routed=routed),
        grid=(t // tm,),
        in_specs=in_specs,
        out_specs=out_specs,
        out_shape=out_shape,
        compiler_params=_params("parallel"),
        name="mix_out",
    )(*args)


def _post_ffn(y, x1, p, gpost_ref, wpp_ref, wpg_ref, gple_ref):
    x2 = x1 + _rms(y, gpost_ref[...])
    e = jnp.dot(p.astype(bf16), wpp_ref[...], preferred_element_type=f32)
    e = e * _sigmoid(jnp.dot(x2.astype(bf16), wpg_ref[...], preferred_element_type=f32))
    return x2 + _rms(e, gple_ref[...])


def _swiglu_chunks(x, wg_ref, wu_ref, wd_ref, lead, width):
    acc = None
    for c0, c1 in _chunks(width, FF_CHUNK):
        g = jnp.dot(x, wg_ref[lead + (slice(None), slice(c0, c1))], preferred_element_type=f32)
        u = jnp.dot(x, wu_ref[lead + (slice(None), slice(c0, c1))], preferred_element_type=f32)
        act = (_silu(g) * u).astype(bf16)
        part = jnp.dot(act, wd_ref[lead + (slice(c0, c1), slice(None))], preferred_element_type=f32)
        acc = part if acc is None else acc + part
    return acc


def _dense_ffn_kernel(*refs, n_side):
    (h_ref, x_ref, p_ref, wg_ref, wu_ref, wd_ref, gpost_ref, wpp_ref, wpg_ref, gple_ref) = refs[:10]
    o_ref = refs[10 + n_side]
    finish = _side_cast(pl.program_id(0), pl.num_programs(0), refs[10:10 + n_side],
                        refs[11 + n_side:11 + 2 * n_side], refs[11 + 2 * n_side:])
    y = _swiglu_chunks(h_ref[...], wg_ref, wu_ref, wd_ref, (), wg_ref.shape[1])
    o_ref[...] = _post_ffn(y, x_ref[...], p_ref[...], gpost_ref, wpp_ref, wpg_ref, gple_ref)
    finish()


def _dense_ffn(h, x1, p, p_blk, wg, wu, wd, g_post, wpp, wpg, g_ple, side=(), tm=FFN_TM):
    t = x1.shape[0]
    ff = wg.shape[1]
    steps = t // tm
    row = lambda i: (i, 0)
    hbm = pl.BlockSpec(memory_space=pl.ANY)
    n_side = len(side)
    outs = pl.pallas_call(
        functools.partial(_dense_ffn_kernel, n_side=n_side),
        grid=(steps,),
        in_specs=[pl.BlockSpec((tm, D_MODEL), row),
                  pl.BlockSpec((tm, D_MODEL), row),
                  pl.BlockSpec((tm, PLE_DIM), lambda i: (i + p_blk, 0)),
                  _const_spec((D_MODEL, ff)),
                  _const_spec((D_MODEL, ff)),
                  _const_spec((ff, D_MODEL)),
                  _const_spec((1, D_MODEL)),
                  _const_spec((PLE_DIM, D_MODEL)),
                  _const_spec((D_MODEL, D_MODEL)),
                  _const_spec((1, D_MODEL))] + [hbm] * n_side,
        out_specs=[pl.BlockSpec((tm, D_MODEL), row)] + [hbm] * n_side,
        out_shape=[jax.ShapeDtypeStruct((t, D_MODEL), f32)] + [jax.ShapeDtypeStruct(m.shape, bf16) for m in side],
        scratch_shapes=_side_cast_scratch(side, steps),
        compiler_params=pltpu.CompilerParams(dimension_semantics=("arbitrary",), vmem_limit_bytes=VMEM_LIMIT_BIG),
        name="dense_ffn",
    )(h, x1, p, wg, wu, wd, g_post, wpp, wpg, g_ple, *side)
    return outs[0], tuple(outs[1:])


def _scatter_kernel(fill_ref, pos_ref, h_ref, xs_out, zbuf, sem, zsem):
    rows = h_ref.shape[0] // ROW_TILE

    @pl.when(pl.program_id(0) == 0)
    def _():
        zbuf[...] = jnp.zeros_like(zbuf)

        def zero_rows(first_row):
            start = pl.multiple_of(first_row * ROW_TILE, ROW_TILE)
            fill = pltpu.make_async_copy(zbuf, xs_out.at[pl.ds(start, MOE_TM * ROW_TILE)], zsem)
            fill.start()
            fill.wait()

        for e in range(N_EXPERTS):
            zero_rows(fill_ref[e])

        def zero_tile(tile, c):
            zero_rows(tile * MOE_TM)
            return c

        lax.fori_loop(fill_ref[N_EXPERTS], xs_out.shape[0] // (MOE_TM * ROW_TILE), zero_tile, 0)

    def issue(r, c):
        for k in range(2):
            pltpu.make_async_copy(_tile_rows(h_ref, r), _tile_rows(xs_out, pos_ref[2 * r + k]),
                                  sem.at[k]).start(priority=k)
        return c

    lax.fori_loop(0, rows, issue, 0, unroll=ISSUE_UNROLL)
    for k in range(2):
        pltpu.make_async_copy(h_ref, xs_out.at[pl.ds(0, rows * ROW_TILE)], sem.at[k]).wait()


def _scatter_rows(fill_plan, pos, h, n_rows):
    t = h.shape[0] // ROW_TILE
    return pl.pallas_call(
        _scatter_kernel,
        grid_spec=pltpu.PrefetchScalarGridSpec(
            num_scalar_prefetch=1,
            grid=(t // ROW_BLK,),
            in_specs=[pl.BlockSpec((2 * ROW_BLK,), lambda i, fill: (i,), memory_space=pltpu.SMEM),
                      pl.BlockSpec((ROW_BLK * ROW_TILE, LANES), lambda i, fill: (i, 0))],
            out_specs=pl.BlockSpec(memory_space=pl.ANY),
            scratch_shapes=[pltpu.VMEM((MOE_TM * ROW_TILE, LANES), f32),
                            pltpu.SemaphoreType.DMA((2,)),
                            pltpu.SemaphoreType.DMA(())]),
        out_shape=jax.ShapeDtypeStruct((n_rows * ROW_TILE, LANES), f32),
        compiler_params=_params("arbitrary"),
        name="moe_scatter",
    )(fill_plan, pos, h)


def _expert_kernel(te_ref, nu_ref, fill_ref, x_ref, wg_ref, wu_ref, wd_ref, o_ref, acc_ref):
    del te_ref
    i = pl.program_id(0)
    f = pl.program_id(1)
    used = i < nu_ref[0]
    half = MOE_TM // 2
    sparse = fill_ref[i] <= half
    last = f == pl.num_programs(1) - 1

    def run(rows):
        x = _load_row_tiles(x_ref, rows).astype(bf16)
        y = _swiglu_chunks(x, wg_ref, wu_ref, wd_ref, (0,), wg_ref.shape[2])

        @pl.when(f == 0)
        def _():
            acc_ref[0:rows] = y

        @pl.when(f > 0)
        def _():
            acc_ref[0:rows] += y

        @pl.when(last)
        def _():
            _store_row_tiles(o_ref, acc_ref[0:rows])

    @pl.when(jnp.logical_and(used, jnp.logical_not(sparse)))
    def _():
        run(MOE_TM)

    @pl.when(jnp.logical_and(used, sparse))
    def _():
        run(half)

        @pl.when(last)
        def _():
            o_ref[half * ROW_TILE:, :] = jnp.zeros(((MOE_TM - half) * ROW_TILE, LANES), o_ref.dtype)

    @pl.when(jnp.logical_and(last, jnp.logical_not(used)))
    def _():
        o_ref[...] = jnp.zeros_like(o_ref)


def _expert_ffn(tile_expert, n_used, tile_fill, xs, wg, wu, wd, tf=1792):
    n_rows = xs.shape[0] // ROW_TILE
    ff = wg.shape[2]
    nf = ff // tf

    def f_idx(i, f, nu):
        return jnp.where(i < nu[0], f, nf - 1)

    return pl.pallas_call(
        _expert_kernel,
        grid_spec=pltpu.PrefetchScalarGridSpec(
            num_scalar_prefetch=3,
            grid=(n_rows // MOE_TM, nf),
            in_specs=[pl.BlockSpec((MOE_TM * ROW_TILE, LANES), lambda i, f, te, nu, fill: (i, 0)),
                      pl.BlockSpec((1, D_MODEL, tf), lambda i, f, te, nu, fill: (te[i], 0, f_idx(i, f, nu))),
                      pl.BlockSpec((1, D_MODEL, tf), lambda i, f, te, nu, fill: (te[i], 0, f_idx(i, f, nu))),
                      pl.BlockSpec((1, tf, D_MODEL), lambda i, f, te, nu, fill: (te[i], f_idx(i, f, nu), 0))],
            out_specs=pl.BlockSpec((MOE_TM * ROW_TILE, LANES), lambda i, f, te, nu, fill: (i, 0)),
            scratch_shapes=[pltpu.VMEM((MOE_TM, D_MODEL), f32)]),
        out_shape=jax.ShapeDtypeStruct((n_rows * ROW_TILE, LANES), f32),
        compiler_params=_params("arbitrary", "arbitrary"),
        name="moe_experts",
    )(tile_expert, n_used, tile_fill, xs, wg, wu, wd)


def _combine_kernel(pos0_ref, pos_next_ref, ys_hbm, route_ref, x_ref, p_ref, gpost_ref, wpp_ref, wpg_ref, gple_ref,
                    o_ref, buf, sem):
    i = pl.program_id(0)
    rows = x_ref.shape[0]
    slot = i % 2

    def fetch(pos_ref, into):
        def issue(r, c):
            for k in range(2):
                pltpu.make_async_copy(_tile_rows(ys_hbm, pos_ref[2 * r + k]), _tile_rows(buf.at[into, k], r),
                                      sem.at[into, k]).start(priority=k)
            return c

        lax.fori_loop(0, rows, issue, 0, unroll=ISSUE_UNROLL)

    @pl.when(i == 0)
    def _():
        fetch(pos0_ref, 0)

    @pl.when(i + 1 < pl.num_programs(0))
    def _():
        fetch(pos_next_ref, 1 - slot)

    for k in range(2):
        pltpu.make_async_copy(ys_hbm.at[pl.ds(0, rows * ROW_TILE)], buf.at[slot, k], sem.at[slot, k]).wait()
    route = route_ref[...]
    y = (route[:, 2:3] * _load_row_tiles(buf.at[slot, 0], rows)
         + route[:, 3:4] * _load_row_tiles(buf.at[slot, 1], rows))
    o_ref[...] = _post_ffn(y, x_ref[...], p_ref[...], gpost_ref, wpp_ref, wpg_ref, gple_ref)


def _combine(pos, ys, route, x1, p, p_blk, g_post, wpp, wpg, g_ple):
    t = x1.shape[0]
    n_blk = t // ROW_BLK
    row = lambda i: (i, 0)
    smem = functools.partial(pl.BlockSpec, (2 * ROW_BLK,), memory_space=pltpu.SMEM)
    return pl.pallas_call(
        _combine_kernel,
        grid=(n_blk,),
        in_specs=[smem(lambda i: (0,)),
                  smem(lambda i: (jnp.minimum(i + 1, n_blk - 1),)),
                  pl.BlockSpec(memory_space=pl.ANY),
                  pl.BlockSpec((ROW_BLK, LANES), row),
                  pl.BlockSpec((ROW_BLK, D_MODEL), row),
                  pl.BlockSpec((ROW_BLK, PLE_DIM), lambda i: (i + p_blk, 0)),
                  _const_spec((1, D_MODEL)),
                  _const_spec((PLE_DIM, D_MODEL)),
                  _const_spec((D_MODEL, D_MODEL)),
                  _const_spec((1, D_MODEL))],
        out_specs=pl.BlockSpec((ROW_BLK, D_MODEL), row),
        out_shape=jax.ShapeDtypeStruct((t, D_MODEL), f32),
        scratch_shapes=[pltpu.VMEM((2, 2, ROW_BLK * ROW_TILE, LANES), f32),
                        pltpu.SemaphoreType.DMA((2, 2))],
        compiler_params=_params("arbitrary"),
        name="moe_combine",
    )(pos, pos, ys, route, x1, p, g_post, wpp, wpg, g_ple)


def _route_slots(route, n_tiles):
    ids = route[:, 0:2].astype(jnp.int32).reshape(-1)
    onehot = (ids[None, :] == jnp.arange(N_EXPERTS, dtype=jnp.int32)[:, None]).astype(jnp.int32)
    csum = jnp.cumsum(onehot, axis=1)
    rank = jnp.sum(onehot * csum, axis=0) - 1
    counts = csum[:, -1]
    padded = ((counts + MOE_TM - 1) // MOE_TM) * MOE_TM
    ends = jnp.cumsum(padded)
    pos = (ends - padded)[ids] + rank
    tile_start = jnp.arange(n_tiles, dtype=jnp.int32) * MOE_TM
    tile_expert = jnp.minimum(jnp.sum((tile_start[:, None] >= ends[None, :]).astype(jnp.int32), axis=1),
                              N_EXPERTS - 1)
    n_used = (ends[-1] // MOE_TM).reshape(1)
    fill_plan = jnp.concatenate([ends - padded + counts, n_used])
    tile_fill = jnp.clip((ends - padded + counts)[tile_expert] - tile_start, 0, MOE_TM)
    return (pos.astype(jnp.int32), tile_expert.astype(jnp.int32), n_used.astype(jnp.int32),
            fill_plan.astype(jnp.int32), tile_fill.astype(jnp.int32))


def _moe(h, route, x1, p, p_blk, wg, wu, wd, g_post, wpp, wpg, g_ple):
    t = x1.shape[0]
    n_tiles = 2 * t // MOE_TM + N_EXPERTS
    pos, tile_expert, n_used, fill_plan, tile_fill = _route_slots(route, n_tiles)
    xs = _scatter_rows(fill_plan, pos, h, n_tiles * MOE_TM)
    ys = _expert_ffn(tile_expert, n_used, tile_fill, xs, wg, wu, wd)
    return _combine(pos, ys, route, x1, p, p_blk, g_post, wpp, wpg, g_ple)


def _prep_w_in(w):
    main = jnp.concatenate([w[:, SRC_KA:SRC_VA], w[:, SRC_QB:SRC_GD], w[:, SRC_R:SRC_END]], axis=1)
    wq_t = w[:, SRC_QA:SRC_KA].T
    wv_t = w[:, SRC_VA:SRC_QB].T
    w_gd = jnp.pad(w[:, SRC_GD:SRC_R], ((0, 0), (0, LANES - GATE_RANK)))
    return main.astype(bf16), wq_t.astype(bf16), wv_t.astype(bf16), w_gd.astype(bf16)


def kernel(x, p, w_in, rel_bias, w_gla_gate_up, b_gla_gate, gla_norm, w_branch_a, w_branch_b, w_out, norm_mix_pre, norm_mix_post, norm_ffn_pre, norm_ffn_post, ffn_w_gate, ffn_w_up, ffn_w_down, router_w, moe_w_gate, moe_w_up, moe_w_down, ple_w_proj, ple_w_gate, ple_norm):
    b, s, d = x.shape
    t = b * s
    depth = w_in.shape[0]
    x = x.reshape(t, d)
    vec = lambda a: a.reshape(1, -1).astype(f32)
    moe_bf16 = None
    p = p.reshape(depth * t, PLE_DIM)
    for i in range(depth):
        w_main, wq_t, wv_t, w_gd = _prep_w_in(w_in[i])
        h_in, qt, vt, gd = _qv_proj(x, vec(norm_mix_pre[i]), wq_t, wv_t, w_gd, b, s)
        routed = i % 2 == 1
        nxt = (i + 1) // 2
        flat = lambda w: w[nxt].reshape(-1, w.shape[-1])
        ahead = (not routed) and i + 1 < depth
        proj, _ = _in_proj(h_in, w_main)
        proj3 = proj.reshape(b, s, MAIN_WIDTH)
        ya = _attention(qt, proj3, vt, _attn_bias(rel_bias[i])).reshape(t, A_WIDTH)
        w_up = jnp.pad(w_gla_gate_up[i], ((0, LANES - GATE_RANK), (0, 0))).astype(bf16)
        yb = _gla(proj3, gd.reshape(b, s, LANES), w_up, vec(b_gla_gate[i]), vec(gla_norm[i])).reshape(t, B_V_WIDTH)
        j = i // 2
        rw = jnp.pad(router_w[j], ((0, 0), (0, LANES - N_EXPERTS))).astype(bf16) if routed else None
        outs = _mix_out(ya, yb, proj, x, w_branch_a[i].astype(bf16), w_branch_b[i].astype(bf16),
                        w_out[i].astype(bf16), vec(norm_mix_post[i]), vec(norm_ffn_pre[i]), rw)
        tail = (vec(norm_ffn_post[i]), ple_w_proj[i].astype(bf16), ple_w_gate[i].astype(bf16), vec(ple_norm[i]))
        if routed:
            x1, h, route = outs
            if moe_bf16 is None:
                moe_bf16 = tuple(w[j].astype(bf16) for w in (moe_w_gate, moe_w_up, moe_w_down))
            x = _moe(h, route, x1, p, i * t // ROW_BLK, *moe_bf16, *tail)
            moe_bf16 = None
        else:
            x1, h = outs
            x, cast = _dense_ffn(h, x1, p, i * t // FFN_TM, ffn_w_gate[j].astype(bf16), ffn_w_up[j].astype(bf16),
                                 ffn_w_down[j].astype(bf16), *tail,
                                 side=(flat(moe_w_gate), flat(moe_w_up)) if ahead else ())
            if ahead:
                moe_bf16 = tuple(c.reshape(w.shape[1:]) for c, w in zip(cast, (moe_w_gate, moe_w_up)))
                moe_bf16 += (moe_w_down[nxt].astype(bf16),)
    return x.reshape(b, s, d)
```
